```python
import jax, jax.numpy as jnp
from jax import lax
import numpy as np

D_MODEL = 1024
BATCH = 8
SEQ = 2048
DEPTH = 1
DEC_BATCH = 128
DEC_SEQ = 8
PAST_LEN = 16384
PAGE_SIZE = 128

RET_HEADS = 4
RET_DK = 128
RET_DV = 128
RET_W = RET_HEADS * RET_DV
GDN_HEADS = 4
GDN_DK = 128
GDN_DV = 128
GDN_W = GDN_HEADS * GDN_DV
MIX_W = RET_W + GDN_W
GDN_CONV_CH = 2 * GDN_HEADS * GDN_DK + GDN_HEADS * GDN_DV
CONV_W = 4
CHUNK = 64
D_FF = 4 * D_MODEL
ROPE_BASE = 10000.0
EPS = 1e-6
IN_SIZES = (RET_HEADS * RET_DK, RET_HEADS * RET_DK, RET_W, RET_W, GDN_CONV_CH, GDN_W, GDN_HEADS, GDN_HEADS)
IN_W = 2 * RET_HEADS * RET_DK + 2 * RET_W + GDN_CONV_CH + GDN_W + 2 * GDN_HEADS

kernel_name = "hybrid_retention_gated_delta_step"


def _rmsnorm(x, w):
    x32 = x.astype(jnp.float32)
    y = x32 * lax.rsqrt(jnp.mean(x32 * x32, axis=-1, keepdims=True) + EPS)
    return (y * w.astype(jnp.float32)).astype(x.dtype)


def _l2norm(x):
    return x * lax.rsqrt(jnp.sum(x * x, axis=-1, keepdims=True) + EPS)


def _rope(x, pos):
    d = x.shape[-1]
    inv_freq = ROPE_BASE ** (-jnp.arange(0, d, 2, dtype=jnp.float32) / d)
    ang = pos[:, None] * inv_freq[None, :]
    cos = jnp.cos(ang)[None, :, None, :]
    sin = jnp.sin(ang)[None, :, None, :]
    x1, x2 = x[..., : d // 2], x[..., d // 2:]
    return jnp.concatenate([x1 * cos - x2 * sin, x1 * sin + x2 * cos], axis=-1)


def _to_chunks(a, C):
    L = a.shape[2]
    N = -(-L // C)
    pad = N * C - L
    a = jnp.pad(a, [(0, 0), (0, 0), (0, pad)] + [(0, 0)] * (a.ndim - 3))
    a = a.reshape(a.shape[:2] + (N, C) + a.shape[3:])
    return jnp.moveaxis(a, 2, 0)


def _from_chunks(o, L):
    o = jnp.moveaxis(o, 0, 2)
    B, H, N, C, d = o.shape
    return o.reshape(B, H, N * C, d)[:, :, :L]


def _retention(q, k, v, logd, S0):
    L = q.shape[2]
    C = min(CHUNK, L)
    causal = jnp.tril(jnp.ones((C, C), dtype=bool))

    def step(S, inp):
        qi, ki, vi, gi = inp
        cum = jnp.cumsum(gi, axis=-1)
        dmat = jnp.exp(jnp.where(causal, cum[..., :, None] - cum[..., None, :], -jnp.inf))
        scores = jnp.einsum('bhid,bhjd->bhij', qi, ki) * dmat
        o = (jnp.einsum('bhcd,bhde->bhce', qi * jnp.exp(cum)[..., None], S)
             + jnp.einsum('bhij,bhje->bhie', scores, vi))
        tail = jnp.exp(cum[..., -1:] - cum)
        S = (jnp.exp(cum[..., -1])[..., None, None] * S
             + jnp.einsum('bhcd,bhce->bhde', ki * tail[..., None], vi))
        return S, o

    S, o = lax.scan(step, S0, (_to_chunks(q, C), _to_chunks(k, C), _to_chunks(v, C), _to_chunks(logd, C)))
    return _from_chunks(o, L), S


def _gated_delta(q, k, v, g, beta, S0):
    L = q.shape[2]
    C = min(CHUNK, L)
    causal = jnp.tril(jnp.ones((C, C), dtype=bool))
    strict = jnp.tril(jnp.ones((C, C), dtype=bool), -1)
    eye = jnp.eye(C, dtype=jnp.float32)
    dk = q.shape[-1]

    def step(S, inp):
        qi, ki, vi, gi, bi = inp
        cum = jnp.cumsum(gi, axis=-1)
        dmat = jnp.exp(jnp.where(causal, cum[..., :, None] - cum[..., None, :], -jnp.inf))
        kk = jnp.einsum('bhid,bhjd->bhij', ki, ki)
        A = jnp.where(strict, bi[..., :, None] * kk * dmat, 0.0)
        rhs = jnp.concatenate([ki * (bi * jnp.exp(cum))[..., None], vi * bi[..., None]], axis=-1)
        sol = lax.linalg.triangular_solve(eye + A, rhs, left_side=True, lower=True, unit_diagonal=True)
        w, u = sol[..., :dk], sol[..., dk:]
        v_new = u - jnp.einsum('bhcd,bhde->bhce', w, S)
        scores = jnp.einsum('bhid,bhjd->bhij', qi, ki) * dmat
        o = (jnp.einsum('bhcd,bhde->bhce', qi * jnp.exp(cum)[..., None], S)
             + jnp.einsum('bhij,bhje->bhie', scores, v_new))
        tail = jnp.exp(cum[..., -1:] - cum)
        S = (jnp.exp(cum[..., -1])[..., None, None] * S
             + jnp.einsum('bhcd,bhce->bhde', ki * tail[..., None], v_new))
        return S, o

    xs = (_to_chunks(q, C), _to_chunks(k, C), _to_chunks(v, C), _to_chunks(g, C), _to_chunks(beta, C))
    S, o = lax.scan(step, S0, xs)
    return _from_chunks(o, L), S


def _hybrid_mixer(xn, pos_offset, s_ret, s_gdn, s_conv, w_in, conv_w, A_log, dt_bias,
                  ret_norm_w, gdn_norm_w, w_out):
    f32 = jnp.float32
    B, L, _ = xn.shape
    proj = jnp.einsum('bld,de->ble', xn, w_in).astype(f32)
    offs = np.cumsum(IN_SIZES)[:-1].tolist()
    q_r, k_r, v_r, gate_r, qkv_g, z_g, a_g, b_g = jnp.split(proj, offs, axis=-1)

    pos = jnp.arange(L, dtype=f32) + jnp.asarray(pos_offset, f32)
    q_r = _rope(q_r.reshape(B, L, RET_HEADS, RET_DK), pos)
    k_r = _rope(k_r.reshape(B, L, RET_HEADS, RET_DK), pos) * (RET_DK ** -0.5)
    v_r = v_r.reshape(B, L, RET_HEADS, RET_DV)
    log_gamma = jnp.log(1.0 - 2.0 ** (-5.0 - jnp.arange(RET_HEADS, dtype=f32)))
    logd = jnp.broadcast_to(log_gamma[None, :, None], (B, RET_HEADS, L))
    o_r, S_r = _retention(q_r.transpose(0, 2, 1, 3), k_r.transpose(0, 2, 1, 3),
                          v_r.transpose(0, 2, 1, 3), logd, s_ret.astype(f32))
    o_r = _rmsnorm(o_r.transpose(0, 2, 1, 3), ret_norm_w).reshape(B, L, RET_W) * jax.nn.silu(gate_r)

    full = jnp.concatenate([s_conv.astype(f32), qkv_g], axis=1)
    cw = conv_w.astype(f32)
    conv = full[:, 0:L] * cw[0]
    for t in range(1, CONV_W):
        conv = conv + full[:, t:t + L] * cw[t]
    conv = jax.nn.silu(conv)
    new_conv = full[:, -(CONV_W - 1):]
    nqk = GDN_HEADS * GDN_DK
    q_g = _l2norm(conv[..., :nqk].reshape(B, L, GDN_HEADS, GDN_DK)) * (GDN_DK ** -0.5)
    k_g = _l2norm(conv[..., nqk:2 * nqk].reshape(B, L, GDN_HEADS, GDN_DK))
    v_g = conv[..., 2 * nqk:].reshape(B, L, GDN_HEADS, GDN_DV)
    g = -jnp.exp(A_log.astype(f32)) * jax.nn.softplus(a_g + dt_bias.astype(f32))
    beta = jax.nn.sigmoid(b_g)
    o_g, S_g = _gated_delta(q_g.transpose(0, 2, 1, 3), k_g.transpose(0, 2, 1, 3),
                            v_g.transpose(0, 2, 1, 3), g.transpose(0, 2, 1),
                            beta.transpose(0, 2, 1), s_gdn.astype(f32))
    o_g = _rmsnorm(o_g.transpose(0, 2, 1, 3), gdn_norm_w).reshape(B, L, GDN_W) * jax.nn.silu(z_g)

    mix = jnp.concatenate([o_r, o_g], axis=-1).astype(w_out.dtype)
    y = jnp.einsum('ble,ed->bld', mix, w_out).astype(xn.dtype)
    return y, S_r, S_g, new_conv


def _layer(x, pos_offset, s_ret, s_gdn, s_conv, pre_mix_w, w_in, conv_w, A_log, dt_bias,
           ret_norm_w, gdn_norm_w, w_out, post_mix_w, pre_mlp_w, w_up, w_down, post_mlp_w):
    h = _rmsnorm(x, pre_mix_w)
    m, S_r, S_g, new_conv = _hybrid_mixer(h, pos_offset, s_ret, s_gdn, s_conv, w_in, conv_w, A_log,
                                          dt_bias, ret_norm_w, gdn_norm_w, w_out)
    x = x + _rmsnorm(m, post_mix_w)
    h = _rmsnorm(x, pre_mlp_w)
    f = jnp.square(jax.nn.relu(jnp.einsum('bld,df->blf', h, w_up)))
    f = jnp.einsum('blf,fd->bld', f, w_down)
    x = x + _rmsnorm(f, post_mlp_w)
    return x, S_r, S_g, new_conv


def setup_inputs(seed: int = 0) -> dict:
    key = jax.random.key(seed)
    ks = jax.random.split(key, 20)
    f32 = jnp.float32
    nrm = lambda k, s, sc: jax.random.normal(k, s, f32) * sc
    gain = lambda k, s: 1.0 + 0.02 * jax.random.normal(k, s, f32)
    dt = jnp.exp(jax.random.uniform(ks[17], (DEPTH, GDN_HEADS), f32, np.log(1e-3), np.log(1e-1)))
    return {
        "x_prompt": nrm(ks[0], (BATCH, SEQ, D_MODEL), 1.0),
        "x_sample": nrm(ks[1], (DEC_BATCH, DEC_SEQ, D_MODEL), 1.0),
        "state_ret": nrm(ks[2], (DEPTH, DEC_BATCH, RET_HEADS, RET_DK, RET_DV), 0.05),
        "state_gdn": nrm(ks[3], (DEPTH, DEC_BATCH, GDN_HEADS, GDN_DK, GDN_DV), 0.05),
        "state_conv": nrm(ks[4], (DEPTH, DEC_BATCH, CONV_W - 1, GDN_CONV_CH), 1.0),
        "pre_mix_w": gain(ks[5], (DEPTH, D_MODEL)),
        "w_in": nrm(ks[6], (DEPTH, D_MODEL, IN_W), D_MODEL ** -0.5),
        "conv_w": nrm(ks[7], (DEPTH, CONV_W, GDN_CONV_CH), CONV_W ** -0.5),
        "A_log": jnp.log(jax.random.uniform(ks[8], (DEPTH, GDN_HEADS), f32, 1.0, 16.0)),
        "dt_bias": dt + jnp.log(-jnp.expm1(-dt)),
        "ret_norm_w": gain(ks[9], (DEPTH, RET_DV)),
        "gdn_norm_w": gain(ks[10], (DEPTH, GDN_DV)),
        "w_out": nrm(ks[11], (DEPTH, MIX_W, D_MODEL), MIX_W ** -0.5),
        "post_mix_w": gain(ks[12], (DEPTH, D_MODEL)),
        "pre_mlp_w": gain(ks[13], (DEPTH, D_MODEL)),
        "w_up": nrm(ks[14], (DEPTH, D_MODEL, D_FF), D_MODEL ** -0.5),
        "w_down": nrm(ks[15], (DEPTH, D_FF, D_MODEL), D_FF ** -0.5),
        "post_mlp_w": gain(ks[16], (DEPTH, D_MODEL)),
    }


def reference(x_prompt, x_sample, state_ret, state_gdn, state_conv, pre_mix_w, w_in, conv_w, A_log,
              dt_bias, ret_norm_w, gdn_norm_w, w_out, post_mix_w, pre_mlp_w, w_up, w_down, post_mlp_w):
    f32 = jnp.float32
    B = x_prompt.shape[0]
    yp, ys = x_prompt, x_sample
    rp, gp, cp, rs, gs, cs = [], [], [], [], [], []
    for l in range(DEPTH):
        params = (pre_mix_w[l], w_in[l], conv_w[l], A_log[l], dt_bias[l], ret_norm_w[l], gdn_norm_w[l],
                  w_out[l], post_mix_w[l], pre_mlp_w[l], w_up[l], w_down[l], post_mlp_w[l])
        z_r = jnp.zeros((B, RET_HEADS, RET_DK, RET_DV), f32)
        z_g = jnp.zeros((B, GDN_HEADS, GDN_DK, GDN_DV), f32)
        z_c = jnp.zeros((B, CONV_W - 1, GDN_CONV_CH), f32)
        yp, s1, s2, s3 = _layer(yp, 0, z_r, z_g, z_c, *params)
        ys, t1, t2, t3 = _layer(ys, PAST_LEN, state_ret[l], state_gdn[l], state_conv[l], *params)
        rp.append(s1.astype(state_ret.dtype)); gp.append(s2.astype(state_gdn.dtype)); cp.append(s3.astype(state_conv.dtype))
        rs.append(t1.astype(state_ret.dtype)); gs.append(t2.astype(state_gdn.dtype)); cs.append(t3.astype(state_conv.dtype))
    return (yp, ys, jnp.stack(rp), jnp.stack(gp), jnp.stack(cp), jnp.stack(rs), jnp.stack(gs), jnp.stack(cs))
```

```python
import functools

import numpy as np
import jax
import jax.numpy as jnp
from jax import lax
from jax.experimental import pallas as pl
from jax.experimental.pallas import tpu as pltpu

D_MODEL = 1024
HEADS = 4
HEAD_DIM = 128
GROUP_W = HEADS * HEAD_DIM
CONV_CH = 3 * GROUP_W
CONV_W = 4
D_FF = 4 * D_MODEL
MAIN_W = 4 * GROUP_W + CONV_CH + GROUP_W
ROPE_BASE = 10000.0
EPS = 1e-6
PAST_LEN = 16384
REF_CHUNK = 64

ROWS = 64
BASE = 8
LANES = 128
VMEM_LIMIT = 52 * 1024 * 1024

OFF_RQ, OFF_RK, OFF_RV, OFF_RG = 0, GROUP_W, 2 * GROUP_W, 3 * GROUP_W
OFF_CONV = 4 * GROUP_W
OFF_Z = OFF_CONV + CONV_CH

F32 = jnp.float32
BF16 = jnp.bfloat16


def _bf(x):
    return x.astype(BF16)


def _dot(a, b):
    return jnp.dot(a, b, preferred_element_type=F32)


def _dot_nt(a, b):
    return lax.dot_general(a, b, (((1,), (1,)), ((), ())), preferred_element_type=F32)


def _split(x, n):
    if x.dtype == BF16:
        return [x]
    parts, r = [], x
    for i in range(n):
        p = r.astype(BF16)
        parts.append(p)
        if i + 1 < n:
            r = r - p.astype(F32)
    return parts


def _mm(a, b, pa=1, pb=1, nt=False):
    ap, bp = _split(a, pa), _split(b, pb)
    n = max(len(ap), len(bp))
    acc = None
    for i, x in enumerate(ap):
        for j, y in enumerate(bp):
            if i + j < n:
                t = _dot_nt(x, y) if nt else _dot(x, y)
                acc = t if acc is None else acc + t
    return acc


def _rms(x, w):
    return x * lax.rsqrt(jnp.mean(x * x, axis=-1, keepdims=True) + EPS) * w


def _silu(x):
    return x / (1.0 + jnp.exp(-x))


def _tri_inverse(a, group, row, col):
    eye = (row == col).astype(F32)
    mm = functools.partial(_mm, pa=2, pb=2)
    n8 = jnp.where((row // BASE) == (col // BASE), -a, 0.0)
    p2 = mm(n8, n8)
    e = n8 + p2 + mm(n8, p2)
    p4 = mm(p2, p2)
    e = e + p4 + mm(e, p4)
    t = e + eye
    s = BASE
    while s < group:
        off = ((row // (2 * s)) == (col // (2 * s))) & ((row // s) != (col // s))
        a_off = jnp.where(off, a, 0.0)
        t = t - mm(mm(t, a_off), t)
        s *= 2
    return t, eye


def _mixer_kernel(cfg, *refs):
    lb, group, prompt = cfg["lb"], cfg["group"], cfg["prompt"]
    ngroups = ROWS // group
    nchunk = lb // ROWS
    it = iter(refs)
    x_ref, prew_ref, wmain_ref, wab_ref, cos_ref, sin_ref = (next(it) for _ in range(6))
    dmat_ref, rtab_ref, convw_ref, pvec_ref, normw_ref = (next(it) for _ in range(5))
    if not prompt:
        sret_in, sgdn_in, st8_ref = (next(it) for _ in range(3))
    mix_ref, sret_out, sgdn_out, conv_out = (next(it) for _ in range(4))
    proj_s, qr_s, kr_s, qg_s, kg_s, vg_s, cum_s, tot_s, beta_s, o_s = (next(it) for _ in range(10))
    if prompt:
        sret_s, sgdn_s, prev8_s = (next(it) for _ in range(3))
        j = pl.program_id(1)

        @pl.when(j == 0)
        def _():
            sret_s[...] = jnp.zeros_like(sret_s)
            sgdn_s[...] = jnp.zeros_like(sgdn_s)
            prev8_s[...] = jnp.zeros_like(prev8_s)

    h = _bf(_rms(x_ref[...], prew_ref[...]))
    proj_s[...] = _dot(h, wmain_ref[...])
    ab = _dot(h, wab_ref[...])

    cos, sin = cos_ref[...], sin_ref[...]
    for hh in range(HEADS):
        sl = slice(hh * HEAD_DIM, (hh + 1) * HEAD_DIM)
        q = proj_s[:, OFF_RQ + hh * HEAD_DIM:OFF_RQ + (hh + 1) * HEAD_DIM]
        qr_s[:, sl] = q * cos + pltpu.roll(q, HEAD_DIM // 2, 1) * sin
        k = proj_s[:, OFF_RK + hh * HEAD_DIM:OFF_RK + (hh + 1) * HEAD_DIM]
        kr_s[:, sl] = (k * cos + pltpu.roll(k, HEAD_DIM // 2, 1) * sin) * (HEAD_DIM ** -0.5)

    t_in_group = lax.broadcasted_iota(jnp.int32, (lb, 1), 0) % group
    for part, dst in enumerate((qg_s, kg_s, vg_s)):
        c0 = OFF_CONV + part * GROUP_W
        xq = proj_s[:, c0:c0 + GROUP_W]
        cw = convw_ref[:, part * GROUP_W:(part + 1) * GROUP_W]
        acc = xq * cw[CONV_W - 1:CONV_W]
        if prompt:
            ext = jnp.concatenate([prev8_s[:, part * GROUP_W:(part + 1) * GROUP_W], xq], axis=0)
            for s in range(1, CONV_W):
                sh = pltpu.roll(ext, s, 0)[8:]
                acc = acc + sh * cw[CONV_W - 1 - s:CONV_W - s]
            prev8_s[:, part * GROUP_W:(part + 1) * GROUP_W] = xq[lb - 8:]
        else:
            st8 = st8_ref[:, part * GROUP_W:(part + 1) * GROUP_W]
            for s in range(1, CONV_W):
                sh = jnp.where(t_in_group < s, pltpu.roll(st8, lb - 8 + s, 0), pltpu.roll(xq, s, 0))
                acc = acc + sh * cw[CONV_W - 1 - s:CONV_W - s]
        conv = _silu(acc)
        for hh in range(HEADS):
            sl = slice(hh * HEAD_DIM, (hh + 1) * HEAD_DIM)
            c = conv[:, sl]
            if part < 2:
                c = c * lax.rsqrt(jnp.sum(c * c, axis=-1, keepdims=True) + EPS)
            if part == 0:
                c = c * (HEAD_DIM ** -0.5)
            dst[:, sl] = c
    if prompt:
        conv_out[...] = prev8_s[...]
    else:
        conv_out[...] = proj_s[:, OFF_CONV:OFF_CONV + CONV_CH]

    pv = pvec_ref[...]
    a_plus = ab + pv[1:2]
    softplus = jnp.maximum(a_plus, 0.0) + jnp.log1p(jnp.exp(-jnp.abs(a_plus)))
    g = -jnp.exp(pv[0:1]) * softplus
    beta_s[...] = 1.0 / (1.0 + jnp.exp(-ab))

    row = lax.broadcasted_iota(jnp.int32, (ROWS, ROWS), 0)
    col = lax.broadcasted_iota(jnp.int32, (ROWS, ROWS), 1)
    same = (row // group) == (col // group)
    causal = same & (row >= col)
    strict = same & (row > col)
    sum_mat = _bf(jnp.concatenate([causal.astype(F32), same.astype(F32)], axis=0))
    for c in range(nchunk):
        r = _mm(sum_mat, g[c * ROWS:(c + 1) * ROWS], 1, 3)
        cum_s[c * ROWS:(c + 1) * ROWS, :] = r[:ROWS]
        tot_s[c * ROWS:(c + 1) * ROWS, :] = r[ROWS:]
    sel = _bf((lax.broadcasted_iota(jnp.int32, (8, LANES), 0)
               == lax.broadcasted_iota(jnp.int32, (8, LANES), 1)).astype(F32))

    def chunk_body(c, carry):
        r0 = pl.multiple_of(c * ROWS, ROWS)
        rows = pl.ds(r0, ROWS)
        cumc = cum_s[rows, :]
        totc = tot_s[rows, :]
        betac = beta_s[rows, :]
        ecumc = jnp.exp(cumc)
        tailc = jnp.exp(totc - cumc)
        dtotc = jnp.exp(totc)
        cumrow = _mm(sel, cumc, 1, 3, nt=True)

        def state_refs(hh, gi):
            if prompt:
                return (sret_s.at[hh], sret_s.at[hh], sgdn_s.at[hh], sgdn_s.at[hh])
            n = c * ngroups + gi
            return (sret_in.at[n, hh], sret_out.at[n, hh], sgdn_in.at[n, hh], sgdn_out.at[n, hh])

        for hh in range(HEADS):
            sl = slice(hh * HEAD_DIM, (hh + 1) * HEAD_DIM)
            q = qr_s[rows, sl]
            k = kr_s[rows, sl]
            v = proj_s[rows, OFF_RV + hh * HEAD_DIM:OFF_RV + (hh + 1) * HEAD_DIM]
            scores = _mm(q, k, nt=True) * dmat_ref[hh]
            qe = q * rtab_ref[0, :, sl]
            kt = k * rtab_ref[1, :, sl]
            vb = _bf(v)
            qs = []
            for gi in range(ngroups):
                rs = slice(gi * group, (gi + 1) * group)
                s_in, s_out, _, _ = state_refs(hh, gi)
                s_old = s_in[...]
                qs.append(_mm(qe[rs], s_old))
                s_out[...] = cfg["ret_dtot"][hh] * s_old + _mm(kt[rs].T, vb[rs])
            qs = qs[0] if ngroups == 1 else jnp.concatenate(qs, axis=0)
            o_s[rows, sl] = qs + _mm(scores, vb)

            q = qg_s[rows, sl]
            k = kg_s[rows, sl]
            v = vg_s[rows, sl]
            cum_h = cumc[:, hh:hh + 1]
            beta_h = betac[:, HEADS + hh:HEADS + hh + 1]
            ecum_h = ecumc[:, hh:hh + 1]
            tail_h = tailc[:, hh:hh + 1]
            diff = cum_h - cumrow[hh:hh + 1, :]
            dmat = jnp.where(causal, jnp.exp(jnp.minimum(diff, 0.0)), 0.0)
            kb = _bf(k)
            qk_kk = _mm(jnp.concatenate([_bf(q), kb], axis=0), kb, nt=True)
            a_mat = jnp.where(strict, beta_h * qk_kk[ROWS:] * dmat, 0.0)
            t_inv, eye = _tri_inverse(a_mat, group, row, col)
            rhs = jnp.concatenate([k * (beta_h * ecum_h), v * beta_h], axis=1)
            sol = rhs + _mm(t_inv - eye, rhs)
            w, u = sol[:, :HEAD_DIM], sol[:, HEAD_DIM:]
            qe = q * ecum_h
            kt = k * tail_h
            qs, ws, olds = [], [], []
            for gi in range(ngroups):
                rs = slice(gi * group, (gi + 1) * group)
                _, _, g_in, _ = state_refs(hh, gi)
                s_old = g_in[...]
                olds.append(s_old)
                r = _mm(jnp.concatenate([qe[rs], w[rs]], axis=0), s_old)
                qs.append(r[:group])
                ws.append(r[group:])
            qs = qs[0] if ngroups == 1 else jnp.concatenate(qs, axis=0)
            ws = ws[0] if ngroups == 1 else jnp.concatenate(ws, axis=0)
            v_new = u - ws
            vnb = _bf(v_new)
            o_s[rows, GROUP_W + hh * HEAD_DIM:GROUP_W + (hh + 1) * HEAD_DIM] = (
                qs + _mm(qk_kk[:ROWS] * dmat, vnb))
            for gi in range(ngroups):
                rs = slice(gi * group, (gi + 1) * group)
                _, _, _, g_out = state_refs(hh, gi)
                dec = dtotc[gi * group:gi * group + 1, hh:hh + 1]
                g_out[...] = dec * olds[gi] + _mm(kt[rs].T, vnb[rs])
        return carry

    lax.fori_loop(0, nchunk, chunk_body, 0)

    for hh in range(2 * HEADS):
        sl = slice(hh * HEAD_DIM, (hh + 1) * HEAD_DIM)
        goff = OFF_RG + hh * HEAD_DIM if hh < HEADS else OFF_Z + (hh - HEADS) * HEAD_DIM
        y = _rms(o_s[:, sl], normw_ref[:, sl]) * _silu(proj_s[:, goff:goff + HEAD_DIM])
        mix_ref[:, sl] = y.astype(mix_ref.dtype)

    if prompt:
        @pl.when(pl.program_id(1) == pl.num_programs(1) - 1)
        def _():
            sret_out[0] = sret_s[...]
            sgdn_out[0] = sgdn_s[...]


def _post_kernel(mix_ref, x_ref, wout_ref, postw_ref, premlpw_ref, wup_ref, wdown_ref, postmlpw_ref, out_ref):
    m = _dot(mix_ref[...], wout_ref[...])
    x1 = x_ref[...] + _rms(m, postw_ref[...])
    h2 = _bf(_rms(x1, premlpw_ref[...]))
    acc = None
    for c in range(D_FF // D_MODEL):
        f = _dot(h2, wup_ref[:, c * D_MODEL:(c + 1) * D_MODEL])
        f = jnp.square(jnp.maximum(f, 0.0))
        part = _dot(_bf(f), wdown_ref[c * D_MODEL:(c + 1) * D_MODEL, :])
        acc = part if acc is None else acc + part
    out_ref[...] = x1 + _rms(acc, postmlpw_ref[...])


def _const_spec(shape):
    return pl.BlockSpec(shape, lambda *_: (0,) * len(shape))


def _retention_tables(group):
    log_gamma = np.log(1.0 - 2.0 ** (-5.0 - np.arange(HEADS, dtype=np.float64)))
    t = np.arange(ROWS) % group
    same = (np.arange(ROWS)[:, None] // group) == (np.arange(ROWS)[None, :] // group)
    causal = same & (t[:, None] >= t[None, :])
    dmat = np.where(causal[None], np.exp((t[:, None] - t[None, :])[None] * log_gamma[:, None, None]), 0.0)
    ecum = np.exp((t[:, None] + 1.0) * log_gamma[None, :])
    tail = np.exp((group - 1.0 - t[:, None]) * log_gamma[None, :])
    rtab = np.stack([np.repeat(ecum, HEAD_DIM, axis=1), np.repeat(tail, HEAD_DIM, axis=1)])
    dtot = tuple(float(np.float32(v)) for v in np.exp(group * log_gamma))
    return jnp.asarray(dmat, F32), jnp.asarray(rtab, F32), dtot


def _rope_tables(length, offset):
    inv_freq = ROPE_BASE ** (-jnp.arange(0, HEAD_DIM, 2, dtype=F32) / HEAD_DIM)
    pos = jnp.arange(length, dtype=F32) + jnp.asarray(offset, F32)
    ang = pos[:, None] * inv_freq[None, :]
    cos, sin = jnp.cos(ang), jnp.sin(ang)
    return jnp.concatenate([cos, cos], axis=-1), jnp.concatenate([-sin, sin], axis=-1)


def _mixer(x, params, *, prompt, states=None):
    b, l, _ = x.shape
    t = b * l
    x2 = x.reshape(t, D_MODEL)
    if prompt:
        group, lb = min(ROWS, l), min(256, l)
        assert l % lb == 0 and lb % ROWS == 0 and l >= 8
        cos, sin = _rope_tables(l, 0)
        grid = (b, l // lb)
        row_map = lambda i, jj: (i * (l // lb) + jj, 0)
        pos_map = lambda i, jj: (jj, 0)
        dims = ("arbitrary", "arbitrary")
    else:
        group = l
        lb = ROWS
        assert ROWS % group == 0 and group == 8 and t % lb == 0
        cos, sin = _rope_tables(l, PAST_LEN)
        cos, sin = jnp.tile(cos, (lb // l, 1)), jnp.tile(sin, (lb // l, 1))
        grid = (t // lb,)
        row_map = lambda i: (i, 0)
        pos_map = lambda i: (0, 0)
        dims = ("arbitrary",)
    nseq = lb // group
    dmat, rtab, dtot = _retention_tables(group)
    cfg = dict(lb=lb, group=group, prompt=prompt, ret_dtot=dtot)

    in_specs = [
        pl.BlockSpec((lb, D_MODEL), row_map),
        _const_spec((1, D_MODEL)),
        _const_spec((D_MODEL, MAIN_W)),
        _const_spec((D_MODEL, LANES)),
        pl.BlockSpec((lb, HEAD_DIM), pos_map),
        pl.BlockSpec((lb, HEAD_DIM), pos_map),
        _const_spec((HEADS, ROWS, ROWS)),
        _const_spec((2, ROWS, GROUP_W)),
        _const_spec((CONV_W, CONV_CH)),
        _const_spec((8, LANES)),
        _const_spec((1, 2 * GROUP_W)),
    ]
    args = [x2, params["pre_w"], params["w_main"], params["w_ab"], cos, sin, dmat, rtab,
            params["conv_w"], params["pvec"], params["normw"]]
    state_shape = (HEADS, HEAD_DIM, HEAD_DIM)
    if prompt:
        out_specs = [
            pl.BlockSpec((lb, 2 * GROUP_W), row_map),
            pl.BlockSpec((1,) + state_shape, lambda i, jj: (i, 0, 0, 0)),
            pl.BlockSpec((1,) + state_shape, lambda i, jj: (i, 0, 0, 0)),
            pl.BlockSpec((8, CONV_CH), lambda i, jj: (i, 0)),
        ]
        out_shape = [
            jax.ShapeDtypeStruct((t, 2 * GROUP_W), BF16),
            jax.ShapeDtypeStruct((b,) + state_shape, F32),
            jax.ShapeDtypeStruct((b,) + state_shape, F32),
            jax.ShapeDtypeStruct((b * 8, CONV_CH), F32),
        ]
    else:
        s_ret, s_gdn, s_conv = states
        st8 = jnp.pad(s_conv.astype(F32), ((0, 0), (8 - (CONV_W - 1), 0), (0, 0))).reshape(t, CONV_CH)
        in_specs += [
            pl.BlockSpec((nseq,) + state_shape, lambda i: (i, 0, 0, 0)),
            pl.BlockSpec((nseq,) + state_shape, lambda i: (i, 0, 0, 0)),
            pl.BlockSpec((lb, CONV_CH), row_map),
        ]
        args += [s_ret.astype(F32), s_gdn.astype(F32), st8]
        out_specs = [
            pl.BlockSpec((lb, 2 * GROUP_W), row_map),
            pl.BlockSpec((nseq,) + state_shape, lambda i: (i, 0, 0, 0)),
            pl.BlockSpec((nseq,) + state_shape, lambda i: (i, 0, 0, 0)),
            pl.BlockSpec((lb, CONV_CH), row_map),
        ]
        out_shape = [
            jax.ShapeDtypeStruct((t, 2 * GROUP_W), BF16),
            jax.ShapeDtypeStruct((b,) + state_shape, F32),
            jax.ShapeDtypeStruct((b,) + state_shape, F32),
            jax.ShapeDtypeStruct((t, CONV_CH), F32),
        ]
    scratch = [
        pltpu.VMEM((lb, MAIN_W), F32),
        pltpu.VMEM((lb, GROUP_W), F32), pltpu.VMEM((lb, GROUP_W), F32),
        pltpu.VMEM((lb, GROUP_W), F32), pltpu.VMEM((lb, GROUP_W), F32), pltpu.VMEM((lb, GROUP_W), F32),
        pltpu.VMEM((lb, LANES), F32), pltpu.VMEM((lb, LANES), F32), pltpu.VMEM((lb, LANES), F32),
        pltpu.VMEM((lb, 2 * GROUP_W), F32),
    ]
    if prompt:
        scratch += [pltpu.VMEM(state_shape, F32), pltpu.VMEM(state_shape, F32), pltpu.VMEM((8, CONV_CH), F32)]

    mix, s_ret_new, s_gdn_new, conv_rows = pl.pallas_call(
        functools.partial(_mixer_kernel, cfg),
        grid=grid,
        in_specs=in_specs,
        out_specs=out_specs,
        out_shape=out_shape,
        scratch_shapes=scratch,
        compiler_params=pltpu.CompilerParams(dimension_semantics=dims, vmem_limit_bytes=VMEM_LIMIT),
        name="mixer_prompt" if prompt else "mixer_sample",
    )(*args)
    new_conv = conv_rows.reshape(b, 8, CONV_CH)[:, 8 - (CONV_W - 1):]
    return mix, s_ret_new, s_gdn_new, new_conv


def _post(mix, x, params):
    b, l, _ = x.shape
    t = b * l
    tm = min(512, t)
    assert t % tm == 0
    rows = pl.BlockSpec((tm, D_MODEL), lambda i: (i, 0))
    y = pl.pallas_call(
        _post_kernel,
        grid=(t // tm,),
        in_specs=[rows, rows, _const_spec((2 * GROUP_W, D_MODEL)), _const_spec((1, D_MODEL)),
                  _const_spec((1, D_MODEL)), _const_spec((D_MODEL, D_FF)), _const_spec((D_FF, D_MODEL)),
                  _const_spec((1, D_MODEL))],
        out_specs=rows,
        out_shape=jax.ShapeDtypeStruct((t, D_MODEL), F32),
        compiler_params=pltpu.CompilerParams(dimension_semantics=("arbitrary",), vmem_limit_bytes=VMEM_LIMIT),
        name="post",
    )(mix, x.reshape(t, D_MODEL), params["w_out"], params["post_w"], params["pre_mlp_w"],
      params["w_up"], params["w_down"], params["post_mlp_w"])
    return y.reshape(b, l, D_MODEL)


def _layer_params(l, pre_mix_w, w_in, conv_w, A_log, dt_bias, ret_norm_w, gdn_norm_w, w_out, post_mix_w,
                  pre_mlp_w, w_up, w_down, post_mlp_w):
    w = w_in[l]
    pvec = jnp.zeros((8, LANES), F32)
    pvec = pvec.at[0, :HEADS].set(A_log[l].astype(F32)).at[1, :HEADS].set(dt_bias[l].astype(F32))
    return dict(
        pre_w=pre_mix_w[l].astype(F32)[None],
        w_main=_bf(w[:, :MAIN_W]),
        w_ab=_bf(jnp.pad(w[:, MAIN_W:], ((0, 0), (0, LANES - 2 * HEADS)))),
        conv_w=conv_w[l].astype(F32),
        pvec=pvec,
        normw=jnp.concatenate([jnp.tile(ret_norm_w[l], HEADS), jnp.tile(gdn_norm_w[l], HEADS)]).astype(F32)[None],
        w_out=_bf(w_out[l]),
        post_w=post_mix_w[l].astype(F32)[None],
        pre_mlp_w=pre_mlp_w[l].astype(F32)[None],
        w_up=_bf(w_up[l]),
        w_down=_bf(w_down[l]),
        post_mlp_w=post_mlp_w[l].astype(F32)[None],
    )


def kernel(x_prompt, x_sample, state_ret, state_gdn, state_conv, pre_mix_w, w_in, conv_w, A_log, dt_bias,
           ret_norm_w, gdn_norm_w, w_out, post_mix_w, pre_mlp_w, w_up, w_down, post_mlp_w):
    depth = w_in.shape[0]
    yp, ys = x_prompt, x_sample
    outs = [[] for _ in range(6)]
    for l in range(depth):
        params = _layer_params(l, pre_mix_w, w_in, conv_w, A_log, dt_bias, ret_norm_w, gdn_norm_w, w_out,
                               post_mix_w, pre_mlp_w, w_up, w_down, post_mlp_w)
        mix_p, rp, gp, cp = _mixer(yp, params, prompt=True)
        yp = _post(mix_p, yp, params)
        mix_s, rs, gs, cs = _mixer(ys, params, prompt=False,
                                   states=(state_ret[l], state_gdn[l], state_conv[l]))
        ys = _post(mix_s, ys, params)
        for dst, val, like in zip(outs, (rp, gp, cp, rs, gs, cs),
                                  (state_ret, state_gdn, state_conv) * 2):
            dst.append(val.astype(like.dtype))
    return (yp, ys) + tuple(jnp.stack(o) for o in outs)
```

```python
import functools

import numpy as np
import jax
import jax.numpy as jnp
from jax import lax
from jax.experimental import pallas as pl
from jax.experimental.pallas import tpu as pltpu

D_MODEL = 1024
HEADS = 4
HEAD_DIM = 128
GROUP_W = HEADS * HEAD_DIM
CONV_CH = 3 * GROUP_W
CONV_W = 4
D_FF = 4 * D_MODEL
MAIN_W = 4 * GROUP_W + CONV_CH + GROUP_W
ROPE_BASE = 10000.0
EPS = 1e-6
PAST_LEN = 16384
REF_CHUNK = 64

ROWS = 64
BASE = 8
LANES = 128
VMEM_LIMIT = 52 * 1024 * 1024

OFF_RQ, OFF_RK, OFF_RV, OFF_RG = 0, GROUP_W, 2 * GROUP_W, 3 * GROUP_W
OFF_CONV = 4 * GROUP_W
OFF_Z = OFF_CONV + CONV_CH

F32 = jnp.float32
BF16 = jnp.bfloat16


def _bf(x):
    return x.astype(BF16)


def _dot(a, b):
    return jnp.dot(a, b, preferred_element_type=F32)


def _dot_nt(a, b):
    return lax.dot_general(a, b, (((1,), (1,)), ((), ())), preferred_element_type=F32)


def _split(x, n):
    if x.dtype == BF16:
        return [x]
    parts, r = [], x
    for i in range(n):
        p = r.astype(BF16)
        parts.append(p)
        if i + 1 < n:
            r = r - p.astype(F32)
    return parts


def _mm(a, b, pa=1, pb=1, nt=False):
    ap, bp = _split(a, pa), _split(b, pb)
    n = max(len(ap), len(bp))
    acc = None
    for i, x in enumerate(ap):
        for j, y in enumerate(bp):
            if i + j < n:
                t = _dot_nt(x, y) if nt else _dot(x, y)
                acc = t if acc is None else acc + t
    return acc


def _rms(x, w):
    return x * lax.rsqrt(jnp.mean(x * x, axis=-1, keepdims=True) + EPS) * w


def _silu(x):
    return x / (1.0 + jnp.exp(-x))


def _tri_inverse_many(mats, group, row, col):
    eye = (row == col).astype(F32)
    mm = functools.partial(_mm, pa=2, pb=2)
    blk = (row // BASE) == (col // BASE)
    n8 = [jnp.where(blk, -a, 0.0) for a in mats]
    p2 = [mm(n, n) for n in n8]
    np2 = [mm(n, p) for n, p in zip(n8, p2)]
    p4 = [mm(p, p) for p in p2]
    e = [n + p + x for n, p, x in zip(n8, p2, np2)]
    ep4 = [mm(x, p) for x, p in zip(e, p4)]
    t = [x + p + y + eye for x, p, y in zip(e, p4, ep4)]
    s = BASE
    while s < group:
        off = ((row // (2 * s)) == (col // (2 * s))) & ((row // s) != (col // s))
        a_off = [jnp.where(off, a, 0.0) for a in mats]
        ta = [mm(x, y) for x, y in zip(t, a_off)]
        tat = [mm(x, y) for x, y in zip(ta, t)]
        t = [x - y for x, y in zip(t, tat)]
        s *= 2
    return t, eye


def _mixer_kernel(cfg, *refs):
    lb, group, prompt = cfg["lb"], cfg["group"], cfg["prompt"]
    ngroups = ROWS // group
    nchunk = lb // ROWS
    it = iter(refs)
    x_ref, prew_ref, wmain_ref, wab_ref, cos_ref, sin_ref = (next(it) for _ in range(6))
    dmat_ref, rtab_ref, convw_ref, pvec_ref, normw_ref = (next(it) for _ in range(5))
    if not prompt:
        sret_in, sgdn_in, st8_ref = (next(it) for _ in range(3))
    mix_ref, sret_out, sgdn_out, conv_out = (next(it) for _ in range(4))
    proj_s, qr_s, kr_s, qg_s, kg_s, vg_s, cum_s, tot_s, beta_s, o_s = (next(it) for _ in range(10))
    w_s, u_s, pr_s, pg_s = (next(it) for _ in range(4))
    if prompt:
        sret_s, sgdn_s, prev8_s = (next(it) for _ in range(3))
        j = pl.program_id(1)

        @pl.when(j == 0)
        def _():
            sret_s[...] = jnp.zeros_like(sret_s)
            sgdn_s[...] = jnp.zeros_like(sgdn_s)
            prev8_s[...] = jnp.zeros_like(prev8_s)

    h = _bf(_rms(x_ref[...], prew_ref[...]))
    proj_s[...] = _dot(h, wmain_ref[...])
    ab = _dot(h, wab_ref[...])

    cos, sin = cos_ref[...], sin_ref[...]
    for hh in range(HEADS):
        sl = slice(hh * HEAD_DIM, (hh + 1) * HEAD_DIM)
        q = proj_s[:, OFF_RQ + hh * HEAD_DIM:OFF_RQ + (hh + 1) * HEAD_DIM]
        qr_s[:, sl] = q * cos + pltpu.roll(q, HEAD_DIM // 2, 1) * sin
        k = proj_s[:, OFF_RK + hh * HEAD_DIM:OFF_RK + (hh + 1) * HEAD_DIM]
        kr_s[:, sl] = (k * cos + pltpu.roll(k, HEAD_DIM // 2, 1) * sin) * (HEAD_DIM ** -0.5)

    t_in_group = lax.broadcasted_iota(jnp.int32, (lb, 1), 0) % group
    for part, dst in enumerate((qg_s, kg_s, vg_s)):
        c0 = OFF_CONV + part * GROUP_W
        xq = proj_s[:, c0:c0 + GROUP_W]
        cw = convw_ref[:, part * GROUP_W:(part + 1) * GROUP_W]
        acc = xq * cw[CONV_W - 1:CONV_W]
        if prompt:
            ext = jnp.concatenate([prev8_s[:, part * GROUP_W:(part + 1) * GROUP_W], xq], axis=0)
            for s in range(1, CONV_W):
                sh = pltpu.roll(ext, s, 0)[8:]
                acc = acc + sh * cw[CONV_W - 1 - s:CONV_W - s]
            prev8_s[:, part * GROUP_W:(part + 1) * GROUP_W] = xq[lb - 8:]
        else:
            st8 = st8_ref[:, part * GROUP_W:(part + 1) * GROUP_W]
            for s in range(1, CONV_W):
                sh = jnp.where(t_in_group < s, pltpu.roll(st8, lb - 8 + s, 0), pltpu.roll(xq, s, 0))
                acc = acc + sh * cw[CONV_W - 1 - s:CONV_W - s]
        conv = _silu(acc)
        for hh in range(HEADS):
            sl = slice(hh * HEAD_DIM, (hh + 1) * HEAD_DIM)
            c = conv[:, sl]
            if part < 2:
                c = c * lax.rsqrt(jnp.sum(c * c, axis=-1, keepdims=True) + EPS)
            if part == 0:
                c = c * (HEAD_DIM ** -0.5)
            dst[:, sl] = c
    if prompt:
        conv_out[...] = prev8_s[...]
    else:
        conv_out[...] = proj_s[:, OFF_CONV:OFF_CONV + CONV_CH]

    pv = pvec_ref[...]
    a_plus = ab + pv[1:2]
    softplus = jnp.maximum(a_plus, 0.0) + jnp.log1p(jnp.exp(-jnp.abs(a_plus)))
    g = -jnp.exp(pv[0:1]) * softplus
    beta_s[...] = 1.0 / (1.0 + jnp.exp(-ab))

    row = lax.broadcasted_iota(jnp.int32, (ROWS, ROWS), 0)
    col = lax.broadcasted_iota(jnp.int32, (ROWS, ROWS), 1)
    same = (row // group) == (col // group)
    causal = same & (row >= col)
    strict = same & (row > col)
    sum_mat = _bf(jnp.concatenate([causal.astype(F32), same.astype(F32)], axis=0))
    for c in range(nchunk):
        r = _mm(sum_mat, g[c * ROWS:(c + 1) * ROWS], 1, 3)
        cum_s[c * ROWS:(c + 1) * ROWS, :] = r[:ROWS]
        tot_s[c * ROWS:(c + 1) * ROWS, :] = r[ROWS:]
    sel = _bf((lax.broadcasted_iota(jnp.int32, (8, LANES), 0)
               == lax.broadcasted_iota(jnp.int32, (8, LANES), 1)).astype(F32))

    chunks = range(nchunk)
    items = [(c, hh) for c in chunks for hh in range(HEADS)]
    rows_of = lambda c: slice(c * ROWS, (c + 1) * ROWS)
    head_sl = lambda hh: slice(hh * HEAD_DIM, (hh + 1) * HEAD_DIM)
    lane = lambda x, n: x[:, n:n + 1]

    cumc = [cum_s[rows_of(c), :] for c in chunks]
    totc = [tot_s[rows_of(c), :] for c in chunks]
    betac = [beta_s[rows_of(c), :] for c in chunks]
    ecumc = [jnp.exp(x) for x in cumc]
    tailc = [jnp.exp(t - x) for t, x in zip(totc, cumc)]
    cumrow = [_mm(sel, x, 1, 3, nt=True) for x in cumc]

    for i, (c, hh) in enumerate(items):
        rows, sl = rows_of(c), head_sl(hh)
        q, k = qr_s[rows, sl], kr_s[rows, sl]
        pr_s[i] = _mm(q, k, nt=True) * dmat_ref[hh]
        qr_s[rows, sl] = q * rtab_ref[0, :, sl]
        kr_s[rows, sl] = k * rtab_ref[1, :, sl]

    qk_kk, dmats = [], []
    for c, hh in items:
        rows, sl = rows_of(c), head_sl(hh)
        kb = _bf(kg_s[rows, sl])
        qk_kk.append(_mm(jnp.concatenate([_bf(qg_s[rows, sl]), kb], axis=0), kb, nt=True))
        diff = lane(cumc[c], hh) - cumrow[c][hh:hh + 1, :]
        dmats.append(jnp.where(causal, jnp.exp(jnp.minimum(diff, 0.0)), 0.0))
    a_mats = [jnp.where(strict, lane(betac[c], HEADS + hh) * x[ROWS:] * d, 0.0)
              for (c, hh), x, d in zip(items, qk_kk, dmats)]
    t_inv, eye = _tri_inverse_many(a_mats, group, row, col)
    for i, (c, hh) in enumerate(items):
        rows, sl = rows_of(c), head_sl(hh)
        q, k, v = qg_s[rows, sl], kg_s[rows, sl], vg_s[rows, sl]
        beta_h = lane(betac[c], HEADS + hh)
        rhs = jnp.concatenate([k * (beta_h * lane(ecumc[c], hh)), v * beta_h], axis=1)
        sol = rhs + _mm(t_inv[i] - eye, rhs)
        w_s[rows, sl] = sol[:, :HEAD_DIM]
        u_s[rows, sl] = sol[:, HEAD_DIM:]
        pg_s[i] = qk_kk[i][:ROWS] * dmats[i]
        qg_s[rows, sl] = q * lane(ecumc[c], hh)
        kg_s[rows, sl] = k * lane(tailc[c], hh)

    def state_refs(c, hh, gi):
        if prompt:
            return (sret_s.at[hh], sret_s.at[hh], sgdn_s.at[hh], sgdn_s.at[hh])
        n = c * ngroups + gi
        return (sret_in.at[n, hh], sret_out.at[n, hh], sgdn_in.at[n, hh], sgdn_out.at[n, hh])

    cat = lambda xs: xs[0] if len(xs) == 1 else jnp.concatenate(xs, axis=0)
    for c in chunks:
        dtotc = jnp.exp(totc[c])
        qs_r, qs_g, ws_g = {}, {}, {}
        for hh in range(HEADS):
            sl = head_sl(hh)
            for gi in range(ngroups):
                rs = slice(c * ROWS + gi * group, c * ROWS + (gi + 1) * group)
                r_in, _, g_in, _ = state_refs(c, hh, gi)
                qs_r[hh, gi] = _mm(qr_s[rs, sl], r_in[...])
                r = _mm(jnp.concatenate([qg_s[rs, sl], w_s[rs, sl]], axis=0), g_in[...])
                qs_g[hh, gi], ws_g[hh, gi] = r[:group], r[group:]
        for hh in range(HEADS):
            rows, sl = rows_of(c), head_sl(hh)
            i = c * HEADS + hh
            vb = _bf(proj_s[rows, OFF_RV + hh * HEAD_DIM:OFF_RV + (hh + 1) * HEAD_DIM])
            v_new = u_s[rows, sl] - cat([ws_g[hh, gi] for gi in range(ngroups)])
            vnb = _bf(v_new)
            o_s[rows, sl] = cat([qs_r[hh, gi] for gi in range(ngroups)]) + _mm(pr_s[i], vb)
            o_s[rows, GROUP_W + hh * HEAD_DIM:GROUP_W + (hh + 1) * HEAD_DIM] = (
                cat([qs_g[hh, gi] for gi in range(ngroups)]) + _mm(pg_s[i], vnb))
            for gi in range(ngroups):
                rs = slice(c * ROWS + gi * group, c * ROWS + (gi + 1) * group)
                ls = slice(gi * group, (gi + 1) * group)
                r_in, r_out, g_in, g_out = state_refs(c, hh, gi)
                r_out[...] = cfg["ret_dtot"][hh] * r_in[...] + _mm(kr_s[rs, sl].T, vb[ls])
                dec = dtotc[gi * group:gi * group + 1, hh:hh + 1]
                g_out[...] = dec * g_in[...] + _mm(kg_s[rs, sl].T, vnb[ls])

    for hh in range(2 * HEADS):
        sl = slice(hh * HEAD_DIM, (hh + 1) * HEAD_DIM)
        goff = OFF_RG + hh * HEAD_DIM if hh < HEADS else OFF_Z + (hh - HEADS) * HEAD_DIM
        y = _rms(o_s[:, sl], normw_ref[:, sl]) * _silu(proj_s[:, goff:goff + HEAD_DIM])
        mix_ref[:, sl] = y.astype(mix_ref.dtype)

    if prompt:
        @pl.when(pl.program_id(1) == pl.num_programs(1) - 1)
        def _():
            sret_out[0] = sret_s[...]
            sgdn_out[0] = sgdn_s[...]


def _post_kernel(mix_ref, x_ref, wout_ref, postw_ref, premlpw_ref, wup_ref, wdown_ref, postmlpw_ref, out_ref):
    m = _dot(mix_ref[...], wout_ref[...])
    x1 = x_ref[...] + _rms(m, postw_ref[...])
    h2 = _bf(_rms(x1, premlpw_ref[...]))
    acc = None
    for c in range(D_FF // D_MODEL):
        f = _dot(h2, wup_ref[:, c * D_MODEL:(c + 1) * D_MODEL])
        f = jnp.square(jnp.maximum(f, 0.0))
        part = _dot(_bf(f), wdown_ref[c * D_MODEL:(c + 1) * D_MODEL, :])
        acc = part if acc is None else acc + part
    out_ref[...] = x1 + _rms(acc, postmlpw_ref[...])


def _const_spec(shape):
    return pl.BlockSpec(shape, lambda *_: (0,) * len(shape))


def _retention_tables(group):
    log_gamma = np.log(1.0 - 2.0 ** (-5.0 - np.arange(HEADS, dtype=np.float64)))
    t = np.arange(ROWS) % group
    same = (np.arange(ROWS)[:, None] // group) == (np.arange(ROWS)[None, :] // group)
    causal = same & (t[:, None] >= t[None, :])
    dmat = np.where(causal[None], np.exp((t[:, None] - t[None, :])[None] * log_gamma[:, None, None]), 0.0)
    ecum = np.exp((t[:, None] + 1.0) * log_gamma[None, :])
    tail = np.exp((group - 1.0 - t[:, None]) * log_gamma[None, :])
    rtab = np.stack([np.repeat(ecum, HEAD_DIM, axis=1), np.repeat(tail, HEAD_DIM, axis=1)])
    dtot = tuple(float(np.float32(v)) for v in np.exp(group * log_gamma))
    return jnp.asarray(dmat, F32), jnp.asarray(rtab, F32), dtot


def _rope_tables(length, offset):
    inv_freq = ROPE_BASE ** (-jnp.arange(0, HEAD_DIM, 2, dtype=F32) / HEAD_DIM)
    pos = jnp.arange(length, dtype=F32) + jnp.asarray(offset, F32)
    ang = pos[:, None] * inv_freq[None, :]
    cos, sin = jnp.cos(ang), jnp.sin(ang)
    return jnp.concatenate([cos, cos], axis=-1), jnp.concatenate([-sin, sin], axis=-1)


def _mixer(x, params, *, prompt, states=None):
    b, l, _ = x.shape
    t = b * l
    x2 = x.reshape(t, D_MODEL)
    if prompt:
        group, lb = min(ROWS, l), min(256, l)
        assert l % lb == 0 and lb % ROWS == 0 and l >= 8
        cos, sin = _rope_tables(l, 0)
        grid = (b, l // lb)
        row_map = lambda i, jj: (i * (l // lb) + jj, 0)
        pos_map = lambda i, jj: (jj, 0)
        dims = ("arbitrary", "arbitrary")
    else:
        group = l
        lb = ROWS
        assert ROWS % group == 0 and group == 8 and t % lb == 0
        cos, sin = _rope_tables(l, PAST_LEN)
        cos, sin = jnp.tile(cos, (lb // l, 1)), jnp.tile(sin, (lb // l, 1))
        grid = (t // lb,)
        row_map = lambda i: (i, 0)
        pos_map = lambda i: (0, 0)
        dims = ("arbitrary",)
    nseq = lb // group
    dmat, rtab, dtot = _retention_tables(group)
    cfg = dict(lb=lb, group=group, prompt=prompt, ret_dtot=dtot)

    in_specs = [
        pl.BlockSpec((lb, D_MODEL), row_map),
        _const_spec((1, D_MODEL)),
        _const_spec((D_MODEL, MAIN_W)),
        _const_spec((D_MODEL, LANES)),
        pl.BlockSpec((lb, HEAD_DIM), pos_map),
        pl.BlockSpec((lb, HEAD_DIM), pos_map),
        _const_spec((HEADS, ROWS, ROWS)),
        _const_spec((2, ROWS, GROUP_W)),
        _const_spec((CONV_W, CONV_CH)),
        _const_spec((8, LANES)),
        _const_spec((1, 2 * GROUP_W)),
    ]
    args = [x2, params["pre_w"], params["w_main"], params["w_ab"], cos, sin, dmat, rtab,
            params["conv_w"], params["pvec"], params["normw"]]
    state_shape = (HEADS, HEAD_DIM, HEAD_DIM)
    if prompt:
        out_specs = [
            pl.BlockSpec((lb, 2 * GROUP_W), row_map),
            pl.BlockSpec((1,) + state_shape, lambda i, jj: (i, 0, 0, 0)),
            pl.BlockSpec((1,) + state_shape, lambda i, jj: (i, 0, 0, 0)),
            pl.BlockSpec((8, CONV_CH), lambda i, jj: (i, 0)),
        ]
        out_shape = [
            jax.ShapeDtypeStruct((t, 2 * GROUP_W), BF16),
            jax.ShapeDtypeStruct((b,) + state_shape, F32),
            jax.ShapeDtypeStruct((b,) + state_shape, F32),
            jax.ShapeDtypeStruct((b * 8, CONV_CH), F32),
        ]
    else:
        s_ret, s_gdn, s_conv = states
        st8 = jnp.pad(s_conv.astype(F32), ((0, 0), (8 - (CONV_W - 1), 0), (0, 0))).reshape(t, CONV_CH)
        in_specs += [
            pl.BlockSpec((nseq,) + state_shape, lambda i: (i, 0, 0, 0)),
            pl.BlockSpec((nseq,) + state_shape, lambda i: (i, 0, 0, 0)),
            pl.BlockSpec((lb, CONV_CH), row_map),
        ]
        args += [s_ret.astype(F32), s_gdn.astype(F32), st8]
        out_specs = [
            pl.BlockSpec((lb, 2 * GROUP_W), row_map),
            pl.BlockSpec((nseq,) + state_shape, lambda i: (i, 0, 0, 0)),
            pl.BlockSpec((nseq,) + state_shape, lambda i: (i, 0, 0, 0)),
            pl.BlockSpec((lb, CONV_CH), row_map),
        ]
        out_shape = [
            jax.ShapeDtypeStruct((t, 2 * GROUP_W), BF16),
            jax.ShapeDtypeStruct((b,) + state_shape, F32),
            jax.ShapeDtypeStruct((b,) + state_shape, F32),
            jax.ShapeDtypeStruct((t, CONV_CH), F32),
        ]
    scratch = [
        pltpu.VMEM((lb, MAIN_W), F32),
        pltpu.VMEM((lb, GROUP_W), F32), pltpu.VMEM((lb, GROUP_W), F32),
        pltpu.VMEM((lb, GROUP_W), F32), pltpu.VMEM((lb, GROUP_W), F32), pltpu.VMEM((lb, GROUP_W), F32),
        pltpu.VMEM((lb, LANES), F32), pltpu.VMEM((lb, LANES), F32), pltpu.VMEM((lb, LANES), F32),
        pltpu.VMEM((lb, 2 * GROUP_W), F32),
        pltpu.VMEM((lb, GROUP_W), F32), pltpu.VMEM((lb, GROUP_W), F32),
        pltpu.VMEM((lb // ROWS * HEADS, ROWS, ROWS), F32), pltpu.VMEM((lb // ROWS * HEADS, ROWS, ROWS), F32),
    ]
    if prompt:
        scratch += [pltpu.VMEM(state_shape, F32), pltpu.VMEM(state_shape, F32), pltpu.VMEM((8, CONV_CH), F32)]

    mix, s_ret_new, s_gdn_new, conv_rows = pl.pallas_call(
        functools.partial(_mixer_kernel, cfg),
        grid=grid,
        in_specs=in_specs,
        out_specs=out_specs,
        out_shape=out_shape,
        scratch_shapes=scratch,
        compiler_params=pltpu.CompilerParams(dimension_semantics=dims, vmem_limit_bytes=VMEM_LIMIT),
        name="mixer_prompt" if prompt else "mixer_sample",
    )(*args)
    new_conv = conv_rows.reshape(b, 8, CONV_CH)[:, 8 - (CONV_W - 1):]
    return mix, s_ret_new, s_gdn_new, new_conv


def _post(mix, x, params):
    b, l, _ = x.shape
    t = b * l
    tm = min(512, t)
    assert t % tm == 0
    rows = pl.BlockSpec((tm, D_MODEL), lambda i: (i, 0))
    y = pl.pallas_call(
        _post_kernel,
        grid=(t // tm,),
        in_specs=[rows, rows, _const_spec((2 * GROUP_W, D_MODEL)), _const_spec((1, D_MODEL)),
                  _const_spec((1, D_MODEL)), _const_spec((D_MODEL, D_FF)), _const_spec((D_FF, D_MODEL)),
                  _const_spec((1, D_MODEL))],
        out_specs=rows,
        out_shape=jax.ShapeDtypeStruct((t, D_MODEL), F32),
        compiler_params=pltpu.CompilerParams(dimension_semantics=("arbitrary",), vmem_limit_bytes=VMEM_LIMIT),
        name="post",
    )(mix, x.reshape(t, D_MODEL), params["w_out"], params["post_w"], params["pre_mlp_w"],
      params["w_up"], params["w_down"], params["post_mlp_w"])
    return y.reshape(b, l, D_MODEL)


def _layer_params(l, pre_mix_w, w_in, conv_w, A_log, dt_bias, ret_norm_w, gdn_norm_w, w_out, post_mix_w,
                  pre_mlp_w, w_up, w_down, post_mlp_w):
    w = w_in[l]
    pvec = jnp.zeros((8, LANES), F32)
    pvec = pvec.at[0, :HEADS].set(A_log[l].astype(F32)).at[1, :HEADS].set(dt_bias[l].astype(F32))
    return dict(
        pre_w=pre_mix_w[l].astype(F32)[None],
        w_main=_bf(w[:, :MAIN_W]),
        w_ab=_bf(jnp.pad(w[:, MAIN_W:], ((0, 0), (0, LANES - 2 * HEADS)))),
        conv_w=conv_w[l].astype(F32),
        pvec=pvec,
        normw=jnp.concatenate([jnp.tile(ret_norm_w[l], HEADS), jnp.tile(gdn_norm_w[l], HEADS)]).astype(F32)[None],
        w_out=_bf(w_out[l]),
        post_w=post_mix_w[l].astype(F32)[None],
        pre_mlp_w=pre_mlp_w[l].astype(F32)[None],
        w_up=_bf(w_up[l]),
        w_down=_bf(w_down[l]),
        post_mlp_w=post_mlp_w[l].astype(F32)[None],
    )


def kernel(x_prompt, x_sample, state_ret, state_gdn, state_conv, pre_mix_w, w_in, conv_w, A_log, dt_bias,
           ret_norm_w, gdn_norm_w, w_out, post_mix_w, pre_mlp_w, w_up, w_down, post_mlp_w):
    depth = w_in.shape[0]
    yp, ys = x_prompt, x_sample
    outs = [[] for _ in range(6)]
    for l in range(depth):
        params = _layer_params(l, pre_mix_w, w_in, conv_w, A_log, dt_bias, ret_norm_w, gdn_norm_w, w_out,
                               post_mix_w, pre_mlp_w, w_up, w_down, post_mlp_w)
        mix_p, rp, gp, cp = _mixer(yp, params, prompt=True)
        yp = _post(mix_p, yp, params)
        mix_s, rs, gs, cs = _mixer(ys, params, prompt=False,
                                   states=(state_ret[l], state_gdn[l], state_conv[l]))
        ys = _post(mix_s, ys, params)
        for dst, val, like in zip(outs, (rp, gp, cp, rs, gs, cs),
                                  (state_ret, state_gdn, state_conv) * 2):
            dst.append(val.astype(like.dtype))
    return (yp, ys) + tuple(jnp.stack(o) for o in outs)
```

```python
import functools

import numpy as np
import jax
import jax.numpy as jnp
from jax import lax
from jax.experimental import pallas as pl
from jax.experimental.pallas import tpu as pltpu

D_MODEL = 1024
HEADS = 4
HEAD_DIM = 128
GROUP_W = HEADS * HEAD_DIM
CONV_CH = 3 * GROUP_W
CONV_W = 4
D_FF = 4 * D_MODEL
MAIN_W = 4 * GROUP_W + CONV_CH + GROUP_W
ROPE_BASE = 10000.0
EPS = 1e-6
PAST_LEN = 16384

ROWS = 64
BASE = 8
LANES = 128
VMEM_LIMIT = 52 * 1024 * 1024

OFF_RQ, OFF_RK, OFF_RV, OFF_RG = 0, GROUP_W, 2 * GROUP_W, 3 * GROUP_W
OFF_CONV = 4 * GROUP_W
OFF_Z = OFF_CONV + CONV_CH

F32 = jnp.float32
BF16 = jnp.bfloat16


def _bf(x):
    return x.astype(BF16)


def _dot(a, b):
    return jnp.dot(a, b, preferred_element_type=F32)


def _dot_nt(a, b):
    return lax.dot_general(a, b, (((1,), (1,)), ((), ())), preferred_element_type=F32)


def _split(x, n):
    if x.dtype == BF16:
        return [x]
    parts, r = [], x
    for i in range(n):
        p = r.astype(BF16)
        parts.append(p)
        if i + 1 < n:
            r = r - p.astype(F32)
    return parts


def _mm(a, b, pa=1, pb=1, nt=False):
    ap, bp = _split(a, pa), _split(b, pb)
    n = max(len(ap), len(bp))
    acc = None
    for i, x in enumerate(ap):
        for j, y in enumerate(bp):
            if i + j < n:
                t = _dot_nt(x, y) if nt else _dot(x, y)
                acc = t if acc is None else acc + t
    return acc


def _rms(x, w):
    return x * lax.rsqrt(jnp.mean(x * x, axis=-1, keepdims=True) + EPS) * w


def _silu(x):
    return x / (1.0 + jnp.exp(-x))


def _run(*gens):
    live = list(gens)
    while live:
        for g in list(live):
            try:
                next(g)
            except StopIteration:
                live.remove(g)


def _tri_inverse_many(mats, group, row, col):
    eye = (row == col).astype(F32)
    mm = functools.partial(_mm, pa=2, pb=2)
    blk = (row // BASE) == (col // BASE)
    n8 = [jnp.where(blk, -a, 0.0) for a in mats]
    p2 = [mm(n, n) for n in n8]
    yield
    np2 = [mm(n, p) for n, p in zip(n8, p2)]
    p4 = [mm(p, p) for p in p2]
    yield
    e = [n + p + x for n, p, x in zip(n8, p2, np2)]
    ep4 = [mm(x, p) for x, p in zip(e, p4)]
    yield
    t = [x + p + y + eye for x, p, y in zip(e, p4, ep4)]
    s = BASE
    while s < group:
        off = ((row // (2 * s)) == (col // (2 * s))) & ((row // s) != (col // s))
        a_off = [jnp.where(off, a, 0.0) for a in mats]
        ta = [mm(x, y) for x, y in zip(t, a_off)]
        yield
        tat = [mm(x, y) for x, y in zip(ta, t)]
        yield
        t = [x - y for x, y in zip(t, tat)]
        s *= 2
    return t, eye


def _mixer_kernel(cfg, *refs):
    lb, group, prompt, nblk, bps = cfg["lb"], cfg["group"], cfg["prompt"], cfg["nblk"], cfg["blocks_per_seq"]
    ngroups = ROWS // group
    chunks = list(range(lb // ROWS))
    it = iter(refs)
    x_ref, prew_ref, wmain_ref, wab_ref, cos_ref, sin_ref = (next(it) for _ in range(6))
    dmat_ref, rtab_ref, convw_ref, pvec_ref, normw_ref = (next(it) for _ in range(5))
    if not prompt:
        sret_in, sgdn_in, st8_ref = (next(it) for _ in range(3))
    mix_ref, sret_out, sgdn_out, conv_out = (next(it) for _ in range(4))
    staged = [next(it) for _ in range(9)]
    o_s, w_s, u_s, pr_s, pg_s = (next(it) for _ in range(5))
    if prompt:
        sret_s, sgdn_s, prev8_s = (next(it) for _ in range(3))

    step = pl.program_id(0)
    rows_of = lambda c: slice(c * ROWS, (c + 1) * ROWS)
    head_sl = lambda hh: slice(hh * HEAD_DIM, (hh + 1) * HEAD_DIM)
    lane = lambda x, n: x[:, n:n + 1]
    cat = lambda xs: xs[0] if len(xs) == 1 else jnp.concatenate(xs, axis=0)

    if prompt:
        first_blk = jnp.minimum(step, nblk - 1) % bps == 0
        rec_first = (step >= 1) & ((step - 1) % bps == 0)
        rec_last = (step >= 1) & ((step - 1) % bps == bps - 1)

        @pl.when(first_blk)
        def _():
            prev8_s[...] = jnp.zeros_like(prev8_s)

        @pl.when(rec_first)
        def _():
            sret_s[...] = jnp.zeros_like(sret_s)
            sgdn_s[...] = jnp.zeros_like(sgdn_s)

    def front(slot):
        proj_s, qr_s, kr_s, qg_s, kg_s, vg_s, cum_s, tot_s, beta_s = (r.at[slot] for r in staged)
        h = _bf(_rms(x_ref[...], prew_ref[...]))
        ab = _dot(h, wab_ref[...])
        yield
        for c0 in range(0, MAIN_W, GROUP_W):
            proj_s[:, c0:c0 + GROUP_W] = _dot(h, wmain_ref[:, c0:c0 + GROUP_W])
            yield
        cos, sin = cos_ref[...], sin_ref[...]
        for hh in range(HEADS):
            sl = head_sl(hh)
            q = proj_s[:, OFF_RQ + hh * HEAD_DIM:OFF_RQ + (hh + 1) * HEAD_DIM]
            qr_s[:, sl] = q * cos + pltpu.roll(q, HEAD_DIM // 2, 1) * sin
            k = proj_s[:, OFF_RK + hh * HEAD_DIM:OFF_RK + (hh + 1) * HEAD_DIM]
            kr_s[:, sl] = (k * cos + pltpu.roll(k, HEAD_DIM // 2, 1) * sin) * (HEAD_DIM ** -0.5)
        yield
        t_in_group = lax.broadcasted_iota(jnp.int32, (lb, 1), 0) % group
        for part, dst in enumerate((qg_s, kg_s, vg_s)):
            cols = slice(OFF_CONV + part * GROUP_W, OFF_CONV + (part + 1) * GROUP_W)
            xq = proj_s[:, cols]
            cw = convw_ref[:, part * GROUP_W:(part + 1) * GROUP_W]
            acc = xq * cw[CONV_W - 1:CONV_W]
            if prompt:
                ext = jnp.concatenate([prev8_s[:, part * GROUP_W:(part + 1) * GROUP_W], xq], axis=0)
                for s in range(1, CONV_W):
                    sh = pltpu.roll(ext, s, 0)[8:]
                    acc = acc + sh * cw[CONV_W - 1 - s:CONV_W - s]
            else:
                st8 = st8_ref[:, part * GROUP_W:(part + 1) * GROUP_W]
                for s in range(1, CONV_W):
                    sh = jnp.where(t_in_group < s, pltpu.roll(st8, lb - 8 + s, 0), pltpu.roll(xq, s, 0))
                    acc = acc + sh * cw[CONV_W - 1 - s:CONV_W - s]
            conv = _silu(acc)
            for hh in range(HEADS):
                sl = head_sl(hh)
                c = conv[:, sl]
                if part < 2:
                    c = c * lax.rsqrt(jnp.sum(c * c, axis=-1, keepdims=True) + EPS)
                if part == 0:
                    c = c * (HEAD_DIM ** -0.5)
                dst[:, sl] = c
            yield
        if prompt:
            prev8_s[...] = proj_s[lb - 8:lb, OFF_CONV:OFF_CONV + CONV_CH]
            conv_out[...] = proj_s[lb - 8:lb, OFF_CONV:OFF_CONV + CONV_CH]
        else:
            conv_out[...] = proj_s[:, OFF_CONV:OFF_CONV + CONV_CH]
        pv = pvec_ref[...]
        a_plus = ab + pv[1:2]
        softplus = jnp.maximum(a_plus, 0.0) + jnp.log1p(jnp.exp(-jnp.abs(a_plus)))
        g = -jnp.exp(pv[0:1]) * softplus
        beta_s[...] = 1.0 / (1.0 + jnp.exp(-ab))
        row = lax.broadcasted_iota(jnp.int32, (ROWS, ROWS), 0)
        col = lax.broadcasted_iota(jnp.int32, (ROWS, ROWS), 1)
        same = (row // group) == (col // group)
        sum_mat = _bf(jnp.concatenate([(same & (row >= col)).astype(F32), same.astype(F32)], axis=0))
        for c in chunks:
            r = _mm(sum_mat, g[rows_of(c)], 1, 3)
            cum_s[rows_of(c), :] = r[:ROWS]
            tot_s[rows_of(c), :] = r[ROWS:]
        yield

    def state_refs(c, hh, gi):
        if prompt:
            return (sret_s.at[hh], sret_s.at[hh], sgdn_s.at[hh], sgdn_s.at[hh])
        n = c * ngroups + gi
        return (sret_in.at[n, hh], sret_out.at[n, hh], sgdn_in.at[n, hh], sgdn_out.at[n, hh])

    def back_parallel(slot):
        proj_s, qr_s, kr_s, qg_s, kg_s, vg_s, cum_s, tot_s, beta_s = (r.at[slot] for r in staged)
        row = lax.broadcasted_iota(jnp.int32, (ROWS, ROWS), 0)
        col = lax.broadcasted_iota(jnp.int32, (ROWS, ROWS), 1)
        same = (row // group) == (col // group)
        causal = same & (row >= col)
        strict = same & (row > col)
        sel = _bf((lax.broadcasted_iota(jnp.int32, (8, LANES), 0)
                   == lax.broadcasted_iota(jnp.int32, (8, LANES), 1)).astype(F32))

        items = [(c, hh) for c in chunks for hh in range(HEADS)]
        cumc = {c: cum_s[rows_of(c), :] for c in chunks}
        totc = {c: tot_s[rows_of(c), :] for c in chunks}
        betac = {c: beta_s[rows_of(c), :] for c in chunks}
        ecumc = {c: jnp.exp(cumc[c]) for c in chunks}
        tailc = {c: jnp.exp(totc[c] - cumc[c]) for c in chunks}
        cumrow = {c: _mm(sel, cumc[c], 1, 3, nt=True) for c in chunks}
        for c, hh in items:
            rows, sl = rows_of(c), head_sl(hh)
            q, k = qr_s[rows, sl], kr_s[rows, sl]
            pr_s[c * HEADS + hh] = _mm(q, k, nt=True) * dmat_ref[hh]
            qr_s[rows, sl] = q * rtab_ref[0, :, sl]
            kr_s[rows, sl] = k * rtab_ref[1, :, sl]
        yield
        qk_kk, dmats = [], []
        for c, hh in items:
            rows, sl = rows_of(c), head_sl(hh)
            kb = _bf(kg_s[rows, sl])
            qk_kk.append(_mm(jnp.concatenate([_bf(qg_s[rows, sl]), kb], axis=0), kb, nt=True))
            diff = lane(cumc[c], hh) - cumrow[c][hh:hh + 1, :]
            dmats.append(jnp.where(causal, jnp.exp(jnp.minimum(diff, 0.0)), 0.0))
        yield
        a_mats = [jnp.where(strict, lane(betac[c], HEADS + hh) * x[ROWS:] * d, 0.0)
                  for (c, hh), x, d in zip(items, qk_kk, dmats)]
        t_inv, eye = yield from _tri_inverse_many(a_mats, group, row, col)
        for i, (c, hh) in enumerate(items):
            rows, sl = rows_of(c), head_sl(hh)
            q, k, v = qg_s[rows, sl], kg_s[rows, sl], vg_s[rows, sl]
            beta_h = lane(betac[c], HEADS + hh)
            rhs = jnp.concatenate([k * (beta_h * lane(ecumc[c], hh)), v * beta_h], axis=1)
            sol = rhs + _mm(t_inv[i] - eye, rhs)
            w_s[rows, sl] = sol[:, :HEAD_DIM]
            u_s[rows, sl] = sol[:, HEAD_DIM:]
            pg_s[c * HEADS + hh] = qk_kk[i][:ROWS] * dmats[i]
            qg_s[rows, sl] = q * lane(ecumc[c], hh)
            kg_s[rows, sl] = k * lane(tailc[c], hh)
        yield

    def back_sequential(slot):
        proj_s, qr_s, kr_s, qg_s, kg_s, vg_s, cum_s, tot_s, beta_s = (r.at[slot] for r in staged)
        for c in chunks:
            dtotc = jnp.exp(tot_s[rows_of(c), :])
            qs_r, qs_g, ws_g = {}, {}, {}
            for hh in range(HEADS):
                sl = head_sl(hh)
                for gi in range(ngroups):
                    rs = slice(c * ROWS + gi * group, c * ROWS + (gi + 1) * group)
                    r_in, _, g_in, _ = state_refs(c, hh, gi)
                    qs_r[hh, gi] = _mm(qr_s[rs, sl], r_in[...])
                    r = _mm(jnp.concatenate([qg_s[rs, sl], w_s[rs, sl]], axis=0), g_in[...])
                    qs_g[hh, gi], ws_g[hh, gi] = r[:group], r[group:]
            yield
            for hh in range(HEADS):
                rows, sl = rows_of(c), head_sl(hh)
                i = c * HEADS + hh
                vb = _bf(proj_s[rows, OFF_RV + hh * HEAD_DIM:OFF_RV + (hh + 1) * HEAD_DIM])
                v_new = u_s[rows, sl] - cat([ws_g[hh, gi] for gi in range(ngroups)])
                vnb = _bf(v_new)
                o_s[rows, sl] = cat([qs_r[hh, gi] for gi in range(ngroups)]) + _mm(pr_s[i], vb)
                o_s[rows, GROUP_W + hh * HEAD_DIM:GROUP_W + (hh + 1) * HEAD_DIM] = (
                    cat([qs_g[hh, gi] for gi in range(ngroups)]) + _mm(pg_s[i], vnb))
                for gi in range(ngroups):
                    rs = slice(c * ROWS + gi * group, c * ROWS + (gi + 1) * group)
                    ls = slice(gi * group, (gi + 1) * group)
                    r_in, r_out, g_in, g_out = state_refs(c, hh, gi)
                    r_out[...] = cfg["ret_dtot"][hh] * r_in[...] + _mm(kr_s[rs, sl].T, vb[ls])
                    dec = dtotc[gi * group:gi * group + 1, hh:hh + 1]
                    g_out[...] = dec * g_in[...] + _mm(kg_s[rs, sl].T, vnb[ls])
            yield

        for hh in range(2 * HEADS):
            sl = head_sl(hh)
            goff = OFF_RG + hh * HEAD_DIM if hh < HEADS else OFF_Z + (hh - HEADS) * HEAD_DIM
            y = _rms(o_s[:, sl], normw_ref[:, sl]) * _silu(proj_s[:, goff:goff + HEAD_DIM])
            mix_ref[:, sl] = y.astype(mix_ref.dtype)
            if hh % 2 == 1:
                yield

    @pl.when(step == 0)
    def _():
        _run(front(0))

    for parity in (0, 1):
        @pl.when((step > 0) & (step % 2 == parity))
        def _():
            _run(back_parallel(1 - parity))
            _run(back_sequential(1 - parity), front(parity))

    if prompt:
        @pl.when(rec_last)
        def _():
            sret_out[0] = sret_s[...]
            sgdn_out[0] = sgdn_s[...]


def _post_kernel(mix_ref, x_ref, wout_ref, postw_ref, premlpw_ref, wup_ref, wdown_ref, postmlpw_ref, out_ref):
    m = _dot(mix_ref[...], wout_ref[...])
    x1 = x_ref[...] + _rms(m, postw_ref[...])
    h2 = _bf(_rms(x1, premlpw_ref[...]))
    acc = None
    for c in range(D_FF // D_MODEL):
        f = _dot(h2, wup_ref[:, c * D_MODEL:(c + 1) * D_MODEL])
        f = jnp.square(jnp.maximum(f, 0.0))
        part = _dot(_bf(f), wdown_ref[c * D_MODEL:(c + 1) * D_MODEL, :])
        acc = part if acc is None else acc + part
    out_ref[...] = x1 + _rms(acc, postmlpw_ref[...])


def _const_spec(shape):
    return pl.BlockSpec(shape, lambda *_: (0,) * len(shape))


def _retention_tables(group):
    log_gamma = np.log(1.0 - 2.0 ** (-5.0 - np.arange(HEADS, dtype=np.float64)))
    t = np.arange(ROWS) % group
    same = (np.arange(ROWS)[:, None] // group) == (np.arange(ROWS)[None, :] // group)
    causal = same & (t[:, None] >= t[None, :])
    dmat = np.where(causal[None], np.exp((t[:, None] - t[None, :])[None] * log_gamma[:, None, None]), 0.0)
    ecum = np.exp((t[:, None] + 1.0) * log_gamma[None, :])
    tail = np.exp((group - 1.0 - t[:, None]) * log_gamma[None, :])
    rtab = np.stack([np.repeat(ecum, HEAD_DIM, axis=1), np.repeat(tail, HEAD_DIM, axis=1)])
    dtot = tuple(float(np.float32(v)) for v in np.exp(group * log_gamma))
    return jnp.asarray(dmat, F32), jnp.asarray(rtab, F32), dtot


def _rope_tables(length, offset):
    inv_freq = ROPE_BASE ** (-jnp.arange(0, HEAD_DIM, 2, dtype=F32) / HEAD_DIM)
    pos = jnp.arange(length, dtype=F32) + jnp.asarray(offset, F32)
    ang = pos[:, None] * inv_freq[None, :]
    cos, sin = jnp.cos(ang), jnp.sin(ang)
    return jnp.concatenate([cos, cos], axis=-1), jnp.concatenate([-sin, sin], axis=-1)


def _mixer(x, params, *, prompt, states=None):
    b, l, _ = x.shape
    t = b * l
    x2 = x.reshape(t, D_MODEL)
    if prompt:
        group, lb = min(ROWS, l), min(256, l)
        assert l % lb == 0 and lb % ROWS == 0 and l >= 8
        bps = l // lb
        cos, sin = _rope_tables(l, 0)
    else:
        group, lb, bps = l, ROWS, 1
        assert ROWS % group == 0 and group == 8 and t % lb == 0
        cos, sin = _rope_tables(l, PAST_LEN)
        cos, sin = jnp.tile(cos, (lb // l, 1)), jnp.tile(sin, (lb // l, 1))
    nblk = t // lb
    nseq = lb // group
    front_blk = lambda s: jnp.minimum(s, nblk - 1)
    back_blk = lambda s: jnp.maximum(s - 1, 0)
    front_map = lambda s: (front_blk(s), 0)
    back_map = lambda s: (back_blk(s), 0)
    pos_map = lambda s: (front_blk(s) % bps, 0)
    dmat, rtab, dtot = _retention_tables(group)
    cfg = dict(lb=lb, group=group, prompt=prompt, nblk=nblk, blocks_per_seq=bps, ret_dtot=dtot)

    in_specs = [
        pl.BlockSpec((lb, D_MODEL), front_map),
        _const_spec((1, D_MODEL)),
        _const_spec((D_MODEL, MAIN_W)),
        _const_spec((D_MODEL, LANES)),
        pl.BlockSpec((lb, HEAD_DIM), pos_map),
        pl.BlockSpec((lb, HEAD_DIM), pos_map),
        _const_spec((HEADS, ROWS, ROWS)),
        _const_spec((2, ROWS, GROUP_W)),
        _const_spec((CONV_W, CONV_CH)),
        _const_spec((8, LANES)),
        _const_spec((1, 2 * GROUP_W)),
    ]
    args = [x2, params["pre_w"], params["w_main"], params["w_ab"], cos, sin, dmat, rtab,
            params["conv_w"], params["pvec"], params["normw"]]
    state_shape = (HEADS, HEAD_DIM, HEAD_DIM)
    if prompt:
        seq_state_map = lambda s: (back_blk(s) // bps, 0, 0, 0)
        out_specs = [
            pl.BlockSpec((lb, 2 * GROUP_W), back_map),
            pl.BlockSpec((1,) + state_shape, seq_state_map),
            pl.BlockSpec((1,) + state_shape, seq_state_map),
            pl.BlockSpec((8, CONV_CH), lambda s: (front_blk(s) // bps, 0)),
        ]
        out_shape = [
            jax.ShapeDtypeStruct((t, 2 * GROUP_W), BF16),
            jax.ShapeDtypeStruct((b,) + state_shape, F32),
            jax.ShapeDtypeStruct((b,) + state_shape, F32),
            jax.ShapeDtypeStruct((b * 8, CONV_CH), F32),
        ]
    else:
        s_ret, s_gdn, s_conv = states
        st8 = jnp.pad(s_conv.astype(F32), ((0, 0), (8 - (CONV_W - 1), 0), (0, 0))).reshape(t, CONV_CH)
        blk_state_map = lambda s: (back_blk(s), 0, 0, 0)
        in_specs += [
            pl.BlockSpec((nseq,) + state_shape, blk_state_map),
            pl.BlockSpec((nseq,) + state_shape, blk_state_map),
            pl.BlockSpec((lb, CONV_CH), front_map),
        ]
        args += [s_ret.astype(F32), s_gdn.astype(F32), st8]
        out_specs = [
            pl.BlockSpec((lb, 2 * GROUP_W), back_map),
            pl.BlockSpec((nseq,) + state_shape, blk_state_map),
            pl.BlockSpec((nseq,) + state_shape, blk_state_map),
            pl.BlockSpec((lb, CONV_CH), front_map),
        ]
        out_shape = [
            jax.ShapeDtypeStruct((t, 2 * GROUP_W), BF16),
            jax.ShapeDtypeStruct((b,) + state_shape, F32),
            jax.ShapeDtypeStruct((b,) + state_shape, F32),
            jax.ShapeDtypeStruct((t, CONV_CH), F32),
        ]
    nmat = lb // ROWS * HEADS
    scratch = [pltpu.VMEM((2, lb, MAIN_W), F32)]
    scratch += [pltpu.VMEM((2, lb, GROUP_W), F32) for _ in range(5)]
    scratch += [pltpu.VMEM((2, lb, LANES), F32) for _ in range(3)]
    scratch += [
        pltpu.VMEM((lb, 2 * GROUP_W), F32),
        pltpu.VMEM((lb, GROUP_W), F32), pltpu.VMEM((lb, GROUP_W), F32),
        pltpu.VMEM((nmat, ROWS, ROWS), F32), pltpu.VMEM((nmat, ROWS, ROWS), F32),
    ]
    if prompt:
        scratch += [pltpu.VMEM(state_shape, F32), pltpu.VMEM(state_shape, F32), pltpu.VMEM((8, CONV_CH), F32)]

    mix, s_ret_new, s_gdn_new, conv_rows = pl.pallas_call(
        functools.partial(_mixer_kernel, cfg),
        grid=(nblk + 1,),
        in_specs=in_specs,
        out_specs=out_specs,
        out_shape=out_shape,
        scratch_shapes=scratch,
        compiler_params=pltpu.CompilerParams(dimension_semantics=("arbitrary",), vmem_limit_bytes=VMEM_LIMIT),
        name="mixer_prompt" if prompt else "mixer_sample",
    )(*args)
    new_conv = conv_rows.reshape(b, 8, CONV_CH)[:, 8 - (CONV_W - 1):]
    return mix, s_ret_new, s_gdn_new, new_conv


def _post(mix, x, params):
    b, l, _ = x.shape
    t = b * l
    tm = min(512, t)
    assert t % tm == 0
    rows = pl.BlockSpec((tm, D_MODEL), lambda i: (i, 0))
    y = pl.pallas_call(
        _post_kernel,
        grid=(t // tm,),
        in_specs=[rows, rows, _const_spec((2 * GROUP_W, D_MODEL)), _const_spec((1, D_MODEL)),
                  _const_spec((1, D_MODEL)), _const_spec((D_MODEL, D_FF)), _const_spec((D_FF, D_MODEL)),
                  _const_spec((1, D_MODEL))],
        out_specs=rows,
        out_shape=jax.ShapeDtypeStruct((t, D_MODEL), F32),
        compiler_params=pltpu.CompilerParams(dimension_semantics=("arbitrary",), vmem_limit_bytes=VMEM_LIMIT),
        name="post",
    )(mix, x.reshape(t, D_MODEL), params["w_out"], params["post_w"], params["pre_mlp_w"],
      params["w_up"], params["w_down"], params["post_mlp_w"])
    return y.reshape(b, l, D_MODEL)


def _layer_params(l, pre_mix_w, w_in, conv_w, A_log, dt_bias, ret_norm_w, gdn_norm_w, w_out, post_mix_w,
                  pre_mlp_w, w_up, w_down, post_mlp_w):
    w = w_in[l]
    pvec = jnp.zeros((8, LANES), F32)
    pvec = pvec.at[0, :HEADS].set(A_log[l].astype(F32)).at[1, :HEADS].set(dt_bias[l].astype(F32))
    return dict(
        pre_w=pre_mix_w[l].astype(F32)[None],
        w_main=_bf(w[:, :MAIN_W]),
        w_ab=_bf(jnp.pad(w[:, MAIN_W:], ((0, 0), (0, LANES - 2 * HEADS)))),
        conv_w=conv_w[l].astype(F32),
        pvec=pvec,
        normw=jnp.concatenate([jnp.tile(ret_norm_w[l], HEADS), jnp.tile(gdn_norm_w[l], HEADS)]).astype(F32)[None],
        w_out=_bf(w_out[l]),
        post_w=post_mix_w[l].astype(F32)[None],
        pre_mlp_w=pre_mlp_w[l].astype(F32)[None],
        w_up=_bf(w_up[l]),
        w_down=_bf(w_down[l]),
        post_mlp_w=post_mlp_w[l].astype(F32)[None],
    )


def kernel(x_prompt, x_sample, state_ret, state_gdn, state_conv, pre_mix_w, w_in, conv_w, A_log, dt_bias,
           ret_norm_w, gdn_norm_w, w_out, post_mix_w, pre_mlp_w, w_up, w_down, post_mlp_w):
    depth = w_in.shape[0]
    yp, ys = x_prompt, x_sample
    outs = [[] for _ in range(6)]
    for l in range(depth):
        params = _layer_params(l, pre_mix_w, w_in, conv_w, A_log, dt_bias, ret_norm_w, gdn_norm_w, w_out,
                               post_mix_w, pre_mlp_w, w_up, w_down, post_mlp_w)
        mix_p, rp, gp, cp = _mixer(yp, params, prompt=True)
        yp = _post(mix_p, yp, params)
        mix_s, rs, gs, cs = _mixer(ys, params, prompt=False,
                                   states=(state_ret[l], state_gdn[l], state_conv[l]))
        ys = _post(mix_s, ys, params)
        for dst, val, like in zip(outs, (rp, gp, cp, rs, gs, cs),
                                  (state_ret, state_gdn, state_conv) * 2):
            dst.append(val.astype(like.dtype))
    return (yp, ys) + tuple(jnp.stack(o) for o in outs)
```

```python
import functools

import numpy as np
import jax
import jax.numpy as jnp
from jax import lax
from jax.experimental import pallas as pl
from jax.experimental.pallas import tpu as pltpu

D_MODEL = 1024
HEADS = 4
HEAD_DIM = 128
GROUP_W = HEADS * HEAD_DIM
CONV_CH = 3 * GROUP_W
CONV_W = 4
D_FF = 4 * D_MODEL
MAIN_W = 4 * GROUP_W + CONV_CH + GROUP_W
ROPE_BASE = 10000.0
EPS = 1e-6
PAST_LEN = 16384

ROWS = 64
BASE = 8
LANES = 128
FF_TILE = 512
VMEM_LIMIT = 56 * 1024 * 1024

OFF_RQ, OFF_RK, OFF_RV, OFF_RG = 0, GROUP_W, 2 * GROUP_W, 3 * GROUP_W
OFF_CONV = 4 * GROUP_W
OFF_Z = OFF_CONV + CONV_CH

F32 = jnp.float32
BF16 = jnp.bfloat16


def _bf(x):
    return x.astype(BF16)


def _dot(a, b):
    return jnp.dot(a, b, preferred_element_type=F32)


def _dot_nt(a, b):
    return lax.dot_general(a, b, (((1,), (1,)), ((), ())), preferred_element_type=F32)


def _split(x, n):
    if x.dtype == BF16:
        return [x]
    parts, r = [], x
    for i in range(n):
        p = r.astype(BF16)
        parts.append(p)
        if i + 1 < n:
            r = r - p.astype(F32)
    return parts


def _mm(a, b, pa=1, pb=1, nt=False):
    ap, bp = _split(a, pa), _split(b, pb)
    n = max(len(ap), len(bp))
    acc = None
    for i, x in enumerate(ap):
        for j, y in enumerate(bp):
            if i + j < n:
                t = _dot_nt(x, y) if nt else _dot(x, y)
                acc = t if acc is None else acc + t
    return acc


def _rms(x, w):
    return x * lax.rsqrt(jnp.mean(x * x, axis=-1, keepdims=True) + EPS) * w


def _silu(x):
    return x / (1.0 + jnp.exp(-x))


def _interleave(main, filler):
    for n in main:
        for _ in range(n or 0):
            next(filler, None)
    for _ in filler:
        pass


def _tri_inverse_many(mats, group, row, col):
    eye = (row == col).astype(F32)
    mm = functools.partial(_mm, pa=2, pb=2)
    blk = (row // BASE) == (col // BASE)
    n8 = [jnp.where(blk, -a, 0.0) for a in mats]
    p2 = [mm(n, n) for n in n8]
    yield 0
    np2 = [mm(n, p) for n, p in zip(n8, p2)]
    p4 = [mm(p, p) for p in p2]
    yield 0
    e = [n + p + x for n, p, x in zip(n8, p2, np2)]
    ep4 = [mm(x, p) for x, p in zip(e, p4)]
    yield 0
    t = [x + p + y + eye for x, p, y in zip(e, p4, ep4)]
    s = BASE
    while s < group:
        off = ((row // (2 * s)) == (col // (2 * s))) & ((row // s) != (col // s))
        a_off = [jnp.where(off, a, 0.0) for a in mats]
        ta = [mm(x, y) for x, y in zip(t, a_off)]
        yield 0
        tat = [mm(x, y) for x, y in zip(ta, t)]
        yield 0
        t = [x - y for x, y in zip(t, tat)]
        s *= 2
    return t, eye


def _layer_kernel(cfg, *refs):
    lb, group, prompt, nblk, bps = cfg["lb"], cfg["group"], cfg["prompt"], cfg["nblk"], cfg["blocks_per_seq"]
    fill = cfg["fill"]
    ngroups = ROWS // group
    chunks = list(range(lb // ROWS))
    it = iter(refs)
    x_ref, xres_ref, prew_ref, wmain_ref, wab_ref, cos_ref, sin_ref = (next(it) for _ in range(7))
    dmat_ref, rtab_ref, convw_ref, pvec_ref, normw_ref = (next(it) for _ in range(5))
    wout_ref, postw_ref, premlpw_ref, wup_ref, wdown_ref, postmlpw_ref = (next(it) for _ in range(6))
    if not prompt:
        sret_in, sgdn_in, st8_ref = (next(it) for _ in range(3))
    y_ref, sret_out, sgdn_out, conv_out = (next(it) for _ in range(4))
    proj_s, qr_s, kr_s, qg_s, kg_s, vg_s, cum_s, tot_s, beta_s, o_s = (next(it) for _ in range(10))
    w_s, u_s, pr_s, pg_s = (next(it) for _ in range(4))
    mix_s, x1_s, h2_s, acc_s = (next(it) for _ in range(4))
    if prompt:
        sret_s, sgdn_s, prev8_s = (next(it) for _ in range(3))

    step = pl.program_id(0)
    rows_of = lambda c: slice(c * ROWS, (c + 1) * ROWS)
    head_sl = lambda hh: slice(hh * HEAD_DIM, (hh + 1) * HEAD_DIM)
    lane = lambda x, n: x[:, n:n + 1]
    cat = lambda xs: xs[0] if len(xs) == 1 else jnp.concatenate(xs, axis=0)

    if prompt:
        @pl.when((step < nblk) & (step % bps == 0))
        def _():
            sret_s[...] = jnp.zeros_like(sret_s)
            sgdn_s[...] = jnp.zeros_like(sgdn_s)
            prev8_s[...] = jnp.zeros_like(prev8_s)

    def state_refs(c, hh, gi):
        if prompt:
            return (sret_s.at[hh], sret_s.at[hh], sgdn_s.at[hh], sgdn_s.at[hh])
        n = c * ngroups + gi
        return (sret_in.at[n, hh], sret_out.at[n, hh], sgdn_in.at[n, hh], sgdn_out.at[n, hh])

    def mixer():
        row = lax.broadcasted_iota(jnp.int32, (ROWS, ROWS), 0)
        col = lax.broadcasted_iota(jnp.int32, (ROWS, ROWS), 1)
        same = (row // group) == (col // group)
        causal = same & (row >= col)
        strict = same & (row > col)
        sum_mat = _bf(jnp.concatenate([causal.astype(F32), same.astype(F32)], axis=0))
        sel = _bf((lax.broadcasted_iota(jnp.int32, (8, LANES), 0)
                   == lax.broadcasted_iota(jnp.int32, (8, LANES), 1)).astype(F32))

        h = _bf(_rms(x_ref[...], prew_ref[...]))
        proj_s[...] = _dot(h, wmain_ref[...])
        ab = _dot(h, wab_ref[...])
        yield fill["proj"]

        cos, sin = cos_ref[...], sin_ref[...]
        for hh in range(HEADS):
            sl = head_sl(hh)
            q = proj_s[:, OFF_RQ + hh * HEAD_DIM:OFF_RQ + (hh + 1) * HEAD_DIM]
            qr_s[:, sl] = q * cos + pltpu.roll(q, HEAD_DIM // 2, 1) * sin
            k = proj_s[:, OFF_RK + hh * HEAD_DIM:OFF_RK + (hh + 1) * HEAD_DIM]
            kr_s[:, sl] = (k * cos + pltpu.roll(k, HEAD_DIM // 2, 1) * sin) * (HEAD_DIM ** -0.5)
        yield fill["elementwise"]

        t_in_group = lax.broadcasted_iota(jnp.int32, (lb, 1), 0) % group
        for part, dst in enumerate((qg_s, kg_s, vg_s)):
            cols = slice(OFF_CONV + part * GROUP_W, OFF_CONV + (part + 1) * GROUP_W)
            xq = proj_s[:, cols]
            cw = convw_ref[:, part * GROUP_W:(part + 1) * GROUP_W]
            acc = xq * cw[CONV_W - 1:CONV_W]
            if prompt:
                ext = jnp.concatenate([prev8_s[:, part * GROUP_W:(part + 1) * GROUP_W], xq], axis=0)
                for s in range(1, CONV_W):
                    sh = pltpu.roll(ext, s, 0)[8:]
                    acc = acc + sh * cw[CONV_W - 1 - s:CONV_W - s]
            else:
                st8 = st8_ref[:, part * GROUP_W:(part + 1) * GROUP_W]
                for s in range(1, CONV_W):
                    sh = jnp.where(t_in_group < s, pltpu.roll(st8, lb - 8 + s, 0), pltpu.roll(xq, s, 0))
                    acc = acc + sh * cw[CONV_W - 1 - s:CONV_W - s]
            conv = _silu(acc)
            for hh in range(HEADS):
                sl = head_sl(hh)
                c = conv[:, sl]
                if part < 2:
                    c = c * lax.rsqrt(jnp.sum(c * c, axis=-1, keepdims=True) + EPS)
                if part == 0:
                    c = c * (HEAD_DIM ** -0.5)
                dst[:, sl] = c
            yield fill["elementwise"]
        if prompt:
            prev8_s[...] = proj_s[lb - 8:lb, OFF_CONV:OFF_CONV + CONV_CH]
            conv_out[...] = proj_s[lb - 8:lb, OFF_CONV:OFF_CONV + CONV_CH]
        else:
            conv_out[...] = proj_s[:, OFF_CONV:OFF_CONV + CONV_CH]

        pv = pvec_ref[...]
        a_plus = ab + pv[1:2]
        softplus = jnp.maximum(a_plus, 0.0) + jnp.log1p(jnp.exp(-jnp.abs(a_plus)))
        g = -jnp.exp(pv[0:1]) * softplus
        beta_s[...] = 1.0 / (1.0 + jnp.exp(-ab))
        for c in chunks:
            r = _mm(sum_mat, g[rows_of(c)], 1, 3)
            cum_s[rows_of(c), :] = r[:ROWS]
            tot_s[rows_of(c), :] = r[ROWS:]
        yield fill["elementwise"]

        items = [(c, hh) for c in chunks for hh in range(HEADS)]
        cumc = {c: cum_s[rows_of(c), :] for c in chunks}
        totc = {c: tot_s[rows_of(c), :] for c in chunks}
        betac = {c: beta_s[rows_of(c), :] for c in chunks}
        ecumc = {c: jnp.exp(cumc[c]) for c in chunks}
        tailc = {c: jnp.exp(totc[c] - cumc[c]) for c in chunks}
        cumrow = {c: _mm(sel, cumc[c], 1, 3, nt=True) for c in chunks}
        for c, hh in items:
            rows, sl = rows_of(c), head_sl(hh)
            q, k = qr_s[rows, sl], kr_s[rows, sl]
            pr_s[c * HEADS + hh] = _mm(q, k, nt=True) * dmat_ref[hh]
            qr_s[rows, sl] = q * rtab_ref[0, :, sl]
            kr_s[rows, sl] = k * rtab_ref[1, :, sl]
        yield 0
        qk_kk, dmats = [], []
        for c, hh in items:
            rows, sl = rows_of(c), head_sl(hh)
            kb = _bf(kg_s[rows, sl])
            qk_kk.append(_mm(jnp.concatenate([_bf(qg_s[rows, sl]), kb], axis=0), kb, nt=True))
            diff = lane(cumc[c], hh) - cumrow[c][hh:hh + 1, :]
            dmats.append(jnp.where(causal, jnp.exp(jnp.minimum(diff, 0.0)), 0.0))
        yield 0
        a_mats = [jnp.where(strict, lane(betac[c], HEADS + hh) * x[ROWS:] * d, 0.0)
                  for (c, hh), x, d in zip(items, qk_kk, dmats)]
        t_inv, eye = yield from _tri_inverse_many(a_mats, group, row, col)
        for i, (c, hh) in enumerate(items):
            rows, sl = rows_of(c), head_sl(hh)
            q, k, v = qg_s[rows, sl], kg_s[rows, sl], vg_s[rows, sl]
            beta_h = lane(betac[c], HEADS + hh)
            rhs = jnp.concatenate([k * (beta_h * lane(ecumc[c], hh)), v * beta_h], axis=1)
            sol = rhs + _mm(t_inv[i] - eye, rhs)
            w_s[rows, sl] = sol[:, :HEAD_DIM]
            u_s[rows, sl] = sol[:, HEAD_DIM:]
            pg_s[c * HEADS + hh] = qk_kk[i][:ROWS] * dmats[i]
            qg_s[rows, sl] = q * lane(ecumc[c], hh)
            kg_s[rows, sl] = k * lane(tailc[c], hh)
        yield 0

        for c in chunks:
            dtotc = jnp.exp(totc[c])
            qs_r, qs_g, ws_g = {}, {}, {}
            for hh in range(HEADS):
                sl = head_sl(hh)
                for gi in range(ngroups):
                    rs = slice(c * ROWS + gi * group, c * ROWS + (gi + 1) * group)
                    r_in, _, g_in, _ = state_refs(c, hh, gi)
                    qs_r[hh, gi] = _mm(qr_s[rs, sl], r_in[...])
                    r = _mm(jnp.concatenate([qg_s[rs, sl], w_s[rs, sl]], axis=0), g_in[...])
                    qs_g[hh, gi], ws_g[hh, gi] = r[:group], r[group:]
            yield fill["sequential"]
            for hh in range(HEADS):
                rows, sl = rows_of(c), head_sl(hh)
                i = c * HEADS + hh
                vb = _bf(proj_s[rows, OFF_RV + hh * HEAD_DIM:OFF_RV + (hh + 1) * HEAD_DIM])
                v_new = u_s[rows, sl] - cat([ws_g[hh, gi] for gi in range(ngroups)])
                vnb = _bf(v_new)
                o_s[rows, sl] = cat([qs_r[hh, gi] for gi in range(ngroups)]) + _mm(pr_s[i], vb)
                o_s[rows, GROUP_W + hh * HEAD_DIM:GROUP_W + (hh + 1) * HEAD_DIM] = (
                    cat([qs_g[hh, gi] for gi in range(ngroups)]) + _mm(pg_s[i], vnb))
                for gi in range(ngroups):
                    rs = slice(c * ROWS + gi * group, c * ROWS + (gi + 1) * group)
                    ls = slice(gi * group, (gi + 1) * group)
                    r_in, r_out, g_in, g_out = state_refs(c, hh, gi)
                    r_out[...] = cfg["ret_dtot"][hh] * r_in[...] + _mm(kr_s[rs, sl].T, vb[ls])
                    dec = dtotc[gi * group:gi * group + 1, hh:hh + 1]
                    g_out[...] = dec * g_in[...] + _mm(kg_s[rs, sl].T, vnb[ls])
            yield fill["sequential"]

        for hh in range(2 * HEADS):
            sl = head_sl(hh)
            goff = OFF_RG + hh * HEAD_DIM if hh < HEADS else OFF_Z + (hh - HEADS) * HEAD_DIM
            y = _rms(o_s[:, sl], normw_ref[:, sl]) * _silu(proj_s[:, goff:goff + HEAD_DIM])
            mix_s[:, sl] = y.astype(mix_s.dtype)
            if hh % 2 == 1:
                yield fill["elementwise"]

    def channel():
        m = _dot(mix_s[...], wout_ref[...])
        x1 = xres_ref[...] + _rms(m, postw_ref[...])
        x1_s[...] = x1
        h2_s[...] = _bf(_rms(x1, premlpw_ref[...]))
        yield
        for c in range(D_FF // FF_TILE):
            f = _dot(h2_s[...], wup_ref[:, c * FF_TILE:(c + 1) * FF_TILE])
            f = jnp.square(jnp.maximum(f, 0.0))
            part = _dot(_bf(f), wdown_ref[c * FF_TILE:(c + 1) * FF_TILE, :])
            if c == 0:
                acc_s[...] = part
            else:
                acc_s[...] += part
            yield
        y_ref[...] = x1_s[...] + _rms(acc_s[...], postmlpw_ref[...])
        yield

    @pl.when(step == 0)
    def _():
        for _ in mixer():
            pass

    @pl.when((step > 0) & (step < nblk))
    def _():
        _interleave(mixer(), channel())

    @pl.when(step == nblk)
    def _():
        for _ in channel():
            pass

    if prompt:
        @pl.when((step < nblk) & (step % bps == bps - 1))
        def _():
            sret_out[0] = sret_s[...]
            sgdn_out[0] = sgdn_s[...]


def _const_spec(shape, single=False):
    index_map = lambda *_: (0,) * len(shape)
    if single:
        return pl.BlockSpec(shape, index_map, pipeline_mode=pl.Buffered(1))
    return pl.BlockSpec(shape, index_map)


def _retention_tables(group):
    log_gamma = np.log(1.0 - 2.0 ** (-5.0 - np.arange(HEADS, dtype=np.float64)))
    t = np.arange(ROWS) % group
    same = (np.arange(ROWS)[:, None] // group) == (np.arange(ROWS)[None, :] // group)
    causal = same & (t[:, None] >= t[None, :])
    dmat = np.where(causal[None], np.exp((t[:, None] - t[None, :])[None] * log_gamma[:, None, None]), 0.0)
    ecum = np.exp((t[:, None] + 1.0) * log_gamma[None, :])
    tail = np.exp((group - 1.0 - t[:, None]) * log_gamma[None, :])
    rtab = np.stack([np.repeat(ecum, HEAD_DIM, axis=1), np.repeat(tail, HEAD_DIM, axis=1)])
    dtot = tuple(float(np.float32(v)) for v in np.exp(group * log_gamma))
    return jnp.asarray(dmat, F32), jnp.asarray(rtab, F32), dtot


def _rope_tables(length, offset):
    inv_freq = ROPE_BASE ** (-jnp.arange(0, HEAD_DIM, 2, dtype=F32) / HEAD_DIM)
    pos = jnp.arange(length, dtype=F32) + jnp.asarray(offset, F32)
    ang = pos[:, None] * inv_freq[None, :]
    cos, sin = jnp.cos(ang), jnp.sin(ang)
    return jnp.concatenate([cos, cos], axis=-1), jnp.concatenate([-sin, sin], axis=-1)


def _layer(x, params, *, prompt, states=None):
    b, l, _ = x.shape
    t = b * l
    x2 = x.reshape(t, D_MODEL)
    if prompt:
        group, lb = min(ROWS, l), min(256, l)
        assert l % lb == 0 and lb % ROWS == 0 and l >= 8
        bps = l // lb
        cos, sin = _rope_tables(l, 0)
        fill = dict(proj=0, elementwise=1, sequential=1)
    else:
        group, lb, bps = l, ROWS, 1
        assert ROWS % group == 0 and group == 8 and t % lb == 0
        cos, sin = _rope_tables(l, PAST_LEN)
        cos, sin = jnp.tile(cos, (lb // l, 1)), jnp.tile(sin, (lb // l, 1))
        fill = dict(proj=1, elementwise=1, sequential=2)
    nblk = t // lb
    nseq = lb // group
    mix_blk = lambda s: jnp.minimum(s, nblk - 1)
    chan_blk = lambda s: jnp.maximum(s - 1, 0)
    mix_map = lambda s: (mix_blk(s), 0)
    chan_map = lambda s: (chan_blk(s), 0)
    pos_map = lambda s: (mix_blk(s) % bps, 0)
    dmat, rtab, dtot = _retention_tables(group)
    cfg = dict(lb=lb, group=group, prompt=prompt, nblk=nblk, blocks_per_seq=bps, ret_dtot=dtot, fill=fill)

    in_specs = [
        pl.BlockSpec((lb, D_MODEL), mix_map),
        pl.BlockSpec((lb, D_MODEL), chan_map),
        _const_spec((1, D_MODEL)),
        _const_spec((D_MODEL, MAIN_W), single=True),
        _const_spec((D_MODEL, LANES), single=True),
        pl.BlockSpec((lb, HEAD_DIM), pos_map),
        pl.BlockSpec((lb, HEAD_DIM), pos_map),
        _const_spec((HEADS, ROWS, ROWS)),
        _const_spec((2, ROWS, GROUP_W)),
        _const_spec((CONV_W, CONV_CH)),
        _const_spec((8, LANES)),
        _const_spec((1, 2 * GROUP_W)),
        _const_spec((2 * GROUP_W, D_MODEL), single=True),
        _const_spec((1, D_MODEL)),
        _const_spec((1, D_MODEL)),
        _const_spec((D_MODEL, D_FF), single=True),
        _const_spec((D_FF, D_MODEL), single=True),
        _const_spec((1, D_MODEL)),
    ]
    args = [x2, x2, params["pre_w"], params["w_main"], params["w_ab"], cos, sin, dmat, rtab,
            params["conv_w"], params["pvec"], params["normw"],
            params["w_out"], params["post_w"], params["pre_mlp_w"], params["w_up"], params["w_down"],
            params["post_mlp_w"]]
    state_shape = (HEADS, HEAD_DIM, HEAD_DIM)
    if prompt:
        seq_state_map = lambda s: (mix_blk(s) // bps, 0, 0, 0)
        out_specs = [
            pl.BlockSpec((lb, D_MODEL), chan_map),
            pl.BlockSpec((1,) + state_shape, seq_state_map),
            pl.BlockSpec((1,) + state_shape, seq_state_map),
            pl.BlockSpec((8, CONV_CH), lambda s: (mix_blk(s) // bps, 0)),
        ]
        out_shape = [
            jax.ShapeDtypeStruct((t, D_MODEL), F32),
            jax.ShapeDtypeStruct((b,) + state_shape, F32),
            jax.ShapeDtypeStruct((b,) + state_shape, F32),
            jax.ShapeDtypeStruct((b * 8, CONV_CH), F32),
        ]
    else:
        s_ret, s_gdn, s_conv = states
        st8 = jnp.pad(s_conv.astype(F32), ((0, 0), (8 - (CONV_W - 1), 0), (0, 0))).reshape(t, CONV_CH)
        blk_state_map = lambda s: (mix_blk(s), 0, 0, 0)
        in_specs += [
            pl.BlockSpec((nseq,) + state_shape, blk_state_map),
            pl.BlockSpec((nseq,) + state_shape, blk_state_map),
            pl.BlockSpec((lb, CONV_CH), mix_map),
        ]
        args += [s_ret.astype(F32), s_gdn.astype(F32), st8]
        out_specs = [
            pl.BlockSpec((lb, D_MODEL), chan_map),
            pl.BlockSpec((nseq,) + state_shape, blk_state_map),
            pl.BlockSpec((nseq,) + state_shape, blk_state_map),
            pl.BlockSpec((lb, CONV_CH), mix_map),
        ]
        out_shape = [
            jax.ShapeDtypeStruct((t, D_MODEL), F32),
            jax.ShapeDtypeStruct((b,) + state_shape, F32),
            jax.ShapeDtypeStruct((b,) + state_shape, F32),
            jax.ShapeDtypeStruct((t, CONV_CH), F32),
        ]
    nmat = lb // ROWS * HEADS
    scratch = [
        pltpu.VMEM((lb, MAIN_W), F32),
        pltpu.VMEM((lb, GROUP_W), F32), pltpu.VMEM((lb, GROUP_W), F32),
        pltpu.VMEM((lb, GROUP_W), F32), pltpu.VMEM((lb, GROUP_W), F32), pltpu.VMEM((lb, GROUP_W), F32),
        pltpu.VMEM((lb, LANES), F32), pltpu.VMEM((lb, LANES), F32), pltpu.VMEM((lb, LANES), F32),
        pltpu.VMEM((lb, 2 * GROUP_W), F32),
        pltpu.VMEM((lb, GROUP_W), F32), pltpu.VMEM((lb, GROUP_W), F32),
        pltpu.VMEM((nmat, ROWS, ROWS), F32), pltpu.VMEM((nmat, ROWS, ROWS), F32),
        pltpu.VMEM((lb, 2 * GROUP_W), BF16), pltpu.VMEM((lb, D_MODEL), F32),
        pltpu.VMEM((lb, D_MODEL), BF16), pltpu.VMEM((lb, D_MODEL), F32),
    ]
    if prompt:
        scratch += [pltpu.VMEM(state_shape, F32), pltpu.VMEM(state_shape, F32), pltpu.VMEM((8, CONV_CH), F32)]

    y, s_ret_new, s_gdn_new, conv_rows = pl.pallas_call(
        functools.partial(_layer_kernel, cfg),
        grid=(nblk + 1,),
        in_specs=in_specs,
        out_specs=out_specs,
        out_shape=out_shape,
        scratch_shapes=scratch,
        compiler_params=pltpu.CompilerParams(dimension_semantics=("arbitrary",), vmem_limit_bytes=VMEM_LIMIT),
        name="layer_prompt" if prompt else "layer_sample",
    )(*args)
    new_conv = conv_rows.reshape(b, 8, CONV_CH)[:, 8 - (CONV_W - 1):]
    return y.reshape(b, l, D_MODEL), s_ret_new, s_gdn_new, new_conv


def _layer_params(l, pre_mix_w, w_in, conv_w, A_log, dt_bias, ret_norm_w, gdn_norm_w, w_out, post_mix_w,
                  pre_mlp_w, w_up, w_down, post_mlp_w):
    w = w_in[l]
    pvec = jnp.zeros((8, LANES), F32)
    pvec = pvec.at[0, :HEADS].set(A_log[l].astype(F32)).at[1, :HEADS].set(dt_bias[l].astype(F32))
    return dict(
        pre_w=pre_mix_w[l].astype(F32)[None],
        w_main=_bf(w[:, :MAIN_W]),
        w_ab=_bf(jnp.pad(w[:, MAIN_W:], ((0, 0), (0, LANES - 2 * HEADS)))),
        conv_w=conv_w[l].astype(F32),
        pvec=pvec,
        normw=jnp.concatenate([jnp.tile(ret_norm_w[l], HEADS), jnp.tile(gdn_norm_w[l], HEADS)]).astype(F32)[None],
        w_out=_bf(w_out[l]),
        post_w=post_mix_w[l].astype(F32)[None],
        pre_mlp_w=pre_mlp_w[l].astype(F32)[None],
        w_up=_bf(w_up[l]),
        w_down=_bf(w_down[l]),
        post_mlp_w=post_mlp_w[l].astype(F32)[None],
    )


def kernel(x_prompt, x_sample, state_ret, state_gdn, state_conv, pre_mix_w, w_in, conv_w, A_log, dt_bias,
           ret_norm_w, gdn_norm_w, w_out, post_mix_w, pre_mlp_w, w_up, w_down, post_mlp_w):
    depth = w_in.shape[0]
    yp, ys = x_prompt, x_sample
    outs = [[] for _ in range(6)]
    for l in range(depth):
        params = _layer_params(l, pre_mix_w, w_in, conv_w, A_log, dt_bias, ret_norm_w, gdn_norm_w, w_out,
                               post_mix_w, pre_mlp_w, w_up, w_down, post_mlp_w)
        yp, rp, gp, cp = _layer(yp, params, prompt=True)
        ys, rs, gs, cs = _layer(ys, params, prompt=False,
                                states=(state_ret[l], state_gdn[l], state_conv[l]))
        for dst, val, like in zip(outs, (rp, gp, cp, rs, gs, cs),
                                  (state_ret, state_gdn, state_conv) * 2):
            dst.append(val.astype(like.dtype))
    return (yp, ys) + tuple(jnp.stack(o) for o in outs)
```

```python
import functools

import numpy as np
import jax
import jax.numpy as jnp
from jax import lax
from jax.experimental import pallas as pl
from jax.experimental.pallas import tpu as pltpu

D_MODEL = 1024
HEADS = 4
HEAD_DIM = 128
GROUP_W = HEADS * HEAD_DIM
CONV_CH = 3 * GROUP_W
CONV_W = 4
D_FF = 4 * D_MODEL
MAIN_W = 4 * GROUP_W + CONV_CH + GROUP_W
IN_W = MAIN_W + 2 * HEADS
ROPE_BASE = 10000.0
EPS = 1e-6
PAST_LEN = 16384

ROWS = 64
BASE = 8
LANES = 128
VMEM_LIMIT = 58 * 1024 * 1024

OFF_RQ, OFF_RK, OFF_RV, OFF_RG = 0, GROUP_W, 2 * GROUP_W, 3 * GROUP_W
OFF_CONV = 4 * GROUP_W
OFF_Z = OFF_CONV + CONV_CH

F32 = jnp.float32
BF16 = jnp.bfloat16


def _bf(x):
    return x.astype(BF16)


def _dot(a, b):
    return jnp.dot(a, b, preferred_element_type=F32)


def _dot_nt(a, b):
    return lax.dot_general(a, b, (((1,), (1,)), ((), ())), preferred_element_type=F32)


def _split(x, n):
    parts, r = [], x
    for i in range(n):
        p = r.astype(BF16)
        parts.append(p)
        if i + 1 < n:
            r = r - p.astype(F32)
    return parts


def _mm(a, b, nt=False):
    return _dot_nt(_bf(a), _bf(b)) if nt else _dot(_bf(a), _bf(b))


def _mm3(a, b):
    a_hi = _bf(a).astype(F32)
    lhs = _bf(jnp.concatenate([a_hi, a - a_hi, a_hi], axis=1))
    b_hi, b_lo = _split(b, 2)
    return _dot(lhs, jnp.concatenate([b_hi, b_hi, b_lo], axis=0))


def _rms(x, w):
    return x * lax.rsqrt(jnp.mean(x * x, axis=-1, keepdims=True) + EPS) * w


def _silu(x):
    return x / (1.0 + jnp.exp(-x))


def _tri_inverse_many(mats, group, row, col):
    eye = (row == col).astype(F32)
    blk = (row // BASE) == (col // BASE)
    n8 = [jnp.where(blk, -a, 0.0) for a in mats]
    p2 = [_mm3(n, n) for n in n8]
    np2 = [_mm3(n, p) for n, p in zip(n8, p2)]
    p4 = [_mm3(p, p) for p in p2]
    e = [n + p + x for n, p, x in zip(n8, p2, np2)]
    ep4 = [_mm3(x, p) for x, p in zip(e, p4)]
    t = [x + p + y + eye for x, p, y in zip(e, p4, ep4)]
    s = BASE
    while s < group:
        off = ((row // (2 * s)) == (col // (2 * s))) & ((row // s) != (col // s))
        a_off = [jnp.where(off, a, 0.0) for a in mats]
        ta = [_mm3(x, y) for x, y in zip(t, a_off)]
        tat = [_mm3(x, y) for x, y in zip(ta, t)]
        t = [x - y for x, y in zip(t, tat)]
        s *= 2
    return t, eye


def _mixer_kernel(cfg, *refs):
    lb, group, prompt = cfg["lb"], cfg["group"], cfg["prompt"]
    ngroups = ROWS // group
    chunks = list(range(lb // ROWS))
    it = iter(refs)
    x_ref, prew_ref, win_ref, wab_ref, cos_ref, sin_ref = (next(it) for _ in range(6))
    dmat_ref, rtab_ref, convw_ref, pvec_ref, normw_ref = (next(it) for _ in range(5))
    if not prompt:
        sret_in, sgdn_in, st8_ref = (next(it) for _ in range(3))
    mix_ref, sret_out, sgdn_out, conv_out = (next(it) for _ in range(4))
    proj_s, qr_s, kr_s, qg_s, kg_s, vg_s, cum_s, tot_s, beta_s, o_s = (next(it) for _ in range(10))
    w_s, u_s, pr_s, pg_s = (next(it) for _ in range(4))
    if prompt:
        sret_s, sgdn_s, prev8_s = (next(it) for _ in range(3))

        @pl.when(pl.program_id(1) == 0)
        def _():
            sret_s[...] = jnp.zeros_like(sret_s)
            sgdn_s[...] = jnp.zeros_like(sgdn_s)
            prev8_s[...] = jnp.zeros_like(prev8_s)

    rows_of = lambda c: slice(c * ROWS, (c + 1) * ROWS)
    head_sl = lambda hh: slice(hh * HEAD_DIM, (hh + 1) * HEAD_DIM)
    lane = lambda x, n: x[:, n:n + 1]
    cat = lambda xs: xs[0] if len(xs) == 1 else jnp.concatenate(xs, axis=0)

    h = _bf(_rms(x_ref[...], prew_ref[...]))
    proj_s[...] = _dot(h, win_ref[:, :MAIN_W])
    ab = _dot(h, wab_ref[...])

    cos, sin = cos_ref[...], sin_ref[...]
    for hh in range(HEADS):
        sl = head_sl(hh)
        q = proj_s[:, OFF_RQ + hh * HEAD_DIM:OFF_RQ + (hh + 1) * HEAD_DIM]
        qr_s[:, sl] = q * cos + pltpu.roll(q, HEAD_DIM // 2, 1) * sin
        k = proj_s[:, OFF_RK + hh * HEAD_DIM:OFF_RK + (hh + 1) * HEAD_DIM]
        kr_s[:, sl] = (k * cos + pltpu.roll(k, HEAD_DIM // 2, 1) * sin) * (HEAD_DIM ** -0.5)

    t_in_group = lax.broadcasted_iota(jnp.int32, (lb, 1), 0) % group
    for part, dst in enumerate((qg_s, kg_s, vg_s)):
        cols = slice(OFF_CONV + part * GROUP_W, OFF_CONV + (part + 1) * GROUP_W)
        xq = proj_s[:, cols]
        cw = convw_ref[:, part * GROUP_W:(part + 1) * GROUP_W]
        acc = xq * cw[CONV_W - 1:CONV_W]
        if prompt:
            ext = jnp.concatenate([prev8_s[:, part * GROUP_W:(part + 1) * GROUP_W], xq], axis=0)
            for s in range(1, CONV_W):
                sh = pltpu.roll(ext, s, 0)[8:]
                acc = acc + sh * cw[CONV_W - 1 - s:CONV_W - s]
        else:
            st8 = st8_ref[:, part * GROUP_W:(part + 1) * GROUP_W]
            for s in range(1, CONV_W):
                sh = jnp.where(t_in_group < s, pltpu.roll(st8, lb - 8 + s, 0), pltpu.roll(xq, s, 0))
                acc = acc + sh * cw[CONV_W - 1 - s:CONV_W - s]
        conv = _silu(acc)
        for hh in range(HEADS):
            sl = head_sl(hh)
            c = conv[:, sl]
            if part < 2:
                c = c * lax.rsqrt(jnp.sum(c * c, axis=-1, keepdims=True) + EPS)
            if part == 0:
                c = c * (HEAD_DIM ** -0.5)
            dst[:, sl] = c
    if prompt:
        prev8_s[...] = proj_s[lb - 8:lb, OFF_CONV:OFF_CONV + CONV_CH]
        conv_out[...] = proj_s[lb - 8:lb, OFF_CONV:OFF_CONV + CONV_CH]
    else:
        conv_out[...] = proj_s[:, OFF_CONV:OFF_CONV + CONV_CH]

    pv = pvec_ref[...]
    a_plus = ab + pv[1:2]
    softplus = jnp.maximum(a_plus, 0.0) + jnp.log1p(jnp.exp(-jnp.abs(a_plus)))
    g = -jnp.exp(pv[0:1]) * softplus
    beta_s[...] = 1.0 / (1.0 + jnp.exp(-ab))

    row = lax.broadcasted_iota(jnp.int32, (ROWS, ROWS), 0)
    col = lax.broadcasted_iota(jnp.int32, (ROWS, ROWS), 1)
    same = (row // group) == (col // group)
    causal = same & (row >= col)
    strict = same & (row > col)
    sum_mat = _bf(jnp.concatenate([causal.astype(F32), same.astype(F32)], axis=0))
    sum_mat3 = jnp.concatenate([sum_mat] * 3, axis=1)
    for c in chunks:
        r = _dot(sum_mat3, jnp.concatenate(_split(g[rows_of(c)], 3), axis=0))
        cum_s[rows_of(c), :] = r[:ROWS]
        tot_s[rows_of(c), :] = r[ROWS:]
    sel = _bf((lax.broadcasted_iota(jnp.int32, (8, LANES), 0)
               == lax.broadcasted_iota(jnp.int32, (8, LANES), 1)).astype(F32))
    sel3 = jnp.concatenate([sel] * 3, axis=1)

    items = [(c, hh) for c in chunks for hh in range(HEADS)]
    cumc = [cum_s[rows_of(c), :] for c in chunks]
    totc = [tot_s[rows_of(c), :] for c in chunks]
    betac = [beta_s[rows_of(c), :] for c in chunks]
    ecumc = [jnp.exp(x) for x in cumc]
    tailc = [jnp.exp(t - x) for t, x in zip(totc, cumc)]
    cumrow = [_dot_nt(sel3, jnp.concatenate(_split(x, 3), axis=1)) for x in cumc]

    for c, hh in items:
        rows, sl = rows_of(c), head_sl(hh)
        q, k = qr_s[rows, sl], kr_s[rows, sl]
        pr_s[c * HEADS + hh] = _mm(q, k, nt=True) * dmat_ref[hh]
        qr_s[rows, sl] = q * rtab_ref[0, :, sl]
        kr_s[rows, sl] = k * rtab_ref[1, :, sl]

    qk_kk, dmats = [], []
    for c, hh in items:
        rows, sl = rows_of(c), head_sl(hh)
        kb = _bf(kg_s[rows, sl])
        qk_kk.append(_dot_nt(jnp.concatenate([_bf(qg_s[rows, sl]), kb], axis=0), kb))
        diff = lane(cumc[c], hh) - cumrow[c][hh:hh + 1, :]
        dmats.append(jnp.where(causal, jnp.exp(jnp.minimum(diff, 0.0)), 0.0))
    a_mats = [jnp.where(strict, lane(betac[c], HEADS + hh) * x[ROWS:] * d, 0.0)
              for (c, hh), x, d in zip(items, qk_kk, dmats)]
    t_inv, eye = _tri_inverse_many(a_mats, group, row, col)
    for i, (c, hh) in enumerate(items):
        rows, sl = rows_of(c), head_sl(hh)
        q, k, v = qg_s[rows, sl], kg_s[rows, sl], vg_s[rows, sl]
        beta_h = lane(betac[c], HEADS + hh)
        rhs = jnp.concatenate([k * (beta_h * lane(ecumc[c], hh)), v * beta_h], axis=1)
        sol = rhs + _mm(t_inv[i] - eye, rhs)
        w_s[rows, sl] = sol[:, :HEAD_DIM]
        u_s[rows, sl] = sol[:, HEAD_DIM:]
        pg_s[c * HEADS + hh] = qk_kk[i][:ROWS] * dmats[i]
        qg_s[rows, sl] = q * lane(ecumc[c], hh)
        kg_s[rows, sl] = k * lane(tailc[c], hh)

    def state_refs(c, hh, gi):
        if prompt:
            return (sret_s.at[hh], sret_s.at[hh], sgdn_s.at[hh], sgdn_s.at[hh])
        n = c * ngroups + gi
        return (sret_in.at[n, hh], sret_out.at[n, hh], sgdn_in.at[n, hh], sgdn_out.at[n, hh])

    for c in chunks:
        dtotc = jnp.exp(totc[c])
        qs_r, qs_g, ws_g = {}, {}, {}
        for hh in range(HEADS):
            sl = head_sl(hh)
            for gi in range(ngroups):
                rs = slice(c * ROWS + gi * group, c * ROWS + (gi + 1) * group)
                r_in, _, g_in, _ = state_refs(c, hh, gi)
                qs_r[hh, gi] = _mm(qr_s[rs, sl], r_in[...])
                r = _mm(jnp.concatenate([qg_s[rs, sl], w_s[rs, sl]], axis=0), g_in[...])
                qs_g[hh, gi], ws_g[hh, gi] = r[:group], r[group:]
        for hh in range(HEADS):
            rows, sl = rows_of(c), head_sl(hh)
            i = c * HEADS + hh
            vb = _bf(proj_s[rows, OFF_RV + hh * HEAD_DIM:OFF_RV + (hh + 1) * HEAD_DIM])
            v_new = u_s[rows, sl] - cat([ws_g[hh, gi] for gi in range(ngroups)])
            vnb = _bf(v_new)
            o_s[rows, sl] = cat([qs_r[hh, gi] for gi in range(ngroups)]) + _mm(pr_s[i], vb)
            o_s[rows, GROUP_W + hh * HEAD_DIM:GROUP_W + (hh + 1) * HEAD_DIM] = (
                cat([qs_g[hh, gi] for gi in range(ngroups)]) + _mm(pg_s[i], vnb))
            for gi in range(ngroups):
                rs = slice(c * ROWS + gi * group, c * ROWS + (gi + 1) * group)
                ls = slice(gi * group, (gi + 1) * group)
                r_in, r_out, g_in, g_out = state_refs(c, hh, gi)
                r_out[...] = cfg["ret_dtot"][hh] * r_in[...] + _mm(kr_s[rs, sl].T, vb[ls])
                dec = dtotc[gi * group:gi * group + 1, hh:hh + 1]
                g_out[...] = dec * g_in[...] + _mm(kg_s[rs, sl].T, vnb[ls])

    for hh in range(2 * HEADS):
        sl = head_sl(hh)
        goff = OFF_RG + hh * HEAD_DIM if hh < HEADS else OFF_Z + (hh - HEADS) * HEAD_DIM
        y = _rms(o_s[:, sl], normw_ref[:, sl]) * _silu(proj_s[:, goff:goff + HEAD_DIM])
        mix_ref[:, sl] = y.astype(mix_ref.dtype)

    if prompt:
        @pl.when(pl.program_id(1) == pl.num_programs(1) - 1)
        def _():
            sret_out[0] = sret_s[...]
            sgdn_out[0] = sgdn_s[...]


def _post_kernel(mix_ref, x_ref, wout_ref, postw_ref, premlpw_ref, wup_ref, wdown_ref, postmlpw_ref, out_ref):
    m = _dot(mix_ref[...], wout_ref[...])
    x1 = x_ref[...] + _rms(m, postw_ref[...])
    h2 = _bf(_rms(x1, premlpw_ref[...]))
    acc = None
    for c in range(D_FF // D_MODEL):
        f = _dot(h2, wup_ref[:, c * D_MODEL:(c + 1) * D_MODEL])
        f = jnp.square(jnp.maximum(f, 0.0))
        part = _dot(_bf(f), wdown_ref[c * D_MODEL:(c + 1) * D_MODEL, :])
        acc = part if acc is None else acc + part
    out_ref[...] = x1 + _rms(acc, postmlpw_ref[...])


def _const_spec(shape, single=False):
    index_map = lambda *_: (0,) * len(shape)
    if single:
        return pl.BlockSpec(shape, index_map, pipeline_mode=pl.Buffered(1))
    return pl.BlockSpec(shape, index_map)


def _retention_tables(group):
    log_gamma = np.log(1.0 - 2.0 ** (-5.0 - np.arange(HEADS, dtype=np.float64)))
    t = np.arange(ROWS) % group
    same = (np.arange(ROWS)[:, None] // group) == (np.arange(ROWS)[None, :] // group)
    causal = same & (t[:, None] >= t[None, :])
    dmat = np.where(causal[None], np.exp((t[:, None] - t[None, :])[None] * log_gamma[:, None, None]), 0.0)
    ecum = np.exp((t[:, None] + 1.0) * log_gamma[None, :])
    tail = np.exp((group - 1.0 - t[:, None]) * log_gamma[None, :])
    rtab = np.stack([np.repeat(ecum, HEAD_DIM, axis=1), np.repeat(tail, HEAD_DIM, axis=1)])
    dtot = tuple(float(np.float32(v)) for v in np.exp(group * log_gamma))
    return jnp.asarray(dmat, F32), jnp.asarray(rtab, F32), dtot


def _rope_tables(length, offset):
    inv_freq = ROPE_BASE ** (-jnp.arange(0, HEAD_DIM, 2, dtype=F32) / HEAD_DIM)
    pos = jnp.arange(length, dtype=F32) + jnp.asarray(offset, F32)
    ang = pos[:, None] * inv_freq[None, :]
    cos, sin = jnp.cos(ang), jnp.sin(ang)
    return jnp.concatenate([cos, cos], axis=-1), jnp.concatenate([-sin, sin], axis=-1)


def _mixer(x, params, *, prompt, states=None):
    b, l, _ = x.shape
    t = b * l
    x2 = x.reshape(t, D_MODEL)
    if prompt:
        group, lb = min(ROWS, l), min(256, l)
        assert l % lb == 0 and lb % ROWS == 0 and l >= 8
        cos, sin = _rope_tables(l, 0)
        grid = (b, l // lb)
        row_map = lambda i, jj: (i * (l // lb) + jj, 0)
        pos_map = lambda i, jj: (jj, 0)
        dims = ("arbitrary", "arbitrary")
    else:
        group = l
        lb = min(2 * ROWS, t)
        assert ROWS % group == 0 and group == 8 and t % lb == 0 and lb % ROWS == 0
        cos, sin = _rope_tables(l, PAST_LEN)
        cos, sin = jnp.tile(cos, (lb // l, 1)), jnp.tile(sin, (lb // l, 1))
        grid = (t // lb,)
        row_map = lambda i: (i, 0)
        pos_map = lambda i: (0, 0)
        dims = ("arbitrary",)
    nseq = lb // group
    dmat, rtab, dtot = _retention_tables(group)
    cfg = dict(lb=lb, group=group, prompt=prompt, ret_dtot=dtot)

    in_specs = [
        pl.BlockSpec((lb, D_MODEL), row_map),
        _const_spec((1, D_MODEL)),
        _const_spec((D_MODEL, IN_W), single=True),
        _const_spec((D_MODEL, LANES), single=True),
        pl.BlockSpec((lb, HEAD_DIM), pos_map),
        pl.BlockSpec((lb, HEAD_DIM), pos_map),
        _const_spec((HEADS, ROWS, ROWS)),
        _const_spec((2, ROWS, GROUP_W)),
        _const_spec((CONV_W, CONV_CH)),
        _const_spec((8, LANES)),
        _const_spec((1, 2 * GROUP_W)),
    ]
    args = [x2, params["pre_w"], params["w_in"], params["w_ab"], cos, sin, dmat, rtab,
            params["conv_w"], params["pvec"], params["normw"]]
    state_shape = (HEADS, HEAD_DIM, HEAD_DIM)
    if prompt:
        out_specs = [
            pl.BlockSpec((lb, 2 * GROUP_W), row_map),
            pl.BlockSpec((1,) + state_shape, lambda i, jj: (i, 0, 0, 0)),
            pl.BlockSpec((1,) + state_shape, lambda i, jj: (i, 0, 0, 0)),
            pl.BlockSpec((8, CONV_CH), lambda i, jj: (i, 0)),
        ]
        out_shape = [
            jax.ShapeDtypeStruct((t, 2 * GROUP_W), BF16),
            jax.ShapeDtypeStruct((b,) + state_shape, F32),
            jax.ShapeDtypeStruct((b,) + state_shape, F32),
            jax.ShapeDtypeStruct((b * 8, CONV_CH), F32),
        ]
    else:
        s_ret, s_gdn, s_conv = states
        st8 = jnp.pad(s_conv.astype(F32), ((0, 0), (8 - (CONV_W - 1), 0), (0, 0))).reshape(t, CONV_CH)
        in_specs += [
            pl.BlockSpec((nseq,) + state_shape, lambda i: (i, 0, 0, 0)),
            pl.BlockSpec((nseq,) + state_shape, lambda i: (i, 0, 0, 0)),
            pl.BlockSpec((lb, CONV_CH), row_map),
        ]
        args += [s_ret.astype(F32), s_gdn.astype(F32), st8]
        out_specs = [
            pl.BlockSpec((lb, 2 * GROUP_W), row_map),
            pl.BlockSpec((nseq,) + state_shape, lambda i: (i, 0, 0, 0)),
            pl.BlockSpec((nseq,) + state_shape, lambda i: (i, 0, 0, 0)),
            pl.BlockSpec((lb, CONV_CH), row_map),
        ]
        out_shape = [
            jax.ShapeDtypeStruct((t, 2 * GROUP_W), BF16),
            jax.ShapeDtypeStruct((b,) + state_shape, F32),
            jax.ShapeDtypeStruct((b,) + state_shape, F32),
            jax.ShapeDtypeStruct((t, CONV_CH), F32),
        ]
    nmat = lb // ROWS * HEADS
    scratch = [
        pltpu.VMEM((lb, MAIN_W), F32),
        pltpu.VMEM((lb, GROUP_W), F32), pltpu.VMEM((lb, GROUP_W), F32),
        pltpu.VMEM((lb, GROUP_W), F32), pltpu.VMEM((lb, GROUP_W), F32), pltpu.VMEM((lb, GROUP_W), F32),
        pltpu.VMEM((lb, LANES), F32), pltpu.VMEM((lb, LANES), F32), pltpu.VMEM((lb, LANES), F32),
        pltpu.VMEM((lb, 2 * GROUP_W), F32),
        pltpu.VMEM((lb, GROUP_W), F32), pltpu.VMEM((lb, GROUP_W), F32),
        pltpu.VMEM((nmat, ROWS, ROWS), F32), pltpu.VMEM((nmat, ROWS, ROWS), F32),
    ]
    if prompt:
        scratch += [pltpu.VMEM(state_shape, F32), pltpu.VMEM(state_shape, F32), pltpu.VMEM((8, CONV_CH), F32)]

    mix, s_ret_new, s_gdn_new, conv_rows = pl.pallas_call(
        functools.partial(_mixer_kernel, cfg),
        grid=grid,
        in_specs=in_specs,
        out_specs=out_specs,
        out_shape=out_shape,
        scratch_shapes=scratch,
        compiler_params=pltpu.CompilerParams(dimension_semantics=dims, vmem_limit_bytes=VMEM_LIMIT),
        name="mixer_prompt" if prompt else "mixer_sample",
    )(*args)
    new_conv = conv_rows.reshape(b, 8, CONV_CH)[:, 8 - (CONV_W - 1):]
    return mix, s_ret_new, s_gdn_new, new_conv


def _post(mix, x, params):
    b, l, _ = x.shape
    t = b * l
    tm = min(512, t)
    assert t % tm == 0
    rows = pl.BlockSpec((tm, D_MODEL), lambda i: (i, 0))
    y = pl.pallas_call(
        _post_kernel,
        grid=(t // tm,),
        in_specs=[rows, rows, _const_spec((2 * GROUP_W, D_MODEL)), _const_spec((1, D_MODEL)),
                  _const_spec((1, D_MODEL)), _const_spec((D_MODEL, D_FF)), _const_spec((D_FF, D_MODEL)),
                  _const_spec((1, D_MODEL))],
        out_specs=rows,
        out_shape=jax.ShapeDtypeStruct((t, D_MODEL), F32),
        compiler_params=pltpu.CompilerParams(dimension_semantics=("arbitrary",), vmem_limit_bytes=VMEM_LIMIT),
        name="post",
    )(mix, x.reshape(t, D_MODEL), params["w_out"], params["post_w"], params["pre_mlp_w"],
      params["w_up"], params["w_down"], params["post_mlp_w"])
    return y.reshape(b, l, D_MODEL)


def _layer_params(l, pre_mix_w, w_in, conv_w, A_log, dt_bias, ret_norm_w, gdn_norm_w, w_out, post_mix_w,
                  pre_mlp_w, w_up, w_down, post_mlp_w):
    w = _bf(w_in[l])
    pvec = jnp.zeros((8, LANES), F32)
    pvec = pvec.at[0, :HEADS].set(A_log[l].astype(F32)).at[1, :HEADS].set(dt_bias[l].astype(F32))
    return dict(
        pre_w=pre_mix_w[l].astype(F32)[None],
        w_in=w,
        w_ab=jnp.pad(w[:, MAIN_W:], ((0, 0), (0, LANES - 2 * HEADS))),
        conv_w=conv_w[l].astype(F32),
        pvec=pvec,
        normw=jnp.concatenate([jnp.tile(ret_norm_w[l], HEADS), jnp.tile(gdn_norm_w[l], HEADS)]).astype(F32)[None],
        w_out=_bf(w_out[l]),
        post_w=post_mix_w[l].astype(F32)[None],
        pre_mlp_w=pre_mlp_w[l].astype(F32)[None],
        w_up=_bf(w_up[l]),
        w_down=_bf(w_down[l]),
        post_mlp_w=post_mlp_w[l].astype(F32)[None],
    )


def kernel(x_prompt, x_sample, state_ret, state_gdn, state_conv, pre_mix_w, w_in, conv_w, A_log, dt_bias,
           ret_norm_w, gdn_norm_w, w_out, post_mix_w, pre_mlp_w, w_up, w_down, post_mlp_w):
    depth = w_in.shape[0]
    yp, ys = x_prompt, x_sample
    outs = [[] for _ in range(6)]
    for l in range(depth):
        params = _layer_params(l, pre_mix_w, w_in, conv_w, A_log, dt_bias, ret_norm_w, gdn_norm_w, w_out,
                               post_mix_w, pre_mlp_w, w_up, w_down, post_mlp_w)
        mix_p, rp, gp, cp = _mixer(yp, params, prompt=True)
        yp = _post(mix_p, yp, params)
        mix_s, rs, gs, cs = _mixer(ys, params, prompt=False,
                                   states=(state_ret[l], state_gdn[l], state_conv[l]))
        ys = _post(mix_s, ys, params)
        for dst, val, like in zip(outs, (rp, gp, cp, rs, gs, cs),
                                  (state_ret, state_gdn, state_conv) * 2):
            dst.append(val.astype(like.dtype))
    return (yp, ys) + tuple(jnp.stack(o) for o in outs)
```

```python
import functools

import numpy as np
import jax
import jax.numpy as jnp
from jax import lax
from jax.experimental import pallas as pl
from jax.experimental.pallas import tpu as pltpu

D_MODEL = 1024
HEADS = 4
HEAD_DIM = 128
GROUP_W = HEADS * HEAD_DIM
CONV_CH = 3 * GROUP_W
CONV_W = 4
D_FF = 4 * D_MODEL
MAIN_W = 4 * GROUP_W + CONV_CH + GROUP_W
IN_W = MAIN_W + 2 * HEADS
ROPE_BASE = 10000.0
EPS = 1e-6
PAST_LEN = 16384

ROWS = 64
BASE = 8
LANES = 128
VMEM_LIMIT = 58 * 1024 * 1024

OFF_RQ, OFF_RK, OFF_RV, OFF_RG = 0, GROUP_W, 2 * GROUP_W, 3 * GROUP_W
OFF_CONV = 4 * GROUP_W
OFF_Z = OFF_CONV + CONV_CH

F32 = jnp.float32
BF16 = jnp.bfloat16


def _bf(x):
    return x.astype(BF16)


def _dot(a, b):
    return jnp.dot(a, b, preferred_element_type=F32)


def _dot_nt(a, b):
    return lax.dot_general(a, b, (((1,), (1,)), ((), ())), preferred_element_type=F32)


def _split(x, n):
    parts, r = [], x
    for i in range(n):
        p = r.astype(BF16)
        parts.append(p)
        if i + 1 < n:
            r = r - p.astype(F32)
    return parts


def _mm(a, b, nt=False):
    return _dot_nt(_bf(a), _bf(b)) if nt else _dot(_bf(a), _bf(b))


def _mm3(a, b):
    (a_hi, a_lo), (b_hi, b_lo) = _split(a, 2), _split(b, 2)
    return _dot(a_hi, b_hi) + _dot(a_hi, b_lo) + _dot(a_lo, b_hi)


def _rms(x, w):
    return x * lax.rsqrt(jnp.mean(x * x, axis=-1, keepdims=True) + EPS) * w


def _silu(x):
    return x / (1.0 + jnp.exp(-x))


def _tri_inverse_many(mats, group, row, col):
    eye = (row == col).astype(F32)
    blk = (row // BASE) == (col // BASE)
    n8 = [jnp.where(blk, -a, 0.0) for a in mats]
    p2 = [_mm3(n, n) for n in n8]
    np2 = [_mm3(n, p) for n, p in zip(n8, p2)]
    p4 = [_mm3(p, p) for p in p2]
    e = [n + p + x for n, p, x in zip(n8, p2, np2)]
    ep4 = [_mm3(x, p) for x, p in zip(e, p4)]
    t = [x + p + y + eye for x, p, y in zip(e, p4, ep4)]
    s = BASE
    while s < group:
        off = ((row // (2 * s)) == (col // (2 * s))) & ((row // s) != (col // s))
        a_off = [jnp.where(off, a, 0.0) for a in mats]
        ta = [_mm3(x, y) for x, y in zip(t, a_off)]
        tat = [_mm3(x, y) for x, y in zip(ta, t)]
        t = [x - y for x, y in zip(t, tat)]
        s *= 2
    return t, eye


def _mixer_kernel(cfg, *refs):
    lb, group, prompt = cfg["lb"], cfg["group"], cfg["prompt"]
    ngroups = ROWS // group
    chunks = list(range(lb // ROWS))
    it = iter(refs)
    x_ref, prew_ref, win_ref, wab_ref, cos_ref, sin_ref = (next(it) for _ in range(6))
    dmat_ref, rtab_ref, convw_ref, pvec_ref, normw_ref = (next(it) for _ in range(5))
    if not prompt:
        sret_in, sgdn_in, st8_ref = (next(it) for _ in range(3))
    mix_ref, sret_out, sgdn_out, conv_out = (next(it) for _ in range(4))
    proj_s, qr_s, kr_s, qg_s, kg_s, vg_s, cum_s, tot_s, beta_s, o_s = (next(it) for _ in range(10))
    w_s, u_s, pr_s, pg_s = (next(it) for _ in range(4))
    if prompt:
        sret_s, sgdn_s, prev8_s = (next(it) for _ in range(3))

        @pl.when(pl.program_id(1) == 0)
        def _():
            sret_s[...] = jnp.zeros_like(sret_s)
            sgdn_s[...] = jnp.zeros_like(sgdn_s)
            prev8_s[...] = jnp.zeros_like(prev8_s)

    rows_of = lambda c: slice(c * ROWS, (c + 1) * ROWS)
    head_sl = lambda hh: slice(hh * HEAD_DIM, (hh + 1) * HEAD_DIM)
    lane = lambda x, n: x[:, n:n + 1]
    cat = lambda xs: xs[0] if len(xs) == 1 else jnp.concatenate(xs, axis=0)

    h = _bf(_rms(x_ref[...], prew_ref[...]))
    proj_s[...] = _dot(h, win_ref[:, :MAIN_W])
    ab = _dot(h, wab_ref[...])

    cos, sin = cos_ref[...], sin_ref[...]
    for hh in range(HEADS):
        sl = head_sl(hh)
        q = proj_s[:, OFF_RQ + hh * HEAD_DIM:OFF_RQ + (hh + 1) * HEAD_DIM]
        qr_s[:, sl] = q * cos + pltpu.roll(q, HEAD_DIM // 2, 1) * sin
        k = proj_s[:, OFF_RK + hh * HEAD_DIM:OFF_RK + (hh + 1) * HEAD_DIM]
        kr_s[:, sl] = (k * cos + pltpu.roll(k, HEAD_DIM // 2, 1) * sin) * (HEAD_DIM ** -0.5)

    t_in_group = lax.broadcasted_iota(jnp.int32, (lb, 1), 0) % group
    for part, dst in enumerate((qg_s, kg_s, vg_s)):
        cols = slice(OFF_CONV + part * GROUP_W, OFF_CONV + (part + 1) * GROUP_W)
        xq = proj_s[:, cols]
        cw = convw_ref[:, part * GROUP_W:(part + 1) * GROUP_W]
        acc = xq * cw[CONV_W - 1:CONV_W]
        if prompt:
            ext = jnp.concatenate([prev8_s[:, part * GROUP_W:(part + 1) * GROUP_W], xq], axis=0)
            for s in range(1, CONV_W):
                sh = pltpu.roll(ext, s, 0)[8:]
                acc = acc + sh * cw[CONV_W - 1 - s:CONV_W - s]
        else:
            st8 = st8_ref[:, part * GROUP_W:(part + 1) * GROUP_W]
            for s in range(1, CONV_W):
                sh = jnp.where(t_in_group < s, pltpu.roll(st8, lb - 8 + s, 0), pltpu.roll(xq, s, 0))
                acc = acc + sh * cw[CONV_W - 1 - s:CONV_W - s]
        conv = _silu(acc)
        for hh in range(HEADS):
            sl = head_sl(hh)
            c = conv[:, sl]
            if part < 2:
                c = c * lax.rsqrt(jnp.sum(c * c, axis=-1, keepdims=True) + EPS)
            if part == 0:
                c = c * (HEAD_DIM ** -0.5)
            dst[:, sl] = c
    if prompt:
        prev8_s[...] = proj_s[lb - 8:lb, OFF_CONV:OFF_CONV + CONV_CH]
        conv_out[...] = proj_s[lb - 8:lb, OFF_CONV:OFF_CONV + CONV_CH]
    else:
        conv_out[...] = proj_s[:, OFF_CONV:OFF_CONV + CONV_CH]

    pv = pvec_ref[...]
    a_plus = ab + pv[1:2]
    softplus = jnp.maximum(a_plus, 0.0) + jnp.log1p(jnp.exp(-jnp.abs(a_plus)))
    g = -jnp.exp(pv[0:1]) * softplus
    beta_s[...] = 1.0 / (1.0 + jnp.exp(-ab))

    row = lax.broadcasted_iota(jnp.int32, (ROWS, ROWS), 0)
    col = lax.broadcasted_iota(jnp.int32, (ROWS, ROWS), 1)
    same = (row // group) == (col // group)
    causal = same & (row >= col)
    strict = same & (row > col)
    sum_mat = _bf(jnp.concatenate([causal.astype(F32), same.astype(F32)], axis=0))
    sum_mat3 = jnp.concatenate([sum_mat] * 3, axis=1)
    for c in chunks:
        r = _dot(sum_mat3, jnp.concatenate(_split(g[rows_of(c)], 3), axis=0))
        cum_s[rows_of(c), :] = r[:ROWS]
        tot_s[rows_of(c), :] = r[ROWS:]
    sel = _bf((lax.broadcasted_iota(jnp.int32, (8, LANES), 0)
               == lax.broadcasted_iota(jnp.int32, (8, LANES), 1)).astype(F32))
    sel3 = jnp.concatenate([sel] * 3, axis=1)

    items = [(c, hh) for c in chunks for hh in range(HEADS)]
    cumc = [cum_s[rows_of(c), :] for c in chunks]
    totc = [tot_s[rows_of(c), :] for c in chunks]
    betac = [beta_s[rows_of(c), :] for c in chunks]
    ecumc = [jnp.exp(x) for x in cumc]
    tailc = [jnp.exp(t - x) for t, x in zip(totc, cumc)]
    cumrow = [_dot_nt(sel3, jnp.concatenate(_split(x, 3), axis=1)) for x in cumc]

    for c, hh in items:
        rows, sl = rows_of(c), head_sl(hh)
        q, k = qr_s[rows, sl], kr_s[rows, sl]
        pr_s[c * HEADS + hh] = _mm(q, k, nt=True) * dmat_ref[hh]
        qr_s[rows, sl] = q * rtab_ref[0, :, sl]
        kr_s[rows, sl] = k * rtab_ref[1, :, sl]

    qk_kk, dmats = [], []
    for c, hh in items:
        rows, sl = rows_of(c), head_sl(hh)
        kb = _bf(kg_s[rows, sl])
        qk_kk.append(_dot_nt(jnp.concatenate([_bf(qg_s[rows, sl]), kb], axis=0), kb))
        diff = lane(cumc[c], hh) - cumrow[c][hh:hh + 1, :]
        dmats.append(jnp.where(causal, jnp.exp(jnp.minimum(diff, 0.0)), 0.0))
    a_mats = [jnp.where(strict, lane(betac[c], HEADS + hh) * x[ROWS:] * d, 0.0)
              for (c, hh), x, d in zip(items, qk_kk, dmats)]
    t_inv, eye = _tri_inverse_many(a_mats, group, row, col)
    for i, (c, hh) in enumerate(items):
        rows, sl = rows_of(c), head_sl(hh)
        q, k, v = qg_s[rows, sl], kg_s[rows, sl], vg_s[rows, sl]
        beta_h = lane(betac[c], HEADS + hh)
        rhs = jnp.concatenate([k * (beta_h * lane(ecumc[c], hh)), v * beta_h], axis=1)
        sol = rhs + _mm(t_inv[i] - eye, rhs)
        w_s[rows, sl] = sol[:, :HEAD_DIM]
        u_s[rows, sl] = sol[:, HEAD_DIM:]
        pg_s[c * HEADS + hh] = qk_kk[i][:ROWS] * dmats[i]
        qg_s[rows, sl] = q * lane(ecumc[c], hh)
        kg_s[rows, sl] = k * lane(tailc[c], hh)

    def state_refs(c, hh, gi):
        if prompt:
            return (sret_s.at[hh], sret_s.at[hh], sgdn_s.at[hh], sgdn_s.at[hh])
        n = c * ngroups + gi
        return (sret_in.at[n, hh], sret_out.at[n, hh], sgdn_in.at[n, hh], sgdn_out.at[n, hh])

    for c in chunks:
        dtotc = jnp.exp(totc[c])
        qs_r, qs_g, ws_g = {}, {}, {}
        for hh in range(HEADS):
            sl = head_sl(hh)
            for gi in range(ngroups):
                rs = slice(c * ROWS + gi * group, c * ROWS + (gi + 1) * group)
                r_in, _, g_in, _ = state_refs(c, hh, gi)
                qs_r[hh, gi] = _mm(qr_s[rs, sl], r_in[...])
                r = _mm(jnp.concatenate([qg_s[rs, sl], w_s[rs, sl]], axis=0), g_in[...])
                qs_g[hh, gi], ws_g[hh, gi] = r[:group], r[group:]
        for hh in range(HEADS):
            rows, sl = rows_of(c), head_sl(hh)
            i = c * HEADS + hh
            vb = _bf(proj_s[rows, OFF_RV + hh * HEAD_DIM:OFF_RV + (hh + 1) * HEAD_DIM])
            v_new = u_s[rows, sl] - cat([ws_g[hh, gi] for gi in range(ngroups)])
            vnb = _bf(v_new)
            o_s[rows, sl] = cat([qs_r[hh, gi] for gi in range(ngroups)]) + _mm(pr_s[i], vb)
            o_s[rows, GROUP_W + hh * HEAD_DIM:GROUP_W + (hh + 1) * HEAD_DIM] = (
                cat([qs_g[hh, gi] for gi in range(ngroups)]) + _mm(pg_s[i], vnb))
            for gi in range(ngroups):
                rs = slice(c * ROWS + gi * group, c * ROWS + (gi + 1) * group)
                ls = slice(gi * group, (gi + 1) * group)
                r_in, r_out, g_in, g_out = state_refs(c, hh, gi)
                r_out[...] = cfg["ret_dtot"][hh] * r_in[...] + _mm(kr_s[rs, sl].T, vb[ls])
                dec = dtotc[gi * group:gi * group + 1, hh:hh + 1]
                g_out[...] = dec * g_in[...] + _mm(kg_s[rs, sl].T, vnb[ls])

    for hh in range(2 * HEADS):
        sl = head_sl(hh)
        goff = OFF_RG + hh * HEAD_DIM if hh < HEADS else OFF_Z + (hh - HEADS) * HEAD_DIM
        y = _rms(o_s[:, sl], normw_ref[:, sl]) * _silu(proj_s[:, goff:goff + HEAD_DIM])
        mix_ref[:, sl] = y.astype(mix_ref.dtype)

    if prompt:
        @pl.when(pl.program_id(1) == pl.num_programs(1) - 1)
        def _():
            sret_out[0] = sret_s[...]
            sgdn_out[0] = sgdn_s[...]


def _post_kernel(mix_ref, x_ref, wout_ref, postw_ref, premlpw_ref, wup_ref, wdown_ref, postmlpw_ref, out_ref):
    m = _dot(mix_ref[...], wout_ref[...])
    x1 = x_ref[...] + _rms(m, postw_ref[...])
    h2 = _bf(_rms(x1, premlpw_ref[...]))
    acc = None
    for c in range(D_FF // D_MODEL):
        f = _dot(h2, wup_ref[:, c * D_MODEL:(c + 1) * D_MODEL])
        f = jnp.square(jnp.maximum(f, 0.0))
        part = _dot(_bf(f), wdown_ref[c * D_MODEL:(c + 1) * D_MODEL, :])
        acc = part if acc is None else acc + part
    out_ref[...] = x1 + _rms(acc, postmlpw_ref[...])


def _const_spec(shape, single=False):
    index_map = lambda *_: (0,) * len(shape)
    if single:
        return pl.BlockSpec(shape, index_map, pipeline_mode=pl.Buffered(1))
    return pl.BlockSpec(shape, index_map)


def _retention_tables(group):
    log_gamma = np.log(1.0 - 2.0 ** (-5.0 - np.arange(HEADS, dtype=np.float64)))
    t = np.arange(ROWS) % group
    same = (np.arange(ROWS)[:, None] // group) == (np.arange(ROWS)[None, :] // group)
    causal = same & (t[:, None] >= t[None, :])
    dmat = np.where(causal[None], np.exp((t[:, None] - t[None, :])[None] * log_gamma[:, None, None]), 0.0)
    ecum = np.exp((t[:, None] + 1.0) * log_gamma[None, :])
    tail = np.exp((group - 1.0 - t[:, None]) * log_gamma[None, :])
    rtab = np.stack([np.repeat(ecum, HEAD_DIM, axis=1), np.repeat(tail, HEAD_DIM, axis=1)])
    dtot = tuple(float(np.float32(v)) for v in np.exp(group * log_gamma))
    return jnp.asarray(dmat, F32), jnp.asarray(rtab, F32), dtot


def _rope_tables(length, offset):
    inv_freq = ROPE_BASE ** (-jnp.arange(0, HEAD_DIM, 2, dtype=F32) / HEAD_DIM)
    pos = jnp.arange(length, dtype=F32) + jnp.asarray(offset, F32)
    ang = pos[:, None] * inv_freq[None, :]
    cos, sin = jnp.cos(ang), jnp.sin(ang)
    return jnp.concatenate([cos, cos], axis=-1), jnp.concatenate([-sin, sin], axis=-1)


def _mixer(x, params, *, prompt, states=None):
    b, l, _ = x.shape
    t = b * l
    x2 = x.reshape(t, D_MODEL)
    if prompt:
        group, lb = min(ROWS, l), min(256, l)
        assert l % lb == 0 and lb % ROWS == 0 and l >= 8
        cos, sin = _rope_tables(l, 0)
        grid = (b, l // lb)
        row_map = lambda i, jj: (i * (l // lb) + jj, 0)
        pos_map = lambda i, jj: (jj, 0)
        dims = ("arbitrary", "arbitrary")
    else:
        group = l
        lb = min(2 * ROWS, t)
        assert ROWS % group == 0 and group == 8 and t % lb == 0 and lb % ROWS == 0
        cos, sin = _rope_tables(l, PAST_LEN)
        cos, sin = jnp.tile(cos, (lb // l, 1)), jnp.tile(sin, (lb // l, 1))
        grid = (t // lb,)
        row_map = lambda i: (i, 0)
        pos_map = lambda i: (0, 0)
        dims = ("arbitrary",)
    nseq = lb // group
    dmat, rtab, dtot = _retention_tables(group)
    cfg = dict(lb=lb, group=group, prompt=prompt, ret_dtot=dtot)

    in_specs = [
        pl.BlockSpec((lb, D_MODEL), row_map),
        _const_spec((1, D_MODEL)),
        _const_spec((D_MODEL, IN_W), single=True),
        _const_spec((D_MODEL, LANES), single=True),
        pl.BlockSpec((lb, HEAD_DIM), pos_map),
        pl.BlockSpec((lb, HEAD_DIM), pos_map),
        _const_spec((HEADS, ROWS, ROWS)),
        _const_spec((2, ROWS, GROUP_W)),
        _const_spec((CONV_W, CONV_CH)),
        _const_spec((8, LANES)),
        _const_spec((1, 2 * GROUP_W)),
    ]
    args = [x2, params["pre_w"], params["w_in"], params["w_ab"], cos, sin, dmat, rtab,
            params["conv_w"], params["pvec"], params["normw"]]
    state_shape = (HEADS, HEAD_DIM, HEAD_DIM)
    if prompt:
        out_specs = [
            pl.BlockSpec((lb, 2 * GROUP_W), row_map),
            pl.BlockSpec((1,) + state_shape, lambda i, jj: (i, 0, 0, 0)),
            pl.BlockSpec((1,) + state_shape, lambda i, jj: (i, 0, 0, 0)),
            pl.BlockSpec((8, CONV_CH), lambda i, jj: (i, 0)),
        ]
        out_shape = [
            jax.ShapeDtypeStruct((t, 2 * GROUP_W), BF16),
            jax.ShapeDtypeStruct((b,) + state_shape, F32),
            jax.ShapeDtypeStruct((b,) + state_shape, F32),
            jax.ShapeDtypeStruct((b * 8, CONV_CH), F32),
        ]
    else:
        s_ret, s_gdn, s_conv = states
        st8 = jnp.pad(s_conv.astype(F32), ((0, 0), (8 - (CONV_W - 1), 0), (0, 0))).reshape(t, CONV_CH)
        in_specs += [
            pl.BlockSpec((nseq,) + state_shape, lambda i: (i, 0, 0, 0)),
            pl.BlockSpec((nseq,) + state_shape, lambda i: (i, 0, 0, 0)),
            pl.BlockSpec((lb, CONV_CH), row_map),
        ]
        args += [s_ret.astype(F32), s_gdn.astype(F32), st8]
        out_specs = [
            pl.BlockSpec((lb, 2 * GROUP_W), row_map),
            pl.BlockSpec((nseq,) + state_shape, lambda i: (i, 0, 0, 0)),
            pl.BlockSpec((nseq,) + state_shape, lambda i: (i, 0, 0, 0)),
            pl.BlockSpec((lb, CONV_CH), row_map),
        ]
        out_shape = [
            jax.ShapeDtypeStruct((t, 2 * GROUP_W), BF16),
            jax.ShapeDtypeStruct((b,) + state_shape, F32),
            jax.ShapeDtypeStruct((b,) + state_shape, F32),
            jax.ShapeDtypeStruct((t, CONV_CH), F32),
        ]
    nmat = lb // ROWS * HEADS
    scratch = [
        pltpu.VMEM((lb, MAIN_W), F32),
        pltpu.VMEM((lb, GROUP_W), F32), pltpu.VMEM((lb, GROUP_W), F32),
        pltpu.VMEM((lb, GROUP_W), F32), pltpu.VMEM((lb, GROUP_W), F32), pltpu.VMEM((lb, GROUP_W), F32),
        pltpu.VMEM((lb, LANES), F32), pltpu.VMEM((lb, LANES), F32), pltpu.VMEM((lb, LANES), F32),
        pltpu.VMEM((lb, 2 * GROUP_W), F32),
        pltpu.VMEM((lb, GROUP_W), F32), pltpu.VMEM((lb, GROUP_W), F32),
        pltpu.VMEM((nmat, ROWS, ROWS), F32), pltpu.VMEM((nmat, ROWS, ROWS), F32),
    ]
    if prompt:
        scratch += [pltpu.VMEM(state_shape, F32), pltpu.VMEM(state_shape, F32), pltpu.VMEM((8, CONV_CH), F32)]

    mix, s_ret_new, s_gdn_new, conv_rows = pl.pallas_call(
        functools.partial(_mixer_kernel, cfg),
        grid=grid,
        in_specs=in_specs,
        out_specs=out_specs,
        out_shape=out_shape,
        scratch_shapes=scratch,
        compiler_params=pltpu.CompilerParams(dimension_semantics=dims, vmem_limit_bytes=VMEM_LIMIT),
        name="mixer_prompt" if prompt else "mixer_sample",
    )(*args)
    new_conv = conv_rows.reshape(b, 8, CONV_CH)[:, 8 - (CONV_W - 1):]
    return mix, s_ret_new, s_gdn_new, new_conv


def _post(mix, x, params):
    b, l, _ = x.shape
    t = b * l
    tm = min(512, t)
    assert t % tm == 0
    rows = pl.BlockSpec((tm, D_MODEL), lambda i: (i, 0))
    y = pl.pallas_call(
        _post_kernel,
        grid=(t // tm,),
        in_specs=[rows, rows, _const_spec((2 * GROUP_W, D_MODEL)), _const_spec((1, D_MODEL)),
                  _const_spec((1, D_MODEL)), _const_spec((D_MODEL, D_FF)), _const_spec((D_FF, D_MODEL)),
                  _const_spec((1, D_MODEL))],
        out_specs=rows,
        out_shape=jax.ShapeDtypeStruct((t, D_MODEL), F32),
        compiler_params=pltpu.CompilerParams(dimension_semantics=("arbitrary",), vmem_limit_bytes=VMEM_LIMIT),
        name="post",
    )(mix, x.reshape(t, D_MODEL), params["w_out"], params["post_w"], params["pre_mlp_w"],
      params["w_up"], params["w_down"], params["post_mlp_w"])
    return y.reshape(b, l, D_MODEL)


def _layer_params(l, pre_mix_w, w_in, conv_w, A_log, dt_bias, ret_norm_w, gdn_norm_w, w_out, post_mix_w,
                  pre_mlp_w, w_up, w_down, post_mlp_w):
    w = _bf(w_in[l])
    pvec = jnp.zeros((8, LANES), F32)
    pvec = pvec.at[0, :HEADS].set(A_log[l].astype(F32)).at[1, :HEADS].set(dt_bias[l].astype(F32))
    return dict(
        pre_w=pre_mix_w[l].astype(F32)[None],
        w_in=w,
        w_ab=jnp.pad(w[:, MAIN_W:], ((0, 0), (0, LANES - 2 * HEADS))),
        conv_w=conv_w[l].astype(F32),
        pvec=pvec,
        normw=jnp.concatenate([jnp.tile(ret_norm_w[l], HEADS), jnp.tile(gdn_norm_w[l], HEADS)]).astype(F32)[None],
        w_out=_bf(w_out[l]),
        post_w=post_mix_w[l].astype(F32)[None],
        pre_mlp_w=pre_mlp_w[l].astype(F32)[None],
        w_up=_bf(w_up[l]),
        w_down=_bf(w_down[l]),
        post_mlp_w=post_mlp_w[l].astype(F32)[None],
    )


def kernel(x_prompt, x_sample, state_ret, state_gdn, state_conv, pre_mix_w, w_in, conv_w, A_log, dt_bias,
           ret_norm_w, gdn_norm_w, w_out, post_mix_w, pre_mlp_w, w_up, w_down, post_mlp_w):
    depth = w_in.shape[0]
    yp, ys = x_prompt, x_sample
    outs = [[] for _ in range(6)]
    for l in range(depth):
        params = _layer_params(l, pre_mix_w, w_in, conv_w, A_log, dt_bias, ret_norm_w, gdn_norm_w, w_out,
                               post_mix_w, pre_mlp_w, w_up, w_down, post_mlp_w)
        mix_p, rp, gp, cp = _mixer(yp, params, prompt=True)
        yp = _post(mix_p, yp, params)
        mix_s, rs, gs, cs = _mixer(ys, params, prompt=False,
                                   states=(state_ret[l], state_gdn[l], state_conv[l]))
        ys = _post(mix_s, ys, params)
        for dst, val, like in zip(outs, (rp, gp, cp, rs, gs, cs),
                                  (state_ret, state_gdn, state_conv) * 2):
            dst.append(val.astype(like.dtype))
    return (yp, ys) + tuple(jnp.stack(o) for o in outs)
```

```python
import functools

import numpy as np
import jax
import jax.numpy as jnp
from jax import lax
from jax.experimental import pallas as pl
from jax.experimental.pallas import tpu as pltpu

D_MODEL = 1024
HEADS = 4
HEAD_DIM = 128
GROUP_W = HEADS * HEAD_DIM
CONV_CH = 3 * GROUP_W
CONV_W = 4
D_FF = 4 * D_MODEL
MAIN_W = 4 * GROUP_W + CONV_CH + GROUP_W
IN_W = MAIN_W + 2 * HEADS
ROPE_BASE = 10000.0
EPS = 1e-6
PAST_LEN = 16384

ROWS = 64
BASE = 8
LANES = 128
VMEM_LIMIT = 58 * 1024 * 1024

OFF_RQ, OFF_RK, OFF_RV, OFF_RG = 0, GROUP_W, 2 * GROUP_W, 3 * GROUP_W
OFF_CONV = 4 * GROUP_W
OFF_Z = OFF_CONV + CONV_CH

F32 = jnp.float32
BF16 = jnp.bfloat16


def _bf(x):
    return x.astype(BF16)


def _dot(a, b):
    return jnp.dot(a, b, preferred_element_type=F32)


def _dot_nt(a, b):
    return lax.dot_general(a, b, (((1,), (1,)), ((), ())), preferred_element_type=F32)


def _split(x, n):
    parts, r = [], x
    for i in range(n):
        p = r.astype(BF16)
        parts.append(p)
        if i + 1 < n:
            r = r - p.astype(F32)
    return parts


def _mm(a, b, nt=False):
    return _dot_nt(_bf(a), _bf(b)) if nt else _dot(_bf(a), _bf(b))


def _mm3(a, b):
    (a_hi, a_lo), (b_hi, b_lo) = _split(a, 2), _split(b, 2)
    return _dot(a_hi, b_hi) + _dot(a_hi, b_lo) + _dot(a_lo, b_hi)


def _rms(x, w):
    return x * lax.rsqrt(jnp.mean(x * x, axis=-1, keepdims=True) + EPS) * w


def _silu(x):
    return x / (1.0 + jnp.exp(-x))


def _tri_inverse_many(mats, group, row, col):
    eye = (row == col).astype(F32)
    blk = (row // BASE) == (col // BASE)
    n8 = [jnp.where(blk, -a, 0.0) for a in mats]
    p2 = [_mm3(n, n) for n in n8]
    np2 = [_mm3(n, p) for n, p in zip(n8, p2)]
    p4 = [_mm3(p, p) for p in p2]
    e = [n + p + x for n, p, x in zip(n8, p2, np2)]
    ep4 = [_mm3(x, p) for x, p in zip(e, p4)]
    t = [x + p + y + eye for x, p, y in zip(e, p4, ep4)]
    s = BASE
    while s < group:
        off = ((row // (2 * s)) == (col // (2 * s))) & ((row // s) != (col // s))
        a_off = [jnp.where(off, a, 0.0) for a in mats]
        ta = [_mm3(x, y) for x, y in zip(t, a_off)]
        tat = [_mm3(x, y) for x, y in zip(ta, t)]
        t = [x - y for x, y in zip(t, tat)]
        s *= 2
    return t, eye


def _mixer_kernel(cfg, *refs):
    lb, group, prompt = cfg["lb"], cfg["group"], cfg["prompt"]
    ngroups = ROWS // group
    chunks = list(range(lb // ROWS))
    it = iter(refs)
    x_ref, prew_ref, win_ref, wab_ref, cos_ref, sin_ref = (next(it) for _ in range(6))
    dmat_ref, rtab_ref, convw_ref, pvec_ref, normw_ref = (next(it) for _ in range(5))
    if not prompt:
        sret_in, sgdn_in, st8_ref = (next(it) for _ in range(3))
    mix_ref, sret_out, sgdn_out, conv_out = (next(it) for _ in range(4))
    proj_s, qr_s, kr_s, qg_s, kg_s, vg_s, cum_s, tot_s, beta_s, o_s = (next(it) for _ in range(10))
    w_s, u_s, pr_s, pg_s = (next(it) for _ in range(4))
    if prompt:
        sret_s, sgdn_s, prev8_s = (next(it) for _ in range(3))

        @pl.when(pl.program_id(1) == 0)
        def _():
            sret_s[...] = jnp.zeros_like(sret_s)
            sgdn_s[...] = jnp.zeros_like(sgdn_s)
            prev8_s[...] = jnp.zeros_like(prev8_s)

    rows_of = lambda c: slice(c * ROWS, (c + 1) * ROWS)
    head_sl = lambda hh: slice(hh * HEAD_DIM, (hh + 1) * HEAD_DIM)
    lane = lambda x, n: x[:, n:n + 1]
    cat = lambda xs: xs[0] if len(xs) == 1 else jnp.concatenate(xs, axis=0)

    h = _bf(_rms(x_ref[...], prew_ref[...]))
    proj_s[...] = _dot(h, win_ref[:, :MAIN_W])
    ab = _dot(h, wab_ref[...])

    cos, sin = cos_ref[...], sin_ref[...]
    for hh in range(HEADS):
        sl = head_sl(hh)
        q = proj_s[:, OFF_RQ + hh * HEAD_DIM:OFF_RQ + (hh + 1) * HEAD_DIM]
        qr_s[:, sl] = q * cos + pltpu.roll(q, HEAD_DIM // 2, 1) * sin
        k = proj_s[:, OFF_RK + hh * HEAD_DIM:OFF_RK + (hh + 1) * HEAD_DIM]
        kr_s[:, sl] = (k * cos + pltpu.roll(k, HEAD_DIM // 2, 1) * sin) * (HEAD_DIM ** -0.5)

    t_in_group = lax.broadcasted_iota(jnp.int32, (lb, 1), 0) % group
    for part, dst in enumerate((qg_s, kg_s, vg_s)):
        cols = slice(OFF_CONV + part * GROUP_W, OFF_CONV + (part + 1) * GROUP_W)
        xq = proj_s[:, cols]
        cw = convw_ref[:, part * GROUP_W:(part + 1) * GROUP_W]
        acc = xq * cw[CONV_W - 1:CONV_W]
        if prompt:
            ext = jnp.concatenate([prev8_s[:, part * GROUP_W:(part + 1) * GROUP_W], xq], axis=0)
            for s in range(1, CONV_W):
                sh = pltpu.roll(ext, s, 0)[8:]
                acc = acc + sh * cw[CONV_W - 1 - s:CONV_W - s]
        else:
            st8 = st8_ref[:, part * GROUP_W:(part + 1) * GROUP_W]
            for s in range(1, CONV_W):
                sh = jnp.where(t_in_group < s, pltpu.roll(st8, lb - 8 + s, 0), pltpu.roll(xq, s, 0))
                acc = acc + sh * cw[CONV_W - 1 - s:CONV_W - s]
        conv = _silu(acc)
        for hh in range(HEADS):
            sl = head_sl(hh)
            c = conv[:, sl]
            if part < 2:
                c = c * lax.rsqrt(jnp.sum(c * c, axis=-1, keepdims=True) + EPS)
            if part == 0:
                c = c * (HEAD_DIM ** -0.5)
            dst[:, sl] = c
    if prompt:
        prev8_s[...] = proj_s[lb - 8:lb, OFF_CONV:OFF_CONV + CONV_CH]
        conv_out[...] = proj_s[lb - 8:lb, OFF_CONV:OFF_CONV + CONV_CH]
    else:
        conv_out[...] = proj_s[:, OFF_CONV:OFF_CONV + CONV_CH]

    pv = pvec_ref[...]
    a_plus = ab + pv[1:2]
    softplus = jnp.maximum(a_plus, 0.0) + jnp.log1p(jnp.exp(-jnp.abs(a_plus)))
    g = -jnp.exp(pv[0:1]) * softplus
    beta_s[...] = 1.0 / (1.0 + jnp.exp(-ab))

    row = lax.broadcasted_iota(jnp.int32, (ROWS, ROWS), 0)
    col = lax.broadcasted_iota(jnp.int32, (ROWS, ROWS), 1)
    same = (row // group) == (col // group)
    causal = same & (row >= col)
    strict = same & (row > col)
    sum_mat = _bf(jnp.concatenate([causal.astype(F32), same.astype(F32)], axis=0))
    sum_mat3 = jnp.concatenate([sum_mat] * 3, axis=1)
    for c in chunks:
        r = _dot(sum_mat3, jnp.concatenate(_split(g[rows_of(c)], 3), axis=0))
        cum_s[rows_of(c), :] = r[:ROWS]
        tot_s[rows_of(c), :] = r[ROWS:]
    sel = _bf((lax.broadcasted_iota(jnp.int32, (8, LANES), 0)
               == lax.broadcasted_iota(jnp.int32, (8, LANES), 1)).astype(F32))
    sel3 = jnp.concatenate([sel] * 3, axis=1)

    items = [(c, hh) for c in chunks for hh in range(HEADS)]
    cumc = [cum_s[rows_of(c), :] for c in chunks]
    totc = [tot_s[rows_of(c), :] for c in chunks]
    betac = [beta_s[rows_of(c), :] for c in chunks]
    ecumc = [jnp.exp(x) for x in cumc]
    tailc = [jnp.exp(t - x) for t, x in zip(totc, cumc)]
    cumrow = [_dot_nt(sel3, jnp.concatenate(_split(x, 3), axis=1)) for x in cumc]

    for c, hh in items:
        rows, sl = rows_of(c), head_sl(hh)
        q, k = qr_s[rows, sl], kr_s[rows, sl]
        pr_s[c * HEADS + hh] = _mm(q, k, nt=True) * dmat_ref[hh]
        qr_s[rows, sl] = q * rtab_ref[0, :, sl]
        kr_s[rows, sl] = k * rtab_ref[1, :, sl]

    qk_kk, dmats = [], []
    for c, hh in items:
        rows, sl = rows_of(c), head_sl(hh)
        kb = _bf(kg_s[rows, sl])
        qk_kk.append(_dot_nt(jnp.concatenate([_bf(qg_s[rows, sl]), kb], axis=0), kb))
        diff = lane(cumc[c], hh) - cumrow[c][hh:hh + 1, :]
        dmats.append(jnp.where(causal, jnp.exp(jnp.minimum(diff, 0.0)), 0.0))
    a_mats = [jnp.where(strict, lane(betac[c], HEADS + hh) * x[ROWS:] * d, 0.0)
              for (c, hh), x, d in zip(items, qk_kk, dmats)]
    t_inv, eye = _tri_inverse_many(a_mats, group, row, col)
    for i, (c, hh) in enumerate(items):
        rows, sl = rows_of(c), head_sl(hh)
        q, k, v = qg_s[rows, sl], kg_s[rows, sl], vg_s[rows, sl]
        beta_h = lane(betac[c], HEADS + hh)
        rhs = jnp.concatenate([k * (beta_h * lane(ecumc[c], hh)), v * beta_h], axis=1)
        sol = rhs + _mm(t_inv[i] - eye, rhs)
        w_s[rows, sl] = sol[:, :HEAD_DIM]
        u_s[rows, sl] = sol[:, HEAD_DIM:]
        pg_s[c * HEADS + hh] = qk_kk[i][:ROWS] * dmats[i]
        qg_s[rows, sl] = q * lane(ecumc[c], hh)
        kg_s[rows, sl] = k * lane(tailc[c], hh)

    def state_refs(c, hh, gi):
        if prompt:
            return (sret_s.at[hh], sret_s.at[hh], sgdn_s.at[hh], sgdn_s.at[hh])
        n = c * ngroups + gi
        return (sret_in.at[n, hh], sret_out.at[n, hh], sgdn_in.at[n, hh], sgdn_out.at[n, hh])

    for c in chunks:
        dtotc = jnp.exp(totc[c])
        qs_r, qs_g, ws_g = {}, {}, {}
        for hh in range(HEADS):
            sl = head_sl(hh)
            for gi in range(ngroups):
                rs = slice(c * ROWS + gi * group, c * ROWS + (gi + 1) * group)
                r_in, _, g_in, _ = state_refs(c, hh, gi)
                qs_r[hh, gi] = _mm(qr_s[rs, sl], r_in[...])
                r = _mm(jnp.concatenate([qg_s[rs, sl], w_s[rs, sl]], axis=0), g_in[...])
                qs_g[hh, gi], ws_g[hh, gi] = r[:group], r[group:]
        for hh in range(HEADS):
            rows, sl = rows_of(c), head_sl(hh)
            i = c * HEADS + hh
            vb = _bf(proj_s[rows, OFF_RV + hh * HEAD_DIM:OFF_RV + (hh + 1) * HEAD_DIM])
            v_new = u_s[rows, sl] - cat([ws_g[hh, gi] for gi in range(ngroups)])
            vnb = _bf(v_new)
            o_s[rows, sl] = cat([qs_r[hh, gi] for gi in range(ngroups)]) + _mm(pr_s[i], vb)
            o_s[rows, GROUP_W + hh * HEAD_DIM:GROUP_W + (hh + 1) * HEAD_DIM] = (
                cat([qs_g[hh, gi] for gi in range(ngroups)]) + _mm(pg_s[i], vnb))
            for gi in range(ngroups):
                rs = slice(c * ROWS + gi * group, c * ROWS + (gi + 1) * group)
                ls = slice(gi * group, (gi + 1) * group)
                r_in, r_out, g_in, g_out = state_refs(c, hh, gi)
                r_out[...] = cfg["ret_dtot"][hh] * r_in[...] + _mm(kr_s[rs, sl].T, vb[ls])
                dec = dtotc[gi * group:gi * group + 1, hh:hh + 1]
                g_out[...] = dec * g_in[...] + _mm(kg_s[rs, sl].T, vnb[ls])

    for hh in range(2 * HEADS):
        sl = head_sl(hh)
        goff = OFF_RG + hh * HEAD_DIM if hh < HEADS else OFF_Z + (hh - HEADS) * HEAD_DIM
        y = _rms(o_s[:, sl], normw_ref[:, sl]) * _silu(proj_s[:, goff:goff + HEAD_DIM])
        mix_ref[:, sl] = y.astype(mix_ref.dtype)

    if prompt:
        @pl.when(pl.program_id(1) == pl.num_programs(1) - 1)
        def _():
            sret_out[0] = sret_s[...]
            sgdn_out[0] = sgdn_s[...]


def _post_kernel(mix_ref, x_ref, wout_ref, postw_ref, premlpw_ref, wup_ref, wdown_ref, postmlpw_ref, out_ref):
    tm = x_ref.shape[0]
    halves = [slice(0, tm // 2), slice(tm // 2, tm)] if tm % 16 == 0 else [slice(0, tm)]
    m = [_dot(mix_ref[r, :], wout_ref[...]) for r in halves]
    x1 = [x_ref[r, :] + _rms(mi, postw_ref[...]) for r, mi in zip(halves, m)]
    h2 = [_bf(_rms(xi, premlpw_ref[...])) for xi in x1]
    acc = [None] * len(halves)
    for c in range(D_FF // D_MODEL):
        for i in range(len(halves)):
            f = _dot(h2[i], wup_ref[:, c * D_MODEL:(c + 1) * D_MODEL])
            f = jnp.square(jnp.maximum(f, 0.0))
            part = _dot(_bf(f), wdown_ref[c * D_MODEL:(c + 1) * D_MODEL, :])
            acc[i] = part if acc[i] is None else acc[i] + part
    for r, xi, ai in zip(halves, x1, acc):
        out_ref[r, :] = xi + _rms(ai, postmlpw_ref[...])


def _const_spec(shape, single=False):
    index_map = lambda *_: (0,) * len(shape)
    if single:
        return pl.BlockSpec(shape, index_map, pipeline_mode=pl.Buffered(1))
    return pl.BlockSpec(shape, index_map)


def _retention_tables(group):
    log_gamma = np.log(1.0 - 2.0 ** (-5.0 - np.arange(HEADS, dtype=np.float64)))
    t = np.arange(ROWS) % group
    same = (np.arange(ROWS)[:, None] // group) == (np.arange(ROWS)[None, :] // group)
    causal = same & (t[:, None] >= t[None, :])
    dmat = np.where(causal[None], np.exp((t[:, None] - t[None, :])[None] * log_gamma[:, None, None]), 0.0)
    ecum = np.exp((t[:, None] + 1.0) * log_gamma[None, :])
    tail = np.exp((group - 1.0 - t[:, None]) * log_gamma[None, :])
    rtab = np.stack([np.repeat(ecum, HEAD_DIM, axis=1), np.repeat(tail, HEAD_DIM, axis=1)])
    dtot = tuple(float(np.float32(v)) for v in np.exp(group * log_gamma))
    return jnp.asarray(dmat, F32), jnp.asarray(rtab, F32), dtot


def _rope_tables(length, offset):
    inv_freq = ROPE_BASE ** (-jnp.arange(0, HEAD_DIM, 2, dtype=F32) / HEAD_DIM)
    pos = jnp.arange(length, dtype=F32) + jnp.asarray(offset, F32)
    ang = pos[:, None] * inv_freq[None, :]
    cos, sin = jnp.cos(ang), jnp.sin(ang)
    return jnp.concatenate([cos, cos], axis=-1), jnp.concatenate([-sin, sin], axis=-1)


def _mixer(x, params, *, prompt, states=None):
    b, l, _ = x.shape
    t = b * l
    x2 = x.reshape(t, D_MODEL)
    if prompt:
        group, lb = min(ROWS, l), min(256, l)
        assert l % lb == 0 and lb % ROWS == 0 and l >= 8
        cos, sin = _rope_tables(l, 0)
        grid = (b, l // lb)
        row_map = lambda i, jj: (i * (l // lb) + jj, 0)
        pos_map = lambda i, jj: (jj, 0)
        dims = ("arbitrary", "arbitrary")
    else:
        group = l
        lb = min(2 * ROWS, t)
        assert ROWS % group == 0 and group == 8 and t % lb == 0 and lb % ROWS == 0
        cos, sin = _rope_tables(l, PAST_LEN)
        cos, sin = jnp.tile(cos, (lb // l, 1)), jnp.tile(sin, (lb // l, 1))
        grid = (t // lb,)
        row_map = lambda i: (i, 0)
        pos_map = lambda i: (0, 0)
        dims = ("arbitrary",)
    nseq = lb // group
    dmat, rtab, dtot = _retention_tables(group)
    cfg = dict(lb=lb, group=group, prompt=prompt, ret_dtot=dtot)

    in_specs = [
        pl.BlockSpec((lb, D_MODEL), row_map),
        _const_spec((1, D_MODEL)),
        _const_spec((D_MODEL, IN_W), single=True),
        _const_spec((D_MODEL, LANES), single=True),
        pl.BlockSpec((lb, HEAD_DIM), pos_map),
        pl.BlockSpec((lb, HEAD_DIM), pos_map),
        _const_spec((HEADS, ROWS, ROWS)),
        _const_spec((2, ROWS, GROUP_W)),
        _const_spec((CONV_W, CONV_CH)),
        _const_spec((8, LANES)),
        _const_spec((1, 2 * GROUP_W)),
    ]
    args = [x2, params["pre_w"], params["w_in"], params["w_ab"], cos, sin, dmat, rtab,
            params["conv_w"], params["pvec"], params["normw"]]
    state_shape = (HEADS, HEAD_DIM, HEAD_DIM)
    if prompt:
        out_specs = [
            pl.BlockSpec((lb, 2 * GROUP_W), row_map),
            pl.BlockSpec((1,) + state_shape, lambda i, jj: (i, 0, 0, 0)),
            pl.BlockSpec((1,) + state_shape, lambda i, jj: (i, 0, 0, 0)),
            pl.BlockSpec((8, CONV_CH), lambda i, jj: (i, 0)),
        ]
        out_shape = [
            jax.ShapeDtypeStruct((t, 2 * GROUP_W), BF16),
            jax.ShapeDtypeStruct((b,) + state_shape, F32),
            jax.ShapeDtypeStruct((b,) + state_shape, F32),
            jax.ShapeDtypeStruct((b * 8, CONV_CH), F32),
        ]
    else:
        s_ret, s_gdn, s_conv = states
        st8 = jnp.pad(s_conv.astype(F32), ((0, 0), (8 - (CONV_W - 1), 0), (0, 0))).reshape(t, CONV_CH)
        in_specs += [
            pl.BlockSpec((nseq,) + state_shape, lambda i: (i, 0, 0, 0)),
            pl.BlockSpec((nseq,) + state_shape, lambda i: (i, 0, 0, 0)),
            pl.BlockSpec((lb, CONV_CH), row_map),
        ]
        args += [s_ret.astype(F32), s_gdn.astype(F32), st8]
        out_specs = [
            pl.BlockSpec((lb, 2 * GROUP_W), row_map),
            pl.BlockSpec((nseq,) + state_shape, lambda i: (i, 0, 0, 0)),
            pl.BlockSpec((nseq,) + state_shape, lambda i: (i, 0, 0, 0)),
            pl.BlockSpec((lb, CONV_CH), row_map),
        ]
        out_shape = [
            jax.ShapeDtypeStruct((t, 2 * GROUP_W), BF16),
            jax.ShapeDtypeStruct((b,) + state_shape, F32),
            jax.ShapeDtypeStruct((b,) + state_shape, F32),
            jax.ShapeDtypeStruct((t, CONV_CH), F32),
        ]
    nmat = lb // ROWS * HEADS
    scratch = [
        pltpu.VMEM((lb, MAIN_W), F32),
        pltpu.VMEM((lb, GROUP_W), F32), pltpu.VMEM((lb, GROUP_W), F32),
        pltpu.VMEM((lb, GROUP_W), F32), pltpu.VMEM((lb, GROUP_W), F32), pltpu.VMEM((lb, GROUP_W), F32),
        pltpu.VMEM((lb, LANES), F32), pltpu.VMEM((lb, LANES), F32), pltpu.VMEM((lb, LANES), F32),
        pltpu.VMEM((lb, 2 * GROUP_W), F32),
        pltpu.VMEM((lb, GROUP_W), F32), pltpu.VMEM((lb, GROUP_W), F32),
        pltpu.VMEM((nmat, ROWS, ROWS), F32), pltpu.VMEM((nmat, ROWS, ROWS), F32),
    ]
    if prompt:
        scratch += [pltpu.VMEM(state_shape, F32), pltpu.VMEM(state_shape, F32), pltpu.VMEM((8, CONV_CH), F32)]

    mix, s_ret_new, s_gdn_new, conv_rows = pl.pallas_call(
        functools.partial(_mixer_kernel, cfg),
        grid=grid,
        in_specs=in_specs,
        out_specs=out_specs,
        out_shape=out_shape,
        scratch_shapes=scratch,
        compiler_params=pltpu.CompilerParams(dimension_semantics=dims, vmem_limit_bytes=VMEM_LIMIT),
        name="mixer_prompt" if prompt else "mixer_sample",
    )(*args)
    new_conv = conv_rows.reshape(b, 8, CONV_CH)[:, 8 - (CONV_W - 1):]
    return mix, s_ret_new, s_gdn_new, new_conv


def _post(mix, x, params):
    b, l, _ = x.shape
    t = b * l
    tm = min(512, t)
    assert t % tm == 0
    rows = pl.BlockSpec((tm, D_MODEL), lambda i: (i, 0))
    y = pl.pallas_call(
        _post_kernel,
        grid=(t // tm,),
        in_specs=[rows, rows, _const_spec((2 * GROUP_W, D_MODEL)), _const_spec((1, D_MODEL)),
                  _const_spec((1, D_MODEL)), _const_spec((D_MODEL, D_FF)), _const_spec((D_FF, D_MODEL)),
                  _const_spec((1, D_MODEL))],
        out_specs=rows,
        out_shape=jax.ShapeDtypeStruct((t, D_MODEL), F32),
        compiler_params=pltpu.CompilerParams(dimension_semantics=("arbitrary",), vmem_limit_bytes=VMEM_LIMIT),
        name="post",
    )(mix, x.reshape(t, D_MODEL), params["w_out"], params["post_w"], params["pre_mlp_w"],
      params["w_up"], params["w_down"], params["post_mlp_w"])
    return y.reshape(b, l, D_MODEL)


def _layer_params(l, pre_mix_w, w_in, conv_w, A_log, dt_bias, ret_norm_w, gdn_norm_w, w_out, post_mix_w,
                  pre_mlp_w, w_up, w_down, post_mlp_w):
    w = _bf(w_in[l])
    pvec = jnp.zeros((8, LANES), F32)
    pvec = pvec.at[0, :HEADS].set(A_log[l].astype(F32)).at[1, :HEADS].set(dt_bias[l].astype(F32))
    return dict(
        pre_w=pre_mix_w[l].astype(F32)[None],
        w_in=w,
        w_ab=jnp.pad(w[:, MAIN_W:], ((0, 0), (0, LANES - 2 * HEADS))),
        conv_w=conv_w[l].astype(F32),
        pvec=pvec,
        normw=jnp.concatenate([jnp.tile(ret_norm_w[l], HEADS), jnp.tile(gdn_norm_w[l], HEADS)]).astype(F32)[None],
        w_out=_bf(w_out[l]),
        post_w=post_mix_w[l].astype(F32)[None],
        pre_mlp_w=pre_mlp_w[l].astype(F32)[None],
        w_up=_bf(w_up[l]),
        w_down=_bf(w_down[l]),
        post_mlp_w=post_mlp_w[l].astype(F32)[None],
    )


def kernel(x_prompt, x_sample, state_ret, state_gdn, state_conv, pre_mix_w, w_in, conv_w, A_log, dt_bias,
           ret_norm_w, gdn_norm_w, w_out, post_mix_w, pre_mlp_w, w_up, w_down, post_mlp_w):
    depth = w_in.shape[0]
    yp, ys = x_prompt, x_sample
    outs = [[] for _ in range(6)]
    for l in range(depth):
        params = _layer_params(l, pre_mix_w, w_in, conv_w, A_log, dt_bias, ret_norm_w, gdn_norm_w, w_out,
                               post_mix_w, pre_mlp_w, w_up, w_down, post_mlp_w)
        mix_p, rp, gp, cp = _mixer(yp, params, prompt=True)
        yp = _post(mix_p, yp, params)
        mix_s, rs, gs, cs = _mixer(ys, params, prompt=False,
                                   states=(state_ret[l], state_gdn[l], state_conv[l]))
        ys = _post(mix_s, ys, params)
        for dst, val, like in zip(outs, (rp, gp, cp, rs, gs, cs),
                                  (state_ret, state_gdn, state_conv) * 2):
            dst.append(val.astype(like.dtype))
    return (yp, ys) + tuple(jnp.stack(o) for o in outs)
```

```python
import functools

import numpy as np
import jax
import jax.numpy as jnp
from jax import lax
from jax.experimental import pallas as pl
from jax.experimental.pallas import tpu as pltpu

D_MODEL = 1024
HEADS = 4
HEAD_DIM = 128
GROUP_W = HEADS * HEAD_DIM
CONV_CH = 3 * GROUP_W
CONV_W = 4
D_FF = 4 * D_MODEL
MAIN_W = 4 * GROUP_W + CONV_CH + GROUP_W
IN_W = MAIN_W + 2 * HEADS
ROPE_BASE = 10000.0
EPS = 1e-6
PAST_LEN = 16384

ROWS = 64
BASE = 8
LANES = 128
VMEM_LIMIT = 58 * 1024 * 1024

OFF_RQ, OFF_RK, OFF_RV, OFF_RG = 0, GROUP_W, 2 * GROUP_W, 3 * GROUP_W
OFF_CONV = 4 * GROUP_W
OFF_Z = OFF_CONV + CONV_CH

F32 = jnp.float32
BF16 = jnp.bfloat16


def _bf(x):
    return x.astype(BF16)


def _dot(a, b):
    return jnp.dot(a, b, preferred_element_type=F32)


def _dot_nt(a, b):
    return lax.dot_general(a, b, (((1,), (1,)), ((), ())), preferred_element_type=F32)


def _split(x, n):
    parts, r = [], x
    for i in range(n):
        p = r.astype(BF16)
        parts.append(p)
        if i + 1 < n:
            r = r - p.astype(F32)
    return parts


def _mm(a, b, nt=False):
    return _dot_nt(_bf(a), _bf(b)) if nt else _dot(_bf(a), _bf(b))


def _mm3(a, b):
    (a_hi, a_lo), (b_hi, b_lo) = _split(a, 2), _split(b, 2)
    n = a.shape[0]
    both = _dot(jnp.concatenate([a_hi, a_lo], axis=0), b_hi)
    return both[:n] + both[n:] + _dot(a_hi, b_lo)


def _rms(x, w):
    return x * lax.rsqrt(jnp.mean(x * x, axis=-1, keepdims=True) + EPS) * w


def _silu(x):
    return x / (1.0 + jnp.exp(-x))


def _tri_inverse_many(mats, group, row, col):
    n = mats[0].shape[0]
    packed = [jnp.concatenate(mats[i:i + HEADS], axis=1) for i in range(0, len(mats), HEADS)]
    prow = lax.broadcasted_iota(jnp.int32, (n, HEADS * n), 0)
    plane = lax.broadcasted_iota(jnp.int32, (n, HEADS * n), 1)
    pcol = plane % n
    seg = [_bf(((plane // n) == hh).astype(F32)) for hh in range(HEADS)]

    def blockdiag(y):
        return jnp.concatenate([y * m for m in seg], axis=0)

    def mm3(x, y):
        (x_hi, x_lo), (y_hi, y_lo) = _split(x, 2), _split(y, 2)
        both = _dot(jnp.concatenate([x_hi, x_lo], axis=0), blockdiag(y_hi))
        return both[:n] + both[n:] + _dot(x_hi, blockdiag(y_lo))

    eye_p = (prow == pcol).astype(F32)
    blk = (prow // BASE) == (pcol // BASE)
    n8 = [jnp.where(blk, -a, 0.0) for a in packed]
    p2 = [mm3(x, x) for x in n8]
    np2 = [mm3(x, p) for x, p in zip(n8, p2)]
    p4 = [mm3(p, p) for p in p2]
    e = [x + p + y for x, p, y in zip(n8, p2, np2)]
    ep4 = [mm3(x, p) for x, p in zip(e, p4)]
    t = [x + p + y + eye_p for x, p, y in zip(e, p4, ep4)]
    s = BASE
    while s < group:
        off = ((prow // (2 * s)) == (pcol // (2 * s))) & ((prow // s) != (pcol // s))
        a_off = [jnp.where(off, a, 0.0) for a in packed]
        ta = [mm3(x, y) for x, y in zip(t, a_off)]
        tat = [mm3(x, y) for x, y in zip(ta, t)]
        t = [x - y for x, y in zip(t, tat)]
        s *= 2
    eye = (row == col).astype(F32)
    return [tp[:, hh * n:(hh + 1) * n] for tp in t for hh in range(HEADS)], eye


def _mixer_kernel(cfg, *refs):
    lb, group, prompt = cfg["lb"], cfg["group"], cfg["prompt"]
    ngroups = ROWS // group
    chunks = list(range(lb // ROWS))
    it = iter(refs)
    x_ref, prew_ref, win_ref, wab_ref, cos_ref, sin_ref = (next(it) for _ in range(6))
    dmat_ref, rtab_ref, convw_ref, pvec_ref, normw_ref = (next(it) for _ in range(5))
    if not prompt:
        sret_in, sgdn_in, st8_ref = (next(it) for _ in range(3))
    mix_ref, sret_out, sgdn_out, conv_out = (next(it) for _ in range(4))
    proj_s, qr_s, kr_s, qg_s, kg_s, vg_s, cum_s, tot_s, beta_s, o_s = (next(it) for _ in range(10))
    w_s, u_s, pr_s, pg_s = (next(it) for _ in range(4))
    if prompt:
        sret_s, sgdn_s, prev8_s = (next(it) for _ in range(3))

        @pl.when(pl.program_id(1) == 0)
        def _():
            sret_s[...] = jnp.zeros_like(sret_s)
            sgdn_s[...] = jnp.zeros_like(sgdn_s)
            prev8_s[...] = jnp.zeros_like(prev8_s)

    rows_of = lambda c: slice(c * ROWS, (c + 1) * ROWS)
    head_sl = lambda hh: slice(hh * HEAD_DIM, (hh + 1) * HEAD_DIM)
    lane = lambda x, n: x[:, n:n + 1]
    cat = lambda xs: xs[0] if len(xs) == 1 else jnp.concatenate(xs, axis=0)

    h = _bf(_rms(x_ref[...], prew_ref[...]))
    proj_s[...] = _dot(h, win_ref[:, :MAIN_W])
    ab = _dot(h, wab_ref[...])

    cos, sin = cos_ref[...], sin_ref[...]
    for hh in range(HEADS):
        sl = head_sl(hh)
        q = proj_s[:, OFF_RQ + hh * HEAD_DIM:OFF_RQ + (hh + 1) * HEAD_DIM]
        qr_s[:, sl] = q * cos + pltpu.roll(q, HEAD_DIM // 2, 1) * sin
        k = proj_s[:, OFF_RK + hh * HEAD_DIM:OFF_RK + (hh + 1) * HEAD_DIM]
        kr_s[:, sl] = (k * cos + pltpu.roll(k, HEAD_DIM // 2, 1) * sin) * (HEAD_DIM ** -0.5)

    t_in_group = lax.broadcasted_iota(jnp.int32, (lb, 1), 0) % group
    for part, dst in enumerate((qg_s, kg_s, vg_s)):
        cols = slice(OFF_CONV + part * GROUP_W, OFF_CONV + (part + 1) * GROUP_W)
        xq = proj_s[:, cols]
        cw = convw_ref[:, part * GROUP_W:(part + 1) * GROUP_W]
        acc = xq * cw[CONV_W - 1:CONV_W]
        if prompt:
            ext = jnp.concatenate([prev8_s[:, part * GROUP_W:(part + 1) * GROUP_W], xq], axis=0)
            for s in range(1, CONV_W):
                sh = pltpu.roll(ext, s, 0)[8:]
                acc = acc + sh * cw[CONV_W - 1 - s:CONV_W - s]
        else:
            st8 = st8_ref[:, part * GROUP_W:(part + 1) * GROUP_W]
            for s in range(1, CONV_W):
                sh = jnp.where(t_in_group < s, pltpu.roll(st8, lb - 8 + s, 0), pltpu.roll(xq, s, 0))
                acc = acc + sh * cw[CONV_W - 1 - s:CONV_W - s]
        conv = _silu(acc)
        for hh in range(HEADS):
            sl = head_sl(hh)
            c = conv[:, sl]
            if part < 2:
                c = c * lax.rsqrt(jnp.sum(c * c, axis=-1, keepdims=True) + EPS)
            if part == 0:
                c = c * (HEAD_DIM ** -0.5)
            dst[:, sl] = c
    if prompt:
        prev8_s[...] = proj_s[lb - 8:lb, OFF_CONV:OFF_CONV + CONV_CH]
        conv_out[...] = proj_s[lb - 8:lb, OFF_CONV:OFF_CONV + CONV_CH]
    else:
        conv_out[...] = proj_s[:, OFF_CONV:OFF_CONV + CONV_CH]

    pv = pvec_ref[...]
    a_plus = ab + pv[1:2]
    softplus = jnp.maximum(a_plus, 0.0) + jnp.log1p(jnp.exp(-jnp.abs(a_plus)))
    g = -jnp.exp(pv[0:1]) * softplus
    beta_s[...] = 1.0 / (1.0 + jnp.exp(-ab))

    row = lax.broadcasted_iota(jnp.int32, (ROWS, ROWS), 0)
    col = lax.broadcasted_iota(jnp.int32, (ROWS, ROWS), 1)
    same = (row // group) == (col // group)
    causal = same & (row >= col)
    strict = same & (row > col)
    sum_mat = _bf(jnp.concatenate([causal.astype(F32), same.astype(F32)], axis=0))
    sum_mat3 = jnp.concatenate([sum_mat] * 3, axis=1)
    for c in chunks:
        r = _dot(sum_mat3, jnp.concatenate(_split(g[rows_of(c)], 3), axis=0))
        cum_s[rows_of(c), :] = r[:ROWS]
        tot_s[rows_of(c), :] = r[ROWS:]
    sel = _bf((lax.broadcasted_iota(jnp.int32, (8, LANES), 0)
               == lax.broadcasted_iota(jnp.int32, (8, LANES), 1)).astype(F32))
    sel3 = jnp.concatenate([sel] * 3, axis=1)

    items = [(c, hh) for c in chunks for hh in range(HEADS)]
    cumc = [cum_s[rows_of(c), :] for c in chunks]
    totc = [tot_s[rows_of(c), :] for c in chunks]
    betac = [beta_s[rows_of(c), :] for c in chunks]
    ecumc = [jnp.exp(x) for x in cumc]
    tailc = [jnp.exp(t - x) for t, x in zip(totc, cumc)]
    cumrow = [_dot_nt(sel3, jnp.concatenate(_split(x, 3), axis=1)) for x in cumc]

    for c, hh in items:
        rows, sl = rows_of(c), head_sl(hh)
        q, k = qr_s[rows, sl], kr_s[rows, sl]
        pr_s[c * HEADS + hh] = _mm(q, k, nt=True) * dmat_ref[hh]
        qr_s[rows, sl] = q * rtab_ref[0, :, sl]
        kr_s[rows, sl] = k * rtab_ref[1, :, sl]

    qk_kk, dmats = [], []
    for c, hh in items:
        rows, sl = rows_of(c), head_sl(hh)
        kb = _bf(kg_s[rows, sl])
        qk_kk.append(_dot_nt(jnp.concatenate([_bf(qg_s[rows, sl]), kb], axis=0), kb))
        diff = lane(cumc[c], hh) - cumrow[c][hh:hh + 1, :]
        dmats.append(jnp.where(causal, jnp.exp(jnp.minimum(diff, 0.0)), 0.0))
    a_mats = [jnp.where(strict, lane(betac[c], HEADS + hh) * x[ROWS:] * d, 0.0)
              for (c, hh), x, d in zip(items, qk_kk, dmats)]
    t_inv, eye = _tri_inverse_many(a_mats, group, row, col)
    for i, (c, hh) in enumerate(items):
        rows, sl = rows_of(c), head_sl(hh)
        q, k, v = qg_s[rows, sl], kg_s[rows, sl], vg_s[rows, sl]
        beta_h = lane(betac[c], HEADS + hh)
        rhs = jnp.concatenate([k * (beta_h * lane(ecumc[c], hh)), v * beta_h], axis=1)
        sol = rhs + _mm(t_inv[i] - eye, rhs)
        w_s[rows, sl] = sol[:, :HEAD_DIM]
        u_s[rows, sl] = sol[:, HEAD_DIM:]
        pg_s[c * HEADS + hh] = qk_kk[i][:ROWS] * dmats[i]
        qg_s[rows, sl] = q * lane(ecumc[c], hh)
        kg_s[rows, sl] = k * lane(tailc[c], hh)

    def state_refs(c, hh, gi):
        if prompt:
            return (sret_s.at[hh], sret_s.at[hh], sgdn_s.at[hh], sgdn_s.at[hh])
        n = c * ngroups + gi
        return (sret_in.at[n, hh], sret_out.at[n, hh], sgdn_in.at[n, hh], sgdn_out.at[n, hh])

    for c in chunks:
        dtotc = jnp.exp(totc[c])
        qs_r, qs_g, ws_g = {}, {}, {}
        for hh in range(HEADS):
            sl = head_sl(hh)
            for gi in range(ngroups):
                rs = slice(c * ROWS + gi * group, c * ROWS + (gi + 1) * group)
                r_in, _, g_in, _ = state_refs(c, hh, gi)
                qs_r[hh, gi] = _mm(qr_s[rs, sl], r_in[...])
                r = _mm(jnp.concatenate([qg_s[rs, sl], w_s[rs, sl]], axis=0), g_in[...])
                qs_g[hh, gi], ws_g[hh, gi] = r[:group], r[group:]
        for hh in range(HEADS):
            rows, sl = rows_of(c), head_sl(hh)
            i = c * HEADS + hh
            vb = _bf(proj_s[rows, OFF_RV + hh * HEAD_DIM:OFF_RV + (hh + 1) * HEAD_DIM])
            v_new = u_s[rows, sl] - cat([ws_g[hh, gi] for gi in range(ngroups)])
            vnb = _bf(v_new)
            o_s[rows, sl] = cat([qs_r[hh, gi] for gi in range(ngroups)]) + _mm(pr_s[i], vb)
            o_s[rows, GROUP_W + hh * HEAD_DIM:GROUP_W + (hh + 1) * HEAD_DIM] = (
                cat([qs_g[hh, gi] for gi in range(ngroups)]) + _mm(pg_s[i], vnb))
            for gi in range(ngroups):
                rs = slice(c * ROWS + gi * group, c * ROWS + (gi + 1) * group)
                ls = slice(gi * group, (gi + 1) * group)
                r_in, r_out, g_in, g_out = state_refs(c, hh, gi)
                r_out[...] = cfg["ret_dtot"][hh] * r_in[...] + _mm(kr_s[rs, sl].T, vb[ls])
                dec = dtotc[gi * group:gi * group + 1, hh:hh + 1]
                g_out[...] = dec * g_in[...] + _mm(kg_s[rs, sl].T, vnb[ls])

    for hh in range(2 * HEADS):
        sl = head_sl(hh)
        goff = OFF_RG + hh * HEAD_DIM if hh < HEADS else OFF_Z + (hh - HEADS) * HEAD_DIM
        y = _rms(o_s[:, sl], normw_ref[:, sl]) * _silu(proj_s[:, goff:goff + HEAD_DIM])
        mix_ref[:, sl] = y.astype(mix_ref.dtype)

    if prompt:
        @pl.when(pl.program_id(1) == pl.num_programs(1) - 1)
        def _():
            sret_out[0] = sret_s[...]
            sgdn_out[0] = sgdn_s[...]


def _post_kernel(mix_ref, x_ref, wout_ref, postw_ref, premlpw_ref, wup_ref, wdown_ref, postmlpw_ref, out_ref):
    tm = x_ref.shape[0]
    halves = [slice(0, tm // 2), slice(tm // 2, tm)] if tm % 16 == 0 else [slice(0, tm)]
    m = [_dot(mix_ref[r, :], wout_ref[...]) for r in halves]
    x1 = [x_ref[r, :] + _rms(mi, postw_ref[...]) for r, mi in zip(halves, m)]
    h2 = [_bf(_rms(xi, premlpw_ref[...])) for xi in x1]
    acc = [None] * len(halves)
    for c in range(D_FF // D_MODEL):
        for i in range(len(halves)):
            f = _dot(h2[i], wup_ref[:, c * D_MODEL:(c + 1) * D_MODEL])
            f = jnp.square(jnp.maximum(f, 0.0))
            part = _dot(_bf(f), wdown_ref[c * D_MODEL:(c + 1) * D_MODEL, :])
            acc[i] = part if acc[i] is None else acc[i] + part
    for r, xi, ai in zip(halves, x1, acc):
        out_ref[r, :] = xi + _rms(ai, postmlpw_ref[...])


def _const_spec(shape, single=False):
    index_map = lambda *_: (0,) * len(shape)
    if single:
        return pl.BlockSpec(shape, index_map, pipeline_mode=pl.Buffered(1))
    return pl.BlockSpec(shape, index_map)


def _retention_tables(group):
    log_gamma = np.log(1.0 - 2.0 ** (-5.0 - np.arange(HEADS, dtype=np.float64)))
    t = np.arange(ROWS) % group
    same = (np.arange(ROWS)[:, None] // group) == (np.arange(ROWS)[None, :] // group)
    causal = same & (t[:, None] >= t[None, :])
    dmat = np.where(causal[None], np.exp((t[:, None] - t[None, :])[None] * log_gamma[:, None, None]), 0.0)
    ecum = np.exp((t[:, None] + 1.0) * log_gamma[None, :])
    tail = np.exp((group - 1.0 - t[:, None]) * log_gamma[None, :])
    rtab = np.stack([np.repeat(ecum, HEAD_DIM, axis=1), np.repeat(tail, HEAD_DIM, axis=1)])
    dtot = tuple(float(np.float32(v)) for v in np.exp(group * log_gamma))
    return jnp.asarray(dmat, F32), jnp.asarray(rtab, F32), dtot


def _rope_tables(length, offset):
    inv_freq = ROPE_BASE ** (-jnp.arange(0, HEAD_DIM, 2, dtype=F32) / HEAD_DIM)
    pos = jnp.arange(length, dtype=F32) + jnp.asarray(offset, F32)
    ang = pos[:, None] * inv_freq[None, :]
    cos, sin = jnp.cos(ang), jnp.sin(ang)
    return jnp.concatenate([cos, cos], axis=-1), jnp.concatenate([-sin, sin], axis=-1)


def _mixer(x, params, *, prompt, states=None):
    b, l, _ = x.shape
    t = b * l
    x2 = x.reshape(t, D_MODEL)
    if prompt:
        group, lb = min(ROWS, l), min(512, l)
        assert l % lb == 0 and lb % ROWS == 0 and l >= 8
        cos, sin = _rope_tables(l, 0)
        grid = (b, l // lb)
        row_map = lambda i, jj: (i * (l // lb) + jj, 0)
        pos_map = lambda i, jj: (jj, 0)
        dims = ("arbitrary", "arbitrary")
    else:
        group = l
        lb = min(2 * ROWS, t)
        assert ROWS % group == 0 and group == 8 and t % lb == 0 and lb % ROWS == 0
        cos, sin = _rope_tables(l, PAST_LEN)
        cos, sin = jnp.tile(cos, (lb // l, 1)), jnp.tile(sin, (lb // l, 1))
        grid = (t // lb,)
        row_map = lambda i: (i, 0)
        pos_map = lambda i: (0, 0)
        dims = ("arbitrary",)
    nseq = lb // group
    dmat, rtab, dtot = _retention_tables(group)
    cfg = dict(lb=lb, group=group, prompt=prompt, ret_dtot=dtot)

    in_specs = [
        pl.BlockSpec((lb, D_MODEL), row_map),
        _const_spec((1, D_MODEL)),
        _const_spec((D_MODEL, IN_W), single=True),
        _const_spec((D_MODEL, LANES), single=True),
        pl.BlockSpec((lb, HEAD_DIM), pos_map),
        pl.BlockSpec((lb, HEAD_DIM), pos_map),
        _const_spec((HEADS, ROWS, ROWS)),
        _const_spec((2, ROWS, GROUP_W)),
        _const_spec((CONV_W, CONV_CH)),
        _const_spec((8, LANES)),
        _const_spec((1, 2 * GROUP_W)),
    ]
    args = [x2, params["pre_w"], params["w_in"], params["w_ab"], cos, sin, dmat, rtab,
            params["conv_w"], params["pvec"], params["normw"]]
    state_shape = (HEADS, HEAD_DIM, HEAD_DIM)
    if prompt:
        out_specs = [
            pl.BlockSpec((lb, 2 * GROUP_W), row_map),
            pl.BlockSpec((1,) + state_shape, lambda i, jj: (i, 0, 0, 0)),
            pl.BlockSpec((1,) + state_shape, lambda i, jj: (i, 0, 0, 0)),
            pl.BlockSpec((8, CONV_CH), lambda i, jj: (i, 0)),
        ]
        out_shape = [
            jax.ShapeDtypeStruct((t, 2 * GROUP_W), BF16),
            jax.ShapeDtypeStruct((b,) + state_shape, F32),
            jax.ShapeDtypeStruct((b,) + state_shape, F32),
            jax.ShapeDtypeStruct((b * 8, CONV_CH), F32),
        ]
    else:
        s_ret, s_gdn, s_conv = states
        st8 = jnp.pad(s_conv.astype(F32), ((0, 0), (8 - (CONV_W - 1), 0), (0, 0))).reshape(t, CONV_CH)
        in_specs += [
            pl.BlockSpec((nseq,) + state_shape, lambda i: (i, 0, 0, 0)),
            pl.BlockSpec((nseq,) + state_shape, lambda i: (i, 0, 0, 0)),
            pl.BlockSpec((lb, CONV_CH), row_map),
        ]
        args += [s_ret.astype(F32), s_gdn.astype(F32), st8]
        out_specs = [
            pl.BlockSpec((lb, 2 * GROUP_W), row_map),
            pl.BlockSpec((nseq,) + state_shape, lambda i: (i, 0, 0, 0)),
            pl.BlockSpec((nseq,) + state_shape, lambda i: (i, 0, 0, 0)),
            pl.BlockSpec((lb, CONV_CH), row_map),
        ]
        out_shape = [
            jax.ShapeDtypeStruct((t, 2 * GROUP_W), BF16),
            jax.ShapeDtypeStruct((b,) + state_shape, F32),
            jax.ShapeDtypeStruct((b,) + state_shape, F32),
            jax.ShapeDtypeStruct((t, CONV_CH), F32),
        ]
    nmat = lb // ROWS * HEADS
    scratch = [
        pltpu.VMEM((lb, MAIN_W), F32),
        pltpu.VMEM((lb, GROUP_W), F32), pltpu.VMEM((lb, GROUP_W), F32),
        pltpu.VMEM((lb, GROUP_W), F32), pltpu.VMEM((lb, GROUP_W), F32), pltpu.VMEM((lb, GROUP_W), F32),
        pltpu.VMEM((lb, LANES), F32), pltpu.VMEM((lb, LANES), F32), pltpu.VMEM((lb, LANES), F32),
        pltpu.VMEM((lb, 2 * GROUP_W), F32),
        pltpu.VMEM((lb, GROUP_W), F32), pltpu.VMEM((lb, GROUP_W), F32),
        pltpu.VMEM((nmat, ROWS, ROWS), F32), pltpu.VMEM((nmat, ROWS, ROWS), F32),
    ]
    if prompt:
        scratch += [pltpu.VMEM(state_shape, F32), pltpu.VMEM(state_shape, F32), pltpu.VMEM((8, CONV_CH), F32)]

    mix, s_ret_new, s_gdn_new, conv_rows = pl.pallas_call(
        functools.partial(_mixer_kernel, cfg),
        grid=grid,
        in_specs=in_specs,
        out_specs=out_specs,
        out_shape=out_shape,
        scratch_shapes=scratch,
        compiler_params=pltpu.CompilerParams(dimension_semantics=dims, vmem_limit_bytes=VMEM_LIMIT),
        name="mixer_prompt" if prompt else "mixer_sample",
    )(*args)
    new_conv = conv_rows.reshape(b, 8, CONV_CH)[:, 8 - (CONV_W - 1):]
    return mix, s_ret_new, s_gdn_new, new_conv


def _post(mix, x, params):
    b, l, _ = x.shape
    t = b * l
    tm = min(512, t)
    assert t % tm == 0
    rows = pl.BlockSpec((tm, D_MODEL), lambda i: (i, 0))
    y = pl.pallas_call(
        _post_kernel,
        grid=(t // tm,),
        in_specs=[rows, rows, _const_spec((2 * GROUP_W, D_MODEL)), _const_spec((1, D_MODEL)),
                  _const_spec((1, D_MODEL)), _const_spec((D_MODEL, D_FF)), _const_spec((D_FF, D_MODEL)),
                  _const_spec((1, D_MODEL))],
        out_specs=rows,
        out_shape=jax.ShapeDtypeStruct((t, D_MODEL), F32),
        compiler_params=pltpu.CompilerParams(dimension_semantics=("arbitrary",), vmem_limit_bytes=VMEM_LIMIT),
        name="post",
    )(mix, x.reshape(t, D_MODEL), params["w_out"], params["post_w"], params["pre_mlp_w"],
      params["w_up"], params["w_down"], params["post_mlp_w"])
    return y.reshape(b, l, D_MODEL)


def _layer_params(l, pre_mix_w, w_in, conv_w, A_log, dt_bias, ret_norm_w, gdn_norm_w, w_out, post_mix_w,
                  pre_mlp_w, w_up, w_down, post_mlp_w):
    w = _bf(w_in[l])
    pvec = jnp.zeros((8, LANES), F32)
    pvec = pvec.at[0, :HEADS].set(A_log[l].astype(F32)).at[1, :HEADS].set(dt_bias[l].astype(F32))
    return dict(
        pre_w=pre_mix_w[l].astype(F32)[None],
        w_in=w,
        w_ab=jnp.pad(w[:, MAIN_W:], ((0, 0), (0, LANES - 2 * HEADS))),
        conv_w=conv_w[l].astype(F32),
        pvec=pvec,
        normw=jnp.concatenate([jnp.tile(ret_norm_w[l], HEADS), jnp.tile(gdn_norm_w[l], HEADS)]).astype(F32)[None],
        w_out=_bf(w_out[l]),
        post_w=post_mix_w[l].astype(F32)[None],
        pre_mlp_w=pre_mlp_w[l].astype(F32)[None],
        w_up=_bf(w_up[l]),
        w_down=_bf(w_down[l]),
        post_mlp_w=post_mlp_w[l].astype(F32)[None],
    )


def kernel(x_prompt, x_sample, state_ret, state_gdn, state_conv, pre_mix_w, w_in, conv_w, A_log, dt_bias,
           ret_norm_w, gdn_norm_w, w_out, post_mix_w, pre_mlp_w, w_up, w_down, post_mlp_w):
    depth = w_in.shape[0]
    yp, ys = x_prompt, x_sample
    outs = [[] for _ in range(6)]
    for l in range(depth):
        params = _layer_params(l, pre_mix_w, w_in, conv_w, A_log, dt_bias, ret_norm_w, gdn_norm_w, w_out,
                               post_mix_w, pre_mlp_w, w_up, w_down, post_mlp_w)
        mix_p, rp, gp, cp = _mixer(yp, params, prompt=True)
        yp = _post(mix_p, yp, params)
        mix_s, rs, gs, cs = _mixer(ys, params, prompt=False,
                                   states=(state_ret[l], state_gdn[l], state_conv[l]))
        ys = _post(mix_s, ys, params)
        for dst, val, like in zip(outs, (rp, gp, cp, rs, gs, cs),
                                  (state_ret, state_gdn, state_conv) * 2):
            dst.append(val.astype(like.dtype))
    return (yp, ys) + tuple(jnp.stack(o) for o in outs)
```

```python
import functools

import numpy as np
import jax
import jax.numpy as jnp
from jax import lax
from jax.experimental import pallas as pl
from jax.experimental.pallas import tpu as pltpu

D_MODEL = 1024
HEADS = 4
HEAD_DIM = 128
GROUP_W = HEADS * HEAD_DIM
CONV_CH = 3 * GROUP_W
CONV_W = 4
D_FF = 4 * D_MODEL
MAIN_W = 4 * GROUP_W + CONV_CH + GROUP_W
IN_W = MAIN_W + 2 * HEADS
ROPE_BASE = 10000.0
EPS = 1e-6
PAST_LEN = 16384

ROWS = 64
BASE = 8
LANES = 128
VMEM_LIMIT = 58 * 1024 * 1024

OFF_RQ, OFF_RK, OFF_RV, OFF_RG = 0, GROUP_W, 2 * GROUP_W, 3 * GROUP_W
OFF_CONV = 4 * GROUP_W
OFF_Z = OFF_CONV + CONV_CH

F32 = jnp.float32
BF16 = jnp.bfloat16


def _bf(x):
    return x.astype(BF16)


def _dot(a, b):
    return jnp.dot(a, b, preferred_element_type=F32)


def _dot_nt(a, b):
    return lax.dot_general(a, b, (((1,), (1,)), ((), ())), preferred_element_type=F32)


def _split(x, n):
    parts, r = [], x
    for i in range(n):
        p = r.astype(BF16)
        parts.append(p)
        if i + 1 < n:
            r = r - p.astype(F32)
    return parts


def _mm(a, b, nt=False):
    return _dot_nt(_bf(a), _bf(b)) if nt else _dot(_bf(a), _bf(b))


def _mm3(a, b):
    (a_hi, a_lo), (b_hi, b_lo) = _split(a, 2), _split(b, 2)
    n = a.shape[0]
    both = _dot(jnp.concatenate([a_hi, a_lo], axis=0), b_hi)
    return both[:n] + both[n:] + _dot(a_hi, b_lo)


def _rms(x, w):
    return x * lax.rsqrt(jnp.mean(x * x, axis=-1, keepdims=True) + EPS) * w


def _silu(x):
    return x / (1.0 + jnp.exp(-x))


def _tri_inverse_many(mats, group, row, col):
    n = mats[0].shape[0]
    packed = [jnp.concatenate(mats[i:i + HEADS], axis=1) for i in range(0, len(mats), HEADS)]
    prow = lax.broadcasted_iota(jnp.int32, (n, HEADS * n), 0)
    plane = lax.broadcasted_iota(jnp.int32, (n, HEADS * n), 1)
    pcol = plane % n
    seg = [_bf(((plane // n) == hh).astype(F32)) for hh in range(HEADS)]

    def blockdiag(y):
        return jnp.concatenate([y * m for m in seg], axis=0)

    def mm3(x, y):
        (x_hi, x_lo), (y_hi, y_lo) = _split(x, 2), _split(y, 2)
        both = _dot(jnp.concatenate([x_hi, x_lo], axis=0), blockdiag(y_hi))
        return both[:n] + both[n:] + _dot(x_hi, blockdiag(y_lo))

    eye_p = (prow == pcol).astype(F32)
    blk = (prow // BASE) == (pcol // BASE)
    n8 = [jnp.where(blk, -a, 0.0) for a in packed]
    p2 = [mm3(x, x) for x in n8]
    np2 = [mm3(x, p) for x, p in zip(n8, p2)]
    p4 = [mm3(p, p) for p in p2]
    e = [x + p + y for x, p, y in zip(n8, p2, np2)]
    ep4 = [mm3(x, p) for x, p in zip(e, p4)]
    t = [x + p + y + eye_p for x, p, y in zip(e, p4, ep4)]
    s = BASE
    while s < group:
        off = ((prow // (2 * s)) == (pcol // (2 * s))) & ((prow // s) != (pcol // s))
        a_off = [jnp.where(off, a, 0.0) for a in packed]
        ta = [mm3(x, y) for x, y in zip(t, a_off)]
        tat = [mm3(x, y) for x, y in zip(ta, t)]
        t = [x - y for x, y in zip(t, tat)]
        s *= 2
    eye = (row == col).astype(F32)
    return [tp[:, hh * n:(hh + 1) * n] for tp in t for hh in range(HEADS)], eye


def _mixer_kernel(cfg, *refs):
    lb, group, prompt = cfg["lb"], cfg["group"], cfg["prompt"]
    ngroups = ROWS // group
    nchunk = lb // ROWS
    chunks = list(range(nchunk))
    it = iter(refs)
    x_ref, prew_ref, win_ref, wab_ref, cos_ref, sin_ref = (next(it) for _ in range(6))
    dmat_ref, rtab_ref, convw_ref, pvec_ref, normw_ref = (next(it) for _ in range(5))
    if not prompt:
        sret_in, sgdn_in, st8_ref = (next(it) for _ in range(3))
    mix_ref, sret_out, sgdn_out, conv_out = (next(it) for _ in range(4))
    proj_s, qr_s, kr_s, qg_s, kg_s, vg_s, cum_s, tot_s, beta_s, o_s = (next(it) for _ in range(10))
    w_s, u_s, pr_s, pg_s = (next(it) for _ in range(4))
    if prompt:
        sret_s, sgdn_s, prev8_s = (next(it) for _ in range(3))

        @pl.when(pl.program_id(0) == 0)
        def _():
            sret_s[...] = jnp.zeros_like(sret_s)
            sgdn_s[...] = jnp.zeros_like(sgdn_s)
            prev8_s[...] = jnp.zeros_like(prev8_s)

    rows_of = lambda c: slice(c * ROWS, (c + 1) * ROWS)
    head_sl = lambda hh: slice(hh * HEAD_DIM, (hh + 1) * HEAD_DIM)
    lane = lambda x, n: x[:, n:n + 1]
    cat = lambda xs: xs[0] if len(xs) == 1 else jnp.concatenate(xs, axis=0)

    x = x_ref[...].reshape(lb, D_MODEL)
    h = _bf(_rms(x, prew_ref[...]))
    proj_s[...] = _dot(h, win_ref[:, :MAIN_W])
    ab = _dot(h, wab_ref[...])

    cos, sin = cos_ref[...], sin_ref[...]
    if prompt:
        cos, sin = jnp.concatenate([cos] * nchunk, axis=0), jnp.concatenate([sin] * nchunk, axis=0)
    for hh in range(HEADS):
        sl = head_sl(hh)
        q = proj_s[:, OFF_RQ + hh * HEAD_DIM:OFF_RQ + (hh + 1) * HEAD_DIM]
        qr_s[:, sl] = q * cos + pltpu.roll(q, HEAD_DIM // 2, 1) * sin
        k = proj_s[:, OFF_RK + hh * HEAD_DIM:OFF_RK + (hh + 1) * HEAD_DIM]
        kr_s[:, sl] = (k * cos + pltpu.roll(k, HEAD_DIM // 2, 1) * sin) * (HEAD_DIM ** -0.5)

    t_in_group = lax.broadcasted_iota(jnp.int32, (lb, 1), 0) % group
    for part, dst in enumerate((qg_s, kg_s, vg_s)):
        cols = slice(OFF_CONV + part * GROUP_W, OFF_CONV + (part + 1) * GROUP_W)
        xq = proj_s[:, cols]
        cw = convw_ref[:, part * GROUP_W:(part + 1) * GROUP_W]
        acc = xq * cw[CONV_W - 1:CONV_W]
        if prompt:
            prev = prev8_s[:, part * GROUP_W:(part + 1) * GROUP_W]
            span = 8 + ROWS
            ext = jnp.concatenate([p for c in chunks for p in (prev[c * 8:(c + 1) * 8], xq[rows_of(c)])], axis=0)
            for s in range(1, CONV_W):
                rolled = pltpu.roll(ext, s, 0)
                sh = jnp.concatenate([rolled[c * span + 8:(c + 1) * span] for c in chunks], axis=0)
                acc = acc + sh * cw[CONV_W - 1 - s:CONV_W - s]
        else:
            st8 = st8_ref[:, part * GROUP_W:(part + 1) * GROUP_W]
            for s in range(1, CONV_W):
                sh = jnp.where(t_in_group < s, pltpu.roll(st8, lb - 8 + s, 0), pltpu.roll(xq, s, 0))
                acc = acc + sh * cw[CONV_W - 1 - s:CONV_W - s]
        conv = _silu(acc)
        for hh in range(HEADS):
            sl = head_sl(hh)
            c = conv[:, sl]
            if part < 2:
                c = c * lax.rsqrt(jnp.sum(c * c, axis=-1, keepdims=True) + EPS)
            if part == 0:
                c = c * (HEAD_DIM ** -0.5)
            dst[:, sl] = c
    if prompt:
        tail = jnp.concatenate([proj_s[(c + 1) * ROWS - 8:(c + 1) * ROWS, OFF_CONV:OFF_CONV + CONV_CH]
                                for c in chunks], axis=0)
        prev8_s[...] = tail
        conv_out[...] = tail
    else:
        conv_out[...] = proj_s[:, OFF_CONV:OFF_CONV + CONV_CH]

    pv = pvec_ref[...]
    a_plus = ab + pv[1:2]
    softplus = jnp.maximum(a_plus, 0.0) + jnp.log1p(jnp.exp(-jnp.abs(a_plus)))
    g = -jnp.exp(pv[0:1]) * softplus
    beta_s[...] = 1.0 / (1.0 + jnp.exp(-ab))

    row = lax.broadcasted_iota(jnp.int32, (ROWS, ROWS), 0)
    col = lax.broadcasted_iota(jnp.int32, (ROWS, ROWS), 1)
    same = (row // group) == (col // group)
    causal = same & (row >= col)
    strict = same & (row > col)
    sum_mat = _bf(jnp.concatenate([causal.astype(F32), same.astype(F32)], axis=0))
    sum_mat3 = jnp.concatenate([sum_mat] * 3, axis=1)
    for c in chunks:
        r = _dot(sum_mat3, jnp.concatenate(_split(g[rows_of(c)], 3), axis=0))
        cum_s[rows_of(c), :] = r[:ROWS]
        tot_s[rows_of(c), :] = r[ROWS:]
    sel = _bf((lax.broadcasted_iota(jnp.int32, (8, LANES), 0)
               == lax.broadcasted_iota(jnp.int32, (8, LANES), 1)).astype(F32))
    sel3 = jnp.concatenate([sel] * 3, axis=1)

    items = [(c, hh) for c in chunks for hh in range(HEADS)]
    cumc = [cum_s[rows_of(c), :] for c in chunks]
    totc = [tot_s[rows_of(c), :] for c in chunks]
    betac = [beta_s[rows_of(c), :] for c in chunks]
    ecumc = [jnp.exp(x) for x in cumc]
    tailc = [jnp.exp(t - x) for t, x in zip(totc, cumc)]
    cumrow = [_dot_nt(sel3, jnp.concatenate(_split(x, 3), axis=1)) for x in cumc]

    for c, hh in items:
        rows, sl = rows_of(c), head_sl(hh)
        q, k = qr_s[rows, sl], kr_s[rows, sl]
        pr_s[c * HEADS + hh] = _mm(q, k, nt=True) * dmat_ref[hh]
        qr_s[rows, sl] = q * rtab_ref[0, :, sl]
        kr_s[rows, sl] = k * rtab_ref[1, :, sl]

    qk_kk, dmats = [], []
    for c, hh in items:
        rows, sl = rows_of(c), head_sl(hh)
        kb = _bf(kg_s[rows, sl])
        qk_kk.append(_dot_nt(jnp.concatenate([_bf(qg_s[rows, sl]), kb], axis=0), kb))
        diff = lane(cumc[c], hh) - cumrow[c][hh:hh + 1, :]
        dmats.append(jnp.where(causal, jnp.exp(jnp.minimum(diff, 0.0)), 0.0))
    a_mats = [jnp.where(strict, lane(betac[c], HEADS + hh) * x[ROWS:] * d, 0.0)
              for (c, hh), x, d in zip(items, qk_kk, dmats)]
    t_inv, eye = _tri_inverse_many(a_mats, group, row, col)
    for i, (c, hh) in enumerate(items):
        rows, sl = rows_of(c), head_sl(hh)
        q, k, v = qg_s[rows, sl], kg_s[rows, sl], vg_s[rows, sl]
        beta_h = lane(betac[c], HEADS + hh)
        rhs = jnp.concatenate([k * (beta_h * lane(ecumc[c], hh)), v * beta_h], axis=1)
        sol = rhs + _mm(t_inv[i] - eye, rhs)
        w_s[rows, sl] = sol[:, :HEAD_DIM]
        u_s[rows, sl] = sol[:, HEAD_DIM:]
        pg_s[c * HEADS + hh] = qk_kk[i][:ROWS] * dmats[i]
        qg_s[rows, sl] = q * lane(ecumc[c], hh)
        kg_s[rows, sl] = k * lane(tailc[c], hh)

    def state_refs(c, hh, gi):
        if prompt:
            return (sret_s.at[c, hh], sret_s.at[c, hh], sgdn_s.at[c, hh], sgdn_s.at[c, hh])
        n = c * ngroups + gi
        return (sret_in.at[n, hh], sret_out.at[n, hh], sgdn_in.at[n, hh], sgdn_out.at[n, hh])

    for c in chunks:
        dtotc = jnp.exp(totc[c])
        qs_r, qs_g, ws_g = {}, {}, {}
        for hh in range(HEADS):
            sl = head_sl(hh)
            for gi in range(ngroups):
                rs = slice(c * ROWS + gi * group, c * ROWS + (gi + 1) * group)
                r_in, _, g_in, _ = state_refs(c, hh, gi)
                qs_r[hh, gi] = _mm(qr_s[rs, sl], r_in[...])
                r = _mm(jnp.concatenate([qg_s[rs, sl], w_s[rs, sl]], axis=0), g_in[...])
                qs_g[hh, gi], ws_g[hh, gi] = r[:group], r[group:]
        for hh in range(HEADS):
            rows, sl = rows_of(c), head_sl(hh)
            i = c * HEADS + hh
            vb = _bf(proj_s[rows, OFF_RV + hh * HEAD_DIM:OFF_RV + (hh + 1) * HEAD_DIM])
            v_new = u_s[rows, sl] - cat([ws_g[hh, gi] for gi in range(ngroups)])
            vnb = _bf(v_new)
            o_s[rows, sl] = cat([qs_r[hh, gi] for gi in range(ngroups)]) + _mm(pr_s[i], vb)
            o_s[rows, GROUP_W + hh * HEAD_DIM:GROUP_W + (hh + 1) * HEAD_DIM] = (
                cat([qs_g[hh, gi] for gi in range(ngroups)]) + _mm(pg_s[i], vnb))
            for gi in range(ngroups):
                rs = slice(c * ROWS + gi * group, c * ROWS + (gi + 1) * group)
                ls = slice(gi * group, (gi + 1) * group)
                r_in, r_out, g_in, g_out = state_refs(c, hh, gi)
                r_out[...] = cfg["ret_dtot"][hh] * r_in[...] + _mm(kr_s[rs, sl].T, vb[ls])
                dec = dtotc[gi * group:gi * group + 1, hh:hh + 1]
                g_out[...] = dec * g_in[...] + _mm(kg_s[rs, sl].T, vnb[ls])

    for hh in range(2 * HEADS):
        sl = head_sl(hh)
        goff = OFF_RG + hh * HEAD_DIM if hh < HEADS else OFF_Z + (hh - HEADS) * HEAD_DIM
        y = _rms(o_s[:, sl], normw_ref[:, sl]) * _silu(proj_s[:, goff:goff + HEAD_DIM])
        y = y.astype(mix_ref.dtype)
        if prompt:
            mix_ref[:, :, sl] = y.reshape(nchunk, ROWS, HEAD_DIM)
        else:
            mix_ref[:, sl] = y

    if prompt:
        @pl.when(pl.program_id(0) == pl.num_programs(0) - 1)
        def _():
            sret_out[...] = sret_s[...]
            sgdn_out[...] = sgdn_s[...]


def _post_kernel(mix_ref, x_ref, wout_ref, postw_ref, premlpw_ref, wup_ref, wdown_ref, postmlpw_ref, out_ref):
    tm = x_ref.shape[0]
    halves = [slice(0, tm // 2), slice(tm // 2, tm)] if tm % 16 == 0 else [slice(0, tm)]
    m = [_dot(mix_ref[r, :], wout_ref[...]) for r in halves]
    x1 = [x_ref[r, :] + _rms(mi, postw_ref[...]) for r, mi in zip(halves, m)]
    h2 = [_bf(_rms(xi, premlpw_ref[...])) for xi in x1]
    acc = [None] * len(halves)
    for c in range(D_FF // D_MODEL):
        for i in range(len(halves)):
            f = _dot(h2[i], wup_ref[:, c * D_MODEL:(c + 1) * D_MODEL])
            f = jnp.square(jnp.maximum(f, 0.0))
            part = _dot(_bf(f), wdown_ref[c * D_MODEL:(c + 1) * D_MODEL, :])
            acc[i] = part if acc[i] is None else acc[i] + part
    for r, xi, ai in zip(halves, x1, acc):
        out_ref[r, :] = xi + _rms(ai, postmlpw_ref[...])


def _const_spec(shape, single=False):
    index_map = lambda *_: (0,) * len(shape)
    if single:
        return pl.BlockSpec(shape, index_map, pipeline_mode=pl.Buffered(1))
    return pl.BlockSpec(shape, index_map)


def _retention_tables(group):
    log_gamma = np.log(1.0 - 2.0 ** (-5.0 - np.arange(HEADS, dtype=np.float64)))
    t = np.arange(ROWS) % group
    same = (np.arange(ROWS)[:, None] // group) == (np.arange(ROWS)[None, :] // group)
    causal = same & (t[:, None] >= t[None, :])
    dmat = np.where(causal[None], np.exp((t[:, None] - t[None, :])[None] * log_gamma[:, None, None]), 0.0)
    ecum = np.exp((t[:, None] + 1.0) * log_gamma[None, :])
    tail = np.exp((group - 1.0 - t[:, None]) * log_gamma[None, :])
    rtab = np.stack([np.repeat(ecum, HEAD_DIM, axis=1), np.repeat(tail, HEAD_DIM, axis=1)])
    dtot = tuple(float(np.float32(v)) for v in np.exp(group * log_gamma))
    return jnp.asarray(dmat, F32), jnp.asarray(rtab, F32), dtot


def _rope_tables(length, offset):
    inv_freq = ROPE_BASE ** (-jnp.arange(0, HEAD_DIM, 2, dtype=F32) / HEAD_DIM)
    pos = jnp.arange(length, dtype=F32) + jnp.asarray(offset, F32)
    ang = pos[:, None] * inv_freq[None, :]
    cos, sin = jnp.cos(ang), jnp.sin(ang)
    return jnp.concatenate([cos, cos], axis=-1), jnp.concatenate([-sin, sin], axis=-1)


def _mixer(x, params, *, prompt, states=None):
    b, l, _ = x.shape
    t = b * l
    if prompt:
        group, lb = ROWS, b * ROWS
        assert l % ROWS == 0
        cos, sin = _rope_tables(l, 0)
        grid = (l // ROWS,)
        x_in = x
        x_spec = pl.BlockSpec((b, ROWS, D_MODEL), lambda j: (0, j, 0))
        pos_spec = pl.BlockSpec((ROWS, HEAD_DIM), lambda j: (j, 0))
    else:
        group = l
        lb = min(2 * ROWS, t)
        assert ROWS % group == 0 and group == 8 and t % lb == 0 and lb % ROWS == 0
        cos, sin = _rope_tables(l, PAST_LEN)
        cos, sin = jnp.tile(cos, (lb // l, 1)), jnp.tile(sin, (lb // l, 1))
        grid = (t // lb,)
        row_map = lambda i: (i, 0)
        x_in = x.reshape(t, D_MODEL)
        x_spec = pl.BlockSpec((lb, D_MODEL), row_map)
        pos_spec = pl.BlockSpec((lb, HEAD_DIM), lambda i: (0, 0))
    nseq = lb // group
    dmat, rtab, dtot = _retention_tables(group)
    cfg = dict(lb=lb, group=group, prompt=prompt, ret_dtot=dtot)

    in_specs = [
        x_spec,
        _const_spec((1, D_MODEL)),
        _const_spec((D_MODEL, IN_W), single=True),
        _const_spec((D_MODEL, LANES), single=True),
        pos_spec,
        pos_spec,
        _const_spec((HEADS, ROWS, ROWS)),
        _const_spec((2, ROWS, GROUP_W)),
        _const_spec((CONV_W, CONV_CH)),
        _const_spec((8, LANES)),
        _const_spec((1, 2 * GROUP_W)),
    ]
    args = [x_in, params["pre_w"], params["w_in"], params["w_ab"], cos, sin, dmat, rtab,
            params["conv_w"], params["pvec"], params["normw"]]
    state_shape = (HEADS, HEAD_DIM, HEAD_DIM)
    if prompt:
        out_specs = [
            pl.BlockSpec((b, ROWS, 2 * GROUP_W), lambda j: (0, j, 0)),
            _const_spec((b,) + state_shape),
            _const_spec((b,) + state_shape),
            _const_spec((b * 8, CONV_CH)),
        ]
        out_shape = [
            jax.ShapeDtypeStruct((b, l, 2 * GROUP_W), BF16),
            jax.ShapeDtypeStruct((b,) + state_shape, F32),
            jax.ShapeDtypeStruct((b,) + state_shape, F32),
            jax.ShapeDtypeStruct((b * 8, CONV_CH), F32),
        ]
    else:
        s_ret, s_gdn, s_conv = states
        st8 = jnp.pad(s_conv.astype(F32), ((0, 0), (8 - (CONV_W - 1), 0), (0, 0))).reshape(t, CONV_CH)
        in_specs += [
            pl.BlockSpec((nseq,) + state_shape, lambda i: (i, 0, 0, 0)),
            pl.BlockSpec((nseq,) + state_shape, lambda i: (i, 0, 0, 0)),
            pl.BlockSpec((lb, CONV_CH), row_map),
        ]
        args += [s_ret.astype(F32), s_gdn.astype(F32), st8]
        out_specs = [
            pl.BlockSpec((lb, 2 * GROUP_W), row_map),
            pl.BlockSpec((nseq,) + state_shape, lambda i: (i, 0, 0, 0)),
            pl.BlockSpec((nseq,) + state_shape, lambda i: (i, 0, 0, 0)),
            pl.BlockSpec((lb, CONV_CH), row_map),
        ]
        out_shape = [
            jax.ShapeDtypeStruct((t, 2 * GROUP_W), BF16),
            jax.ShapeDtypeStruct((b,) + state_shape, F32),
            jax.ShapeDtypeStruct((b,) + state_shape, F32),
            jax.ShapeDtypeStruct((t, CONV_CH), F32),
        ]
    nmat = lb // ROWS * HEADS
    scratch = [
        pltpu.VMEM((lb, MAIN_W), F32),
        pltpu.VMEM((lb, GROUP_W), F32), pltpu.VMEM((lb, GROUP_W), F32),
        pltpu.VMEM((lb, GROUP_W), F32), pltpu.VMEM((lb, GROUP_W), F32), pltpu.VMEM((lb, GROUP_W), F32),
        pltpu.VMEM((lb, LANES), F32), pltpu.VMEM((lb, LANES), F32), pltpu.VMEM((lb, LANES), F32),
        pltpu.VMEM((lb, 2 * GROUP_W), F32),
        pltpu.VMEM((lb, GROUP_W), F32), pltpu.VMEM((lb, GROUP_W), F32),
        pltpu.VMEM((nmat, ROWS, ROWS), F32), pltpu.VMEM((nmat, ROWS, ROWS), F32),
    ]
    if prompt:
        scratch += [pltpu.VMEM((b,) + state_shape, F32), pltpu.VMEM((b,) + state_shape, F32),
                    pltpu.VMEM((b * 8, CONV_CH), F32)]

    mix, s_ret_new, s_gdn_new, conv_rows = pl.pallas_call(
        functools.partial(_mixer_kernel, cfg),
        grid=grid,
        in_specs=in_specs,
        out_specs=out_specs,
        out_shape=out_shape,
        scratch_shapes=scratch,
        compiler_params=pltpu.CompilerParams(dimension_semantics=("arbitrary",), vmem_limit_bytes=VMEM_LIMIT),
        name="mixer_prompt" if prompt else "mixer_sample",
    )(*args)
    new_conv = conv_rows.reshape(b, 8, CONV_CH)[:, 8 - (CONV_W - 1):]
    return mix.reshape(t, 2 * GROUP_W), s_ret_new, s_gdn_new, new_conv


def _post(mix, x, params):
    b, l, _ = x.shape
    t = b * l
    tm = min(512, t)
    assert t % tm == 0
    rows = pl.BlockSpec((tm, D_MODEL), lambda i: (i, 0))
    y = pl.pallas_call(
        _post_kernel,
        grid=(t // tm,),
        in_specs=[rows, rows, _const_spec((2 * GROUP_W, D_MODEL)), _const_spec((1, D_MODEL)),
                  _const_spec((1, D_MODEL)), _const_spec((D_MODEL, D_FF)), _const_spec((D_FF, D_MODEL)),
                  _const_spec((1, D_MODEL))],
        out_specs=rows,
        out_shape=jax.ShapeDtypeStruct((t, D_MODEL), F32),
        compiler_params=pltpu.CompilerParams(dimension_semantics=("arbitrary",), vmem_limit_bytes=VMEM_LIMIT),
        name="post",
    )(mix, x.reshape(t, D_MODEL), params["w_out"], params["post_w"], params["pre_mlp_w"],
      params["w_up"], params["w_down"], params["post_mlp_w"])
    return y.reshape(b, l, D_MODEL)


def _layer_params(l, pre_mix_w, w_in, conv_w, A_log, dt_bias, ret_norm_w, gdn_norm_w, w_out, post_mix_w,
                  pre_mlp_w, w_up, w_down, post_mlp_w):
    w = _bf(w_in[l])
    pvec = jnp.zeros((8, LANES), F32)
    pvec = pvec.at[0, :HEADS].set(A_log[l].astype(F32)).at[1, :HEADS].set(dt_bias[l].astype(F32))
    return dict(
        pre_w=pre_mix_w[l].astype(F32)[None],
        w_in=w,
        w_ab=jnp.pad(w[:, MAIN_W:], ((0, 0), (0, LANES - 2 * HEADS))),
        conv_w=conv_w[l].astype(F32),
        pvec=pvec,
        normw=jnp.concatenate([jnp.tile(ret_norm_w[l], HEADS), jnp.tile(gdn_norm_w[l], HEADS)]).astype(F32)[None],
        w_out=_bf(w_out[l]),
        post_w=post_mix_w[l].astype(F32)[None],
        pre_mlp_w=pre_mlp_w[l].astype(F32)[None],
        w_up=_bf(w_up[l]),
        w_down=_bf(w_down[l]),
        post_mlp_w=post_mlp_w[l].astype(F32)[None],
    )


def kernel(x_prompt, x_sample, state_ret, state_gdn, state_conv, pre_mix_w, w_in, conv_w, A_log, dt_bias,
           ret_norm_w, gdn_norm_w, w_out, post_mix_w, pre_mlp_w, w_up, w_down, post_mlp_w):
    depth = w_in.shape[0]
    yp, ys = x_prompt, x_sample
    outs = [[] for _ in range(6)]
    for l in range(depth):
        params = _layer_params(l, pre_mix_w, w_in, conv_w, A_log, dt_bias, ret_norm_w, gdn_norm_w, w_out,
                               post_mix_w, pre_mlp_w, w_up, w_down, post_mlp_w)
        mix_p, rp, gp, cp = _mixer(yp, params, prompt=True)
        yp = _post(mix_p, yp, params)
        mix_s, rs, gs, cs = _mixer(ys, params, prompt=False,
                                   states=(state_ret[l], state_gdn[l], state_conv[l]))
        ys = _post(mix_s, ys, params)
        for dst, val, like in zip(outs, (rp, gp, cp, rs, gs, cs),
                                  (state_ret, state_gdn, state_conv) * 2):
            dst.append(val.astype(like.dtype))
    return (yp, ys) + tuple(jnp.stack(o) for o in outs)
```

```python
import functools

import numpy as np
import jax
import jax.numpy as jnp
from jax import lax
from jax.experimental import pallas as pl
from jax.experimental.pallas import tpu as pltpu

D_MODEL = 1024
HEADS = 4
HEAD_DIM = 128
GROUP_W = HEADS * HEAD_DIM
CONV_CH = 3 * GROUP_W
CONV_W = 4
D_FF = 4 * D_MODEL
MAIN_W = 4 * GROUP_W + CONV_CH + GROUP_W
IN_W = MAIN_W + 2 * HEADS
ROPE_BASE = 10000.0
EPS = 1e-6
PAST_LEN = 16384

ROWS = 64
BASE = 8
LANES = 128
TILE = 256
VMEM_LIMIT = 58 * 1024 * 1024

OFF_RQ, OFF_RK, OFF_RV, OFF_RG = 0, GROUP_W, 2 * GROUP_W, 3 * GROUP_W
OFF_CONV = 4 * GROUP_W
OFF_Z = OFF_CONV + CONV_CH

F32 = jnp.float32
BF16 = jnp.bfloat16


def _bf(x):
    return x.astype(BF16)


def _dot(a, b):
    return jnp.dot(a, b, preferred_element_type=F32)


def _dot_nt(a, b):
    return lax.dot_general(a, b, (((1,), (1,)), ((), ())), preferred_element_type=F32)


def _split(x, n):
    parts, r = [], x
    for i in range(n):
        p = r.astype(BF16)
        parts.append(p)
        if i + 1 < n:
            r = r - p.astype(F32)
    return parts


def _mm(a, b, nt=False):
    return _dot_nt(_bf(a), _bf(b)) if nt else _dot(_bf(a), _bf(b))


def _mm3(a, b):
    (a_hi, a_lo), (b_hi, b_lo) = _split(a, 2), _split(b, 2)
    n = a.shape[0]
    both = _dot(jnp.concatenate([a_hi, a_lo], axis=0), b_hi)
    return both[:n] + both[n:] + _dot(a_hi, b_lo)


def _rms(x, w):
    return x * lax.rsqrt(jnp.mean(x * x, axis=-1, keepdims=True) + EPS) * w


def _silu(x):
    return x / (1.0 + jnp.exp(-x))


def _tri_inverse_many(mats, group, row, col):
    n = mats[0].shape[0]
    packed = [jnp.concatenate(mats[i:i + HEADS], axis=1) for i in range(0, len(mats), HEADS)]
    prow = lax.broadcasted_iota(jnp.int32, (n, HEADS * n), 0)
    plane = lax.broadcasted_iota(jnp.int32, (n, HEADS * n), 1)
    pcol = plane % n
    seg = [_bf(((plane // n) == hh).astype(F32)) for hh in range(HEADS)]

    def blockdiag(y):
        return jnp.concatenate([y * m for m in seg], axis=0)

    def mm3(x, y):
        (x_hi, x_lo), (y_hi, y_lo) = _split(x, 2), _split(y, 2)
        both = _dot(jnp.concatenate([x_hi, x_lo], axis=0), blockdiag(y_hi))
        return both[:n] + both[n:] + _dot(x_hi, blockdiag(y_lo))

    eye_p = (prow == pcol).astype(F32)
    blk = (prow // BASE) == (pcol // BASE)
    n8 = [jnp.where(blk, -a, 0.0) for a in packed]
    p2 = [mm3(x, x) for x in n8]
    np2 = [mm3(x, p) for x, p in zip(n8, p2)]
    p4 = [mm3(p, p) for p in p2]
    e = [x + p + y for x, p, y in zip(n8, p2, np2)]
    ep4 = [mm3(x, p) for x, p in zip(e, p4)]
    t = [x + p + y + eye_p for x, p, y in zip(e, p4, ep4)]
    s = BASE
    while s < group:
        off = ((prow // (2 * s)) == (pcol // (2 * s))) & ((prow // s) != (pcol // s))
        a_off = [jnp.where(off, a, 0.0) for a in packed]
        ta = [mm3(x, y) for x, y in zip(t, a_off)]
        tat = [mm3(x, y) for x, y in zip(ta, t)]
        t = [x - y for x, y in zip(t, tat)]
        s *= 2
    eye = (row == col).astype(F32)
    return [tp[:, hh * n:(hh + 1) * n] for tp in t for hh in range(HEADS)], eye


def _mixer_kernel(cfg, *refs):
    lb, group, prompt = cfg["lb"], cfg["group"], cfg["prompt"]
    ngroups = ROWS // group
    nchunk = lb // ROWS
    chunks = list(range(nchunk))
    it = iter(refs)
    x_ref, prew_ref, win_ref, wab_ref, cos_ref, sin_ref = (next(it) for _ in range(6))
    dmat_ref, rtab_ref, convw_ref, pvec_ref, normw_ref = (next(it) for _ in range(5))
    if not prompt:
        sret_in, sgdn_in, st8_ref = (next(it) for _ in range(3))
    mix_ref, sret_out, sgdn_out, conv_out = (next(it) for _ in range(4))
    qr_s, kr_s, qg_s, kg_s, vg_s, cum_s, tot_s, beta_s, o_s = (next(it) for _ in range(9))
    w_s, u_s, pr_s, pg_s, vb_s, gate_s = (next(it) for _ in range(6))
    if prompt:
        sret_s, sgdn_s, ext_s = (next(it) for _ in range(3))

        @pl.when(pl.program_id(0) == 0)
        def _():
            sret_s[...] = jnp.zeros_like(sret_s)
            sgdn_s[...] = jnp.zeros_like(sgdn_s)
            ext_s[:, 0:8, :] = jnp.zeros((nchunk, 8, CONV_CH), F32)

    rows_of = lambda c: slice(c * ROWS, (c + 1) * ROWS)
    head_sl = lambda hh: slice(hh * HEAD_DIM, (hh + 1) * HEAD_DIM)
    lane = lambda x, n: x[:, n:n + 1]
    cat = lambda xs: xs[0] if len(xs) == 1 else jnp.concatenate(xs, axis=0)

    x = x_ref[...].reshape(lb, D_MODEL)
    h = _bf(_rms(x, prew_ref[...]))
    ab = _dot(h, wab_ref[...])
    cos, sin = cos_ref[...], sin_ref[...]
    if prompt:
        cos, sin = jnp.concatenate([cos] * nchunk, axis=0), jnp.concatenate([sin] * nchunk, axis=0)
    t_in_group = lax.broadcasted_iota(jnp.int32, (lb, 1), 0) % group

    for c0 in range(0, MAIN_W, TILE):
        res = _dot(h, win_ref[:, c0:c0 + TILE])
        tiles = [res[:, j * HEAD_DIM:(j + 1) * HEAD_DIM] for j in range(TILE // HEAD_DIM)]
        if c0 < OFF_RV:
            for j, tl in enumerate(tiles):
                rot = tl * cos + pltpu.roll(tl, HEAD_DIM // 2, 1) * sin
                if c0 < OFF_RK:
                    qr_s[:, c0 - OFF_RQ + j * HEAD_DIM:c0 - OFF_RQ + (j + 1) * HEAD_DIM] = rot
                else:
                    kr_s[:, c0 - OFF_RK + j * HEAD_DIM:c0 - OFF_RK + (j + 1) * HEAD_DIM] = rot * (HEAD_DIM ** -0.5)
        elif c0 < OFF_RG:
            vb_s[:, c0 - OFF_RV:c0 - OFF_RV + TILE] = _bf(res)
        elif c0 < OFF_CONV:
            gate_s[:, c0 - OFF_RG:c0 - OFF_RG + TILE] = _silu(res)
        elif c0 >= OFF_Z:
            gate_s[:, GROUP_W + c0 - OFF_Z:GROUP_W + c0 - OFF_Z + TILE] = _silu(res)
        else:
            cc = c0 - OFF_CONV
            part = cc // GROUP_W
            dst = (qg_s, kg_s, vg_s)[part]
            cw = convw_ref[:, cc:cc + TILE]
            acc = res * cw[CONV_W - 1:CONV_W]
            if prompt:
                for c in chunks:
                    ext_s[c, 8:, cc:cc + TILE] = res[rows_of(c)]
                for s in range(1, CONV_W):
                    sh = jnp.concatenate([ext_s[c, 8 - s:8 - s + ROWS, cc:cc + TILE] for c in chunks], axis=0)
                    acc = acc + sh * cw[CONV_W - 1 - s:CONV_W - s]
                for c in chunks:
                    tail = ext_s[c, ROWS:, cc:cc + TILE]
                    ext_s[c, 0:8, cc:cc + TILE] = tail
                    conv_out[c * 8:(c + 1) * 8, cc:cc + TILE] = tail
            else:
                st8 = st8_ref[:, cc:cc + TILE]
                for s in range(1, CONV_W):
                    sh = jnp.where(t_in_group < s, pltpu.roll(st8, lb - 8 + s, 0), pltpu.roll(res, s, 0))
                    acc = acc + sh * cw[CONV_W - 1 - s:CONV_W - s]
                conv_out[:, cc:cc + TILE] = res
            conv = _silu(acc)
            for j in range(TILE // HEAD_DIM):
                c = conv[:, j * HEAD_DIM:(j + 1) * HEAD_DIM]
                if part < 2:
                    c = c * lax.rsqrt(jnp.sum(c * c, axis=-1, keepdims=True) + EPS)
                if part == 0:
                    c = c * (HEAD_DIM ** -0.5)
                d0 = cc - part * GROUP_W + j * HEAD_DIM
                dst[:, d0:d0 + HEAD_DIM] = c

    pv = pvec_ref[...]
    a_plus = ab + pv[1:2]
    softplus = jnp.maximum(a_plus, 0.0) + jnp.log1p(jnp.exp(-jnp.abs(a_plus)))
    g = -jnp.exp(pv[0:1]) * softplus
    beta_s[...] = 1.0 / (1.0 + jnp.exp(-ab))

    row = lax.broadcasted_iota(jnp.int32, (ROWS, ROWS), 0)
    col = lax.broadcasted_iota(jnp.int32, (ROWS, ROWS), 1)
    same = (row // group) == (col // group)
    causal = same & (row >= col)
    strict = same & (row > col)
    sum_mat = _bf(jnp.concatenate([causal.astype(F32), same.astype(F32)], axis=0))
    sum_mat3 = jnp.concatenate([sum_mat] * 3, axis=1)
    for c in chunks:
        r = _dot(sum_mat3, jnp.concatenate(_split(g[rows_of(c)], 3), axis=0))
        cum_s[rows_of(c), :] = r[:ROWS]
        tot_s[rows_of(c), :] = r[ROWS:]
    sel = _bf((lax.broadcasted_iota(jnp.int32, (8, LANES), 0)
               == lax.broadcasted_iota(jnp.int32, (8, LANES), 1)).astype(F32))
    sel3 = jnp.concatenate([sel] * 3, axis=1)

    items = [(c, hh) for c in chunks for hh in range(HEADS)]
    cumc = [cum_s[rows_of(c), :] for c in chunks]
    totc = [tot_s[rows_of(c), :] for c in chunks]
    betac = [beta_s[rows_of(c), :] for c in chunks]
    ecumc = [jnp.exp(x) for x in cumc]
    tailc = [jnp.exp(t - x) for t, x in zip(totc, cumc)]
    cumrow = [_dot_nt(sel3, jnp.concatenate(_split(x, 3), axis=1)) for x in cumc]

    for c, hh in items:
        rows, sl = rows_of(c), head_sl(hh)
        q, k = qr_s[rows, sl], kr_s[rows, sl]
        pr_s[c * HEADS + hh] = _mm(q, k, nt=True) * dmat_ref[hh]
        qr_s[rows, sl] = q * rtab_ref[0, :, sl]
        kr_s[rows, sl] = k * rtab_ref[1, :, sl]

    qk_kk, dmats = [], []
    for c, hh in items:
        rows, sl = rows_of(c), head_sl(hh)
        kb = _bf(kg_s[rows, sl])
        qk_kk.append(_dot_nt(jnp.concatenate([_bf(qg_s[rows, sl]), kb], axis=0), kb))
        diff = lane(cumc[c], hh) - cumrow[c][hh:hh + 1, :]
        dmats.append(jnp.where(causal, jnp.exp(jnp.minimum(diff, 0.0)), 0.0))
    a_mats = [jnp.where(strict, lane(betac[c], HEADS + hh) * x[ROWS:] * d, 0.0)
              for (c, hh), x, d in zip(items, qk_kk, dmats)]
    t_inv, eye = _tri_inverse_many(a_mats, group, row, col)
    for i, (c, hh) in enumerate(items):
        rows, sl = rows_of(c), head_sl(hh)
        q, k, v = qg_s[rows, sl], kg_s[rows, sl], vg_s[rows, sl]
        beta_h = lane(betac[c], HEADS + hh)
        rhs = jnp.concatenate([k * (beta_h * lane(ecumc[c], hh)), v * beta_h], axis=1)
        sol = rhs + _mm(t_inv[i] - eye, rhs)
        w_s[rows, sl] = sol[:, :HEAD_DIM]
        u_s[rows, sl] = sol[:, HEAD_DIM:]
        pg_s[c * HEADS + hh] = qk_kk[i][:ROWS] * dmats[i]
        qg_s[rows, sl] = q * lane(ecumc[c], hh)
        kg_s[rows, sl] = k * lane(tailc[c], hh)

    def state_refs(c, hh, gi):
        if prompt:
            return (sret_s.at[c, hh], sret_s.at[c, hh], sgdn_s.at[c, hh], sgdn_s.at[c, hh])
        n = c * ngroups + gi
        return (sret_in.at[n, hh], sret_out.at[n, hh], sgdn_in.at[n, hh], sgdn_out.at[n, hh])

    for c in chunks:
        dtotc = jnp.exp(totc[c])
        qs_r, qs_g, ws_g = {}, {}, {}
        for hh in range(HEADS):
            sl = head_sl(hh)
            for gi in range(ngroups):
                rs = slice(c * ROWS + gi * group, c * ROWS + (gi + 1) * group)
                r_in, _, g_in, _ = state_refs(c, hh, gi)
                qs_r[hh, gi] = _mm(qr_s[rs, sl], r_in[...])
                r = _mm(jnp.concatenate([qg_s[rs, sl], w_s[rs, sl]], axis=0), g_in[...])
                qs_g[hh, gi], ws_g[hh, gi] = r[:group], r[group:]
        for hh in range(HEADS):
            rows, sl = rows_of(c), head_sl(hh)
            i = c * HEADS + hh
            vb = vb_s[rows, sl]
            v_new = u_s[rows, sl] - cat([ws_g[hh, gi] for gi in range(ngroups)])
            vnb = _bf(v_new)
            o_s[rows, sl] = cat([qs_r[hh, gi] for gi in range(ngroups)]) + _mm(pr_s[i], vb)
            o_s[rows, GROUP_W + hh * HEAD_DIM:GROUP_W + (hh + 1) * HEAD_DIM] = (
                cat([qs_g[hh, gi] for gi in range(ngroups)]) + _mm(pg_s[i], vnb))
            for gi in range(ngroups):
                rs = slice(c * ROWS + gi * group, c * ROWS + (gi + 1) * group)
                ls = slice(gi * group, (gi + 1) * group)
                r_in, r_out, g_in, g_out = state_refs(c, hh, gi)
                r_out[...] = cfg["ret_dtot"][hh] * r_in[...] + _mm(kr_s[rs, sl].T, vb[ls])
                dec = dtotc[gi * group:gi * group + 1, hh:hh + 1]
                g_out[...] = dec * g_in[...] + _mm(kg_s[rs, sl].T, vnb[ls])

    for hh in range(2 * HEADS):
        sl = head_sl(hh)
        y = _rms(o_s[:, sl], normw_ref[:, sl]) * gate_s[:, sl]
        y = y.astype(mix_ref.dtype)
        if prompt:
            mix_ref[:, :, sl] = y.reshape(nchunk, ROWS, HEAD_DIM)
        else:
            mix_ref[:, sl] = y

    if prompt:
        @pl.when(pl.program_id(0) == pl.num_programs(0) - 1)
        def _():
            sret_out[...] = sret_s[...]
            sgdn_out[...] = sgdn_s[...]


def _post_kernel(mix_ref, x_ref, wout_ref, postw_ref, premlpw_ref, wup_ref, wdown_ref, postmlpw_ref, out_ref):
    tm = x_ref.shape[0]
    halves = [slice(0, tm // 2), slice(tm // 2, tm)] if tm % 16 == 0 else [slice(0, tm)]
    m = [_dot(mix_ref[r, :], wout_ref[...]) for r in halves]
    x1 = [x_ref[r, :] + _rms(mi, postw_ref[...]) for r, mi in zip(halves, m)]
    h2 = [_bf(_rms(xi, premlpw_ref[...])) for xi in x1]
    acc = [None] * len(halves)
    for c in range(D_FF // D_MODEL):
        for i in range(len(halves)):
            f = _dot(h2[i], wup_ref[:, c * D_MODEL:(c + 1) * D_MODEL])
            f = jnp.square(jnp.maximum(f, 0.0))
            part = _dot(_bf(f), wdown_ref[c * D_MODEL:(c + 1) * D_MODEL, :])
            acc[i] = part if acc[i] is None else acc[i] + part
    for r, xi, ai in zip(halves, x1, acc):
        out_ref[r, :] = xi + _rms(ai, postmlpw_ref[...])


def _const_spec(shape, single=False):
    index_map = lambda *_: (0,) * len(shape)
    if single:
        return pl.BlockSpec(shape, index_map, pipeline_mode=pl.Buffered(1))
    return pl.BlockSpec(shape, index_map)


def _retention_tables(group):
    log_gamma = np.log(1.0 - 2.0 ** (-5.0 - np.arange(HEADS, dtype=np.float64)))
    t = np.arange(ROWS) % group
    same = (np.arange(ROWS)[:, None] // group) == (np.arange(ROWS)[None, :] // group)
    causal = same & (t[:, None] >= t[None, :])
    dmat = np.where(causal[None], np.exp((t[:, None] - t[None, :])[None] * log_gamma[:, None, None]), 0.0)
    ecum = np.exp((t[:, None] + 1.0) * log_gamma[None, :])
    tail = np.exp((group - 1.0 - t[:, None]) * log_gamma[None, :])
    rtab = np.stack([np.repeat(ecum, HEAD_DIM, axis=1), np.repeat(tail, HEAD_DIM, axis=1)])
    dtot = tuple(float(np.float32(v)) for v in np.exp(group * log_gamma))
    return jnp.asarray(dmat, F32), jnp.asarray(rtab, F32), dtot


def _rope_tables(length, offset):
    inv_freq = ROPE_BASE ** (-jnp.arange(0, HEAD_DIM, 2, dtype=F32) / HEAD_DIM)
    pos = jnp.arange(length, dtype=F32) + jnp.asarray(offset, F32)
    ang = pos[:, None] * inv_freq[None, :]
    cos, sin = jnp.cos(ang), jnp.sin(ang)
    return jnp.concatenate([cos, cos], axis=-1), jnp.concatenate([-sin, sin], axis=-1)


def _mixer(x, params, *, prompt, states=None):
    b, l, _ = x.shape
    t = b * l
    if prompt:
        group, lb = ROWS, b * ROWS
        assert l % ROWS == 0
        cos, sin = _rope_tables(l, 0)
        grid = (l // ROWS,)
        x_in = x
        x_spec = pl.BlockSpec((b, ROWS, D_MODEL), lambda j: (0, j, 0))
        pos_spec = pl.BlockSpec((ROWS, HEAD_DIM), lambda j: (j, 0))
    else:
        group = l
        lb = min(2 * ROWS, t)
        assert ROWS % group == 0 and group == 8 and t % lb == 0 and lb % ROWS == 0
        cos, sin = _rope_tables(l, PAST_LEN)
        cos, sin = jnp.tile(cos, (lb // l, 1)), jnp.tile(sin, (lb // l, 1))
        grid = (t // lb,)
        row_map = lambda i: (i, 0)
        x_in = x.reshape(t, D_MODEL)
        x_spec = pl.BlockSpec((lb, D_MODEL), row_map)
        pos_spec = pl.BlockSpec((lb, HEAD_DIM), lambda i: (0, 0))
    nseq = lb // group
    dmat, rtab, dtot = _retention_tables(group)
    cfg = dict(lb=lb, group=group, prompt=prompt, ret_dtot=dtot)

    in_specs = [
        x_spec,
        _const_spec((1, D_MODEL)),
        _const_spec((D_MODEL, IN_W), single=True),
        _const_spec((D_MODEL, LANES), single=True),
        pos_spec,
        pos_spec,
        _const_spec((HEADS, ROWS, ROWS)),
        _const_spec((2, ROWS, GROUP_W)),
        _const_spec((CONV_W, CONV_CH)),
        _const_spec((8, LANES)),
        _const_spec((1, 2 * GROUP_W)),
    ]
    args = [x_in, params["pre_w"], params["w_in"], params["w_ab"], cos, sin, dmat, rtab,
            params["conv_w"], params["pvec"], params["normw"]]
    state_shape = (HEADS, HEAD_DIM, HEAD_DIM)
    if prompt:
        out_specs = [
            pl.BlockSpec((b, ROWS, 2 * GROUP_W), lambda j: (0, j, 0)),
            _const_spec((b,) + state_shape),
            _const_spec((b,) + state_shape),
            _const_spec((b * 8, CONV_CH)),
        ]
        out_shape = [
            jax.ShapeDtypeStruct((b, l, 2 * GROUP_W), BF16),
            jax.ShapeDtypeStruct((b,) + state_shape, F32),
            jax.ShapeDtypeStruct((b,) + state_shape, F32),
            jax.ShapeDtypeStruct((b * 8, CONV_CH), F32),
        ]
    else:
        s_ret, s_gdn, s_conv = states
        st8 = jnp.pad(s_conv.astype(F32), ((0, 0), (8 - (CONV_W - 1), 0), (0, 0))).reshape(t, CONV_CH)
        in_specs += [
            pl.BlockSpec((nseq,) + state_shape, lambda i: (i, 0, 0, 0)),
            pl.BlockSpec((nseq,) + state_shape, lambda i: (i, 0, 0, 0)),
            pl.BlockSpec((lb, CONV_CH), row_map),
        ]
        args += [s_ret.astype(F32), s_gdn.astype(F32), st8]
        out_specs = [
            pl.BlockSpec((lb, 2 * GROUP_W), row_map),
            pl.BlockSpec((nseq,) + state_shape, lambda i: (i, 0, 0, 0)),
            pl.BlockSpec((nseq,) + state_shape, lambda i: (i, 0, 0, 0)),
            pl.BlockSpec((lb, CONV_CH), row_map),
        ]
        out_shape = [
            jax.ShapeDtypeStruct((t, 2 * GROUP_W), BF16),
            jax.ShapeDtypeStruct((b,) + state_shape, F32),
            jax.ShapeDtypeStruct((b,) + state_shape, F32),
            jax.ShapeDtypeStruct((t, CONV_CH), F32),
        ]
    nmat = lb // ROWS * HEADS
    scratch = [
        pltpu.VMEM((lb, GROUP_W), F32), pltpu.VMEM((lb, GROUP_W), F32),
        pltpu.VMEM((lb, GROUP_W), F32), pltpu.VMEM((lb, GROUP_W), F32), pltpu.VMEM((lb, GROUP_W), F32),
        pltpu.VMEM((lb, LANES), F32), pltpu.VMEM((lb, LANES), F32), pltpu.VMEM((lb, LANES), F32),
        pltpu.VMEM((lb, 2 * GROUP_W), F32),
        pltpu.VMEM((lb, GROUP_W), F32), pltpu.VMEM((lb, GROUP_W), F32),
        pltpu.VMEM((nmat, ROWS, ROWS), F32), pltpu.VMEM((nmat, ROWS, ROWS), F32),
        pltpu.VMEM((lb, GROUP_W), BF16), pltpu.VMEM((lb, 2 * GROUP_W), F32),
    ]
    if prompt:
        scratch += [pltpu.VMEM((b,) + state_shape, F32), pltpu.VMEM((b,) + state_shape, F32),
                    pltpu.VMEM((b, 8 + ROWS, CONV_CH), F32)]

    mix, s_ret_new, s_gdn_new, conv_rows = pl.pallas_call(
        functools.partial(_mixer_kernel, cfg),
        grid=grid,
        in_specs=in_specs,
        out_specs=out_specs,
        out_shape=out_shape,
        scratch_shapes=scratch,
        compiler_params=pltpu.CompilerParams(dimension_semantics=("arbitrary",), vmem_limit_bytes=VMEM_LIMIT),
        name="mixer_prompt" if prompt else "mixer_sample",
    )(*args)
    new_conv = conv_rows.reshape(b, 8, CONV_CH)[:, 8 - (CONV_W - 1):]
    return mix.reshape(t, 2 * GROUP_W), s_ret_new, s_gdn_new, new_conv


def _post(mix, x, params):
    b, l, _ = x.shape
    t = b * l
    tm = min(512, t)
    assert t % tm == 0
    rows = pl.BlockSpec((tm, D_MODEL), lambda i: (i, 0))
    y = pl.pallas_call(
        _post_kernel,
        grid=(t // tm,),
        in_specs=[rows, rows, _const_spec((2 * GROUP_W, D_MODEL)), _const_spec((1, D_MODEL)),
                  _const_spec((1, D_MODEL)), _const_spec((D_MODEL, D_FF)), _const_spec((D_FF, D_MODEL)),
                  _const_spec((1, D_MODEL))],
        out_specs=rows,
        out_shape=jax.ShapeDtypeStruct((t, D_MODEL), F32),
        compiler_params=pltpu.CompilerParams(dimension_semantics=("arbitrary",), vmem_limit_bytes=VMEM_LIMIT),
        name="post",
    )(mix, x.reshape(t, D_MODEL), params["w_out"], params["post_w"], params["pre_mlp_w"],
      params["w_up"], params["w_down"], params["post_mlp_w"])
    return y.reshape(b, l, D_MODEL)


def _layer_params(l, pre_mix_w, w_in, conv_w, A_log, dt_bias, ret_norm_w, gdn_norm_w, w_out, post_mix_w,
                  pre_mlp_w, w_up, w_down, post_mlp_w):
    w = _bf(w_in[l])
    pvec = jnp.zeros((8, LANES), F32)
    pvec = pvec.at[0, :HEADS].set(A_log[l].astype(F32)).at[1, :HEADS].set(dt_bias[l].astype(F32))
    return dict(
        pre_w=pre_mix_w[l].astype(F32)[None],
        w_in=w,
        w_ab=jnp.pad(w[:, MAIN_W:], ((0, 0), (0, LANES - 2 * HEADS))),
        conv_w=conv_w[l].astype(F32),
        pvec=pvec,
        normw=jnp.concatenate([jnp.tile(ret_norm_w[l], HEADS), jnp.tile(gdn_norm_w[l], HEADS)]).astype(F32)[None],
        w_out=_bf(w_out[l]),
        post_w=post_mix_w[l].astype(F32)[None],
        pre_mlp_w=pre_mlp_w[l].astype(F32)[None],
        w_up=_bf(w_up[l]),
        w_down=_bf(w_down[l]),
        post_mlp_w=post_mlp_w[l].astype(F32)[None],
    )


def kernel(x_prompt, x_sample, state_ret, state_gdn, state_conv, pre_mix_w, w_in, conv_w, A_log, dt_bias,
           ret_norm_w, gdn_norm_w, w_out, post_mix_w, pre_mlp_w, w_up, w_down, post_mlp_w):
    depth = w_in.shape[0]
    yp, ys = x_prompt, x_sample
    outs = [[] for _ in range(6)]
    for l in range(depth):
        params = _layer_params(l, pre_mix_w, w_in, conv_w, A_log, dt_bias, ret_norm_w, gdn_norm_w, w_out,
                               post_mix_w, pre_mlp_w, w_up, w_down, post_mlp_w)
        mix_p, rp, gp, cp = _mixer(yp, params, prompt=True)
        yp = _post(mix_p, yp, params)
        mix_s, rs, gs, cs = _mixer(ys, params, prompt=False,
                                   states=(state_ret[l], state_gdn[l], state_conv[l]))
        ys = _post(mix_s, ys, params)
        for dst, val, like in zip(outs, (rp, gp, cp, rs, gs, cs),
                                  (state_ret, state_gdn, state_conv) * 2):
            dst.append(val.astype(like.dtype))
    return (yp, ys) + tuple(jnp.stack(o) for o in outs)
```

```python
import functools

import numpy as np
import jax
import jax.numpy as jnp
from jax import lax
from jax.experimental import pallas as pl
from jax.experimental.pallas import tpu as pltpu

D_MODEL = 1024
HEADS = 4
HEAD_DIM = 128
GROUP_W = HEADS * HEAD_DIM
CONV_CH = 3 * GROUP_W
CONV_W = 4
D_FF = 4 * D_MODEL
MAIN_W = 4 * GROUP_W + CONV_CH + GROUP_W
IN_W = MAIN_W + 2 * HEADS
ROPE_BASE = 10000.0
EPS = 1e-6
PAST_LEN = 16384

ROWS = 64
BASE = 8
LANES = 128
TILE = 256
POST_ROWS = 1024
POST_PIECE = 256
VMEM_LIMIT = 58 * 1024 * 1024

OFF_RQ, OFF_RK, OFF_RV, OFF_RG = 0, GROUP_W, 2 * GROUP_W, 3 * GROUP_W
OFF_CONV = 4 * GROUP_W
OFF_Z = OFF_CONV + CONV_CH

F32 = jnp.float32
BF16 = jnp.bfloat16


def _bf(x):
    return x.astype(BF16)


def _dot(a, b):
    return jnp.dot(a, b, preferred_element_type=F32)


def _dot_nt(a, b):
    return lax.dot_general(a, b, (((1,), (1,)), ((), ())), preferred_element_type=F32)


def _split(x, n):
    parts, r = [], x
    for i in range(n):
        p = r.astype(BF16)
        parts.append(p)
        if i + 1 < n:
            r = r - p.astype(F32)
    return parts


def _mm(a, b, nt=False):
    return _dot_nt(_bf(a), _bf(b)) if nt else _dot(_bf(a), _bf(b))


def _mm3(a, b):
    (a_hi, a_lo), (b_hi, b_lo) = _split(a, 2), _split(b, 2)
    n = a.shape[0]
    both = _dot(jnp.concatenate([a_hi, a_lo], axis=0), b_hi)
    return both[:n] + both[n:] + _dot(a_hi, b_lo)


def _rms(x, w):
    return x * lax.rsqrt(jnp.mean(x * x, axis=-1, keepdims=True) + EPS) * w


def _silu(x):
    return x / (1.0 + jnp.exp(-x))


def _tri_inverse_many(mats, group, row, col):
    n = mats[0].shape[0]
    packed = [jnp.concatenate(mats[i:i + HEADS], axis=1) for i in range(0, len(mats), HEADS)]
    prow = lax.broadcasted_iota(jnp.int32, (n, HEADS * n), 0)
    plane = lax.broadcasted_iota(jnp.int32, (n, HEADS * n), 1)
    pcol = plane % n
    seg = [_bf(((plane // n) == hh).astype(F32)) for hh in range(HEADS)]

    def blockdiag(y):
        return jnp.concatenate([y * m for m in seg], axis=0)

    def mm3(x, y):
        (x_hi, x_lo), (y_hi, y_lo) = _split(x, 2), _split(y, 2)
        both = _dot(jnp.concatenate([x_hi, x_lo], axis=0), blockdiag(y_hi))
        return both[:n] + both[n:] + _dot(x_hi, blockdiag(y_lo))

    eye_p = (prow == pcol).astype(F32)
    blk = (prow // BASE) == (pcol // BASE)
    n8 = [jnp.where(blk, -a, 0.0) for a in packed]
    p2 = [mm3(x, x) for x in n8]
    np2 = [mm3(x, p) for x, p in zip(n8, p2)]
    p4 = [mm3(p, p) for p in p2]
    e = [x + p + y for x, p, y in zip(n8, p2, np2)]
    ep4 = [mm3(x, p) for x, p in zip(e, p4)]
    t = [x + p + y + eye_p for x, p, y in zip(e, p4, ep4)]
    s = BASE
    while s < group:
        off = ((prow // (2 * s)) == (pcol // (2 * s))) & ((prow // s) != (pcol // s))
        a_off = [jnp.where(off, a, 0.0) for a in packed]
        ta = [mm3(x, y) for x, y in zip(t, a_off)]
        tat = [mm3(x, y) for x, y in zip(ta, t)]
        t = [x - y for x, y in zip(t, tat)]
        s *= 2
    eye = (row == col).astype(F32)
    return [tp[:, hh * n:(hh + 1) * n] for tp in t for hh in range(HEADS)], eye


def _mixer_kernel(cfg, *refs):
    lb, group, prompt = cfg["lb"], cfg["group"], cfg["prompt"]
    ngroups = ROWS // group
    nchunk = lb // ROWS
    chunks = list(range(nchunk))
    it = iter(refs)
    x_ref, prew_ref, win_ref, wab_ref, cos_ref, sin_ref = (next(it) for _ in range(6))
    dmat_ref, rtab_ref, convw_ref, pvec_ref, normw_ref = (next(it) for _ in range(5))
    if not prompt:
        sret_in, sgdn_in, st8_ref = (next(it) for _ in range(3))
    mix_ref, sret_out, sgdn_out, conv_out = (next(it) for _ in range(4))
    qr_s, kr_s, qg_s, kg_s, vg_s, cum_s, tot_s, beta_s, o_s = (next(it) for _ in range(9))
    w_s, u_s, pr_s, pg_s, vb_s, gate_s = (next(it) for _ in range(6))
    if prompt:
        sret_s, sgdn_s, ext_s = (next(it) for _ in range(3))

        @pl.when(pl.program_id(0) == 0)
        def _():
            sret_s[...] = jnp.zeros_like(sret_s)
            sgdn_s[...] = jnp.zeros_like(sgdn_s)
            ext_s[:, 0:8, :] = jnp.zeros((nchunk, 8, CONV_CH), F32)

    rows_of = lambda c: slice(c * ROWS, (c + 1) * ROWS)
    head_sl = lambda hh: slice(hh * HEAD_DIM, (hh + 1) * HEAD_DIM)
    lane = lambda x, n: x[:, n:n + 1]
    cat = lambda xs: xs[0] if len(xs) == 1 else jnp.concatenate(xs, axis=0)

    x = x_ref[...].reshape(lb, D_MODEL)
    h = _bf(_rms(x, prew_ref[...]))
    ab = _dot(h, wab_ref[...])
    cos, sin = cos_ref[...], sin_ref[...]
    if prompt:
        cos, sin = jnp.concatenate([cos] * nchunk, axis=0), jnp.concatenate([sin] * nchunk, axis=0)
    t_in_group = lax.broadcasted_iota(jnp.int32, (lb, 1), 0) % group

    for c0 in range(0, MAIN_W, TILE):
        res = _dot(h, win_ref[:, c0:c0 + TILE])
        tiles = [res[:, j * HEAD_DIM:(j + 1) * HEAD_DIM] for j in range(TILE // HEAD_DIM)]
        if c0 < OFF_RV:
            for j, tl in enumerate(tiles):
                rot = tl * cos + pltpu.roll(tl, HEAD_DIM // 2, 1) * sin
                if c0 < OFF_RK:
                    qr_s[:, c0 - OFF_RQ + j * HEAD_DIM:c0 - OFF_RQ + (j + 1) * HEAD_DIM] = rot
                else:
                    kr_s[:, c0 - OFF_RK + j * HEAD_DIM:c0 - OFF_RK + (j + 1) * HEAD_DIM] = rot * (HEAD_DIM ** -0.5)
        elif c0 < OFF_RG:
            vb_s[:, c0 - OFF_RV:c0 - OFF_RV + TILE] = _bf(res)
        elif c0 < OFF_CONV:
            gate_s[:, c0 - OFF_RG:c0 - OFF_RG + TILE] = _silu(res)
        elif c0 >= OFF_Z:
            gate_s[:, GROUP_W + c0 - OFF_Z:GROUP_W + c0 - OFF_Z + TILE] = _silu(res)
        else:
            cc = c0 - OFF_CONV
            part = cc // GROUP_W
            dst = (qg_s, kg_s, vg_s)[part]
            cw = convw_ref[:, cc:cc + TILE]
            acc = res * cw[CONV_W - 1:CONV_W]
            if prompt:
                for c in chunks:
                    ext_s[c, 8:, cc:cc + TILE] = res[rows_of(c)]
                for s in range(1, CONV_W):
                    sh = jnp.concatenate([ext_s[c, 8 - s:8 - s + ROWS, cc:cc + TILE] for c in chunks], axis=0)
                    acc = acc + sh * cw[CONV_W - 1 - s:CONV_W - s]
                for c in chunks:
                    tail = ext_s[c, ROWS:, cc:cc + TILE]
                    ext_s[c, 0:8, cc:cc + TILE] = tail
                    conv_out[c * 8:(c + 1) * 8, cc:cc + TILE] = tail
            else:
                st8 = st8_ref[:, cc:cc + TILE]
                for s in range(1, CONV_W):
                    sh = jnp.where(t_in_group < s, pltpu.roll(st8, lb - 8 + s, 0), pltpu.roll(res, s, 0))
                    acc = acc + sh * cw[CONV_W - 1 - s:CONV_W - s]
                conv_out[:, cc:cc + TILE] = res
            conv = _silu(acc)
            for j in range(TILE // HEAD_DIM):
                c = conv[:, j * HEAD_DIM:(j + 1) * HEAD_DIM]
                if part < 2:
                    c = c * lax.rsqrt(jnp.sum(c * c, axis=-1, keepdims=True) + EPS)
                if part == 0:
                    c = c * (HEAD_DIM ** -0.5)
                d0 = cc - part * GROUP_W + j * HEAD_DIM
                dst[:, d0:d0 + HEAD_DIM] = c

    pv = pvec_ref[...]
    a_plus = ab + pv[1:2]
    softplus = jnp.maximum(a_plus, 0.0) + jnp.log1p(jnp.exp(-jnp.abs(a_plus)))
    g = -jnp.exp(pv[0:1]) * softplus
    beta_s[...] = 1.0 / (1.0 + jnp.exp(-ab))

    row = lax.broadcasted_iota(jnp.int32, (ROWS, ROWS), 0)
    col = lax.broadcasted_iota(jnp.int32, (ROWS, ROWS), 1)
    same = (row // group) == (col // group)
    causal = same & (row >= col)
    strict = same & (row > col)
    sum_mat = _bf(jnp.concatenate([causal.astype(F32), same.astype(F32)], axis=0))
    sum_mat3 = jnp.concatenate([sum_mat] * 3, axis=1)
    for c in chunks:
        r = _dot(sum_mat3, jnp.concatenate(_split(g[rows_of(c)], 3), axis=0))
        cum_s[rows_of(c), :] = r[:ROWS]
        tot_s[rows_of(c), :] = r[ROWS:]
    sel = _bf((lax.broadcasted_iota(jnp.int32, (8, LANES), 0)
               == lax.broadcasted_iota(jnp.int32, (8, LANES), 1)).astype(F32))
    sel3 = jnp.concatenate([sel] * 3, axis=1)

    items = [(c, hh) for c in chunks for hh in range(HEADS)]
    cumc = [cum_s[rows_of(c), :] for c in chunks]
    totc = [tot_s[rows_of(c), :] for c in chunks]
    betac = [beta_s[rows_of(c), :] for c in chunks]
    ecumc = [jnp.exp(x) for x in cumc]
    tailc = [jnp.exp(t - x) for t, x in zip(totc, cumc)]
    cumrow = [_dot_nt(sel3, jnp.concatenate(_split(x, 3), axis=1)) for x in cumc]

    for c, hh in items:
        rows, sl = rows_of(c), head_sl(hh)
        q, k = qr_s[rows, sl], kr_s[rows, sl]
        pr_s[c * HEADS + hh] = _mm(q, k, nt=True) * dmat_ref[hh]
        qr_s[rows, sl] = q * rtab_ref[0, :, sl]
        kr_s[rows, sl] = k * rtab_ref[1, :, sl]

    qk_kk, dmats = [], []
    for c, hh in items:
        rows, sl = rows_of(c), head_sl(hh)
        kb = _bf(kg_s[rows, sl])
        qk_kk.append(_dot_nt(jnp.concatenate([_bf(qg_s[rows, sl]), kb], axis=0), kb))
        diff = lane(cumc[c], hh) - cumrow[c][hh:hh + 1, :]
        dmats.append(jnp.where(causal, jnp.exp(jnp.minimum(diff, 0.0)), 0.0))
    a_mats = [jnp.where(strict, lane(betac[c], HEADS + hh) * x[ROWS:] * d, 0.0)
              for (c, hh), x, d in zip(items, qk_kk, dmats)]
    t_inv, eye = _tri_inverse_many(a_mats, group, row, col)
    for i, (c, hh) in enumerate(items):
        rows, sl = rows_of(c), head_sl(hh)
        q, k, v = qg_s[rows, sl], kg_s[rows, sl], vg_s[rows, sl]
        beta_h = lane(betac[c], HEADS + hh)
        rhs = jnp.concatenate([k * (beta_h * lane(ecumc[c], hh)), v * beta_h], axis=1)
        sol = rhs + _mm(t_inv[i] - eye, rhs)
        w_s[rows, sl] = sol[:, :HEAD_DIM]
        u_s[rows, sl] = sol[:, HEAD_DIM:]
        pg_s[c * HEADS + hh] = qk_kk[i][:ROWS] * dmats[i]
        qg_s[rows, sl] = q * lane(ecumc[c], hh)
        kg_s[rows, sl] = k * lane(tailc[c], hh)

    def state_refs(c, hh, gi):
        if prompt:
            return (sret_s.at[c, hh], sret_s.at[c, hh], sgdn_s.at[c, hh], sgdn_s.at[c, hh])
        n = c * ngroups + gi
        return (sret_in.at[n, hh], sret_out.at[n, hh], sgdn_in.at[n, hh], sgdn_out.at[n, hh])

    for c in chunks:
        dtotc = jnp.exp(totc[c])
        qs_r, qs_g, ws_g = {}, {}, {}
        for hh in range(HEADS):
            sl = head_sl(hh)
            for gi in range(ngroups):
                rs = slice(c * ROWS + gi * group, c * ROWS + (gi + 1) * group)
                r_in, _, g_in, _ = state_refs(c, hh, gi)
                qs_r[hh, gi] = _mm(qr_s[rs, sl], r_in[...])
                r = _mm(jnp.concatenate([qg_s[rs, sl], w_s[rs, sl]], axis=0), g_in[...])
                qs_g[hh, gi], ws_g[hh, gi] = r[:group], r[group:]
        for hh in range(HEADS):
            rows, sl = rows_of(c), head_sl(hh)
            i = c * HEADS + hh
            vb = vb_s[rows, sl]
            v_new = u_s[rows, sl] - cat([ws_g[hh, gi] for gi in range(ngroups)])
            vnb = _bf(v_new)
            o_s[rows, sl] = cat([qs_r[hh, gi] for gi in range(ngroups)]) + _mm(pr_s[i], vb)
            o_s[rows, GROUP_W + hh * HEAD_DIM:GROUP_W + (hh + 1) * HEAD_DIM] = (
                cat([qs_g[hh, gi] for gi in range(ngroups)]) + _mm(pg_s[i], vnb))
            for gi in range(ngroups):
                rs = slice(c * ROWS + gi * group, c * ROWS + (gi + 1) * group)
                ls = slice(gi * group, (gi + 1) * group)
                r_in, r_out, g_in, g_out = state_refs(c, hh, gi)
                r_out[...] = cfg["ret_dtot"][hh] * r_in[...] + _mm(kr_s[rs, sl].T, vb[ls])
                dec = dtotc[gi * group:gi * group + 1, hh:hh + 1]
                g_out[...] = dec * g_in[...] + _mm(kg_s[rs, sl].T, vnb[ls])

    for hh in range(2 * HEADS):
        sl = head_sl(hh)
        y = _rms(o_s[:, sl], normw_ref[:, sl]) * gate_s[:, sl]
        y = y.astype(mix_ref.dtype)
        if prompt:
            mix_ref[:, :, sl] = y.reshape(nchunk, ROWS, HEAD_DIM)
        else:
            mix_ref[:, sl] = y

    if prompt:
        @pl.when(pl.program_id(0) == pl.num_programs(0) - 1)
        def _():
            sret_out[...] = sret_s[...]
            sgdn_out[...] = sgdn_s[...]


def _post_kernel(mix_ref, x_ref, wout_ref, postw_ref, premlpw_ref, wup_ref, wdown_ref, postmlpw_ref, out_ref):
    tm = x_ref.shape[0]
    piece = POST_PIECE if tm % POST_PIECE == 0 else tm
    halves = [slice(r, r + piece) for r in range(0, tm, piece)]
    m = [_dot(mix_ref[r, :], wout_ref[...]) for r in halves]
    x1 = [x_ref[r, :] + _rms(mi, postw_ref[...]) for r, mi in zip(halves, m)]
    h2 = [_bf(_rms(xi, premlpw_ref[...])) for xi in x1]
    acc = [None] * len(halves)
    for c in range(D_FF // D_MODEL):
        for i in range(len(halves)):
            f = _dot(h2[i], wup_ref[:, c * D_MODEL:(c + 1) * D_MODEL])
            f = jnp.square(jnp.maximum(f, 0.0))
            part = _dot(_bf(f), wdown_ref[c * D_MODEL:(c + 1) * D_MODEL, :])
            acc[i] = part if acc[i] is None else acc[i] + part
    for r, xi, ai in zip(halves, x1, acc):
        out_ref[r, :] = xi + _rms(ai, postmlpw_ref[...])


def _const_spec(shape, single=False):
    index_map = lambda *_: (0,) * len(shape)
    if single:
        return pl.BlockSpec(shape, index_map, pipeline_mode=pl.Buffered(1))
    return pl.BlockSpec(shape, index_map)


def _retention_tables(group):
    log_gamma = np.log(1.0 - 2.0 ** (-5.0 - np.arange(HEADS, dtype=np.float64)))
    t = np.arange(ROWS) % group
    same = (np.arange(ROWS)[:, None] // group) == (np.arange(ROWS)[None, :] // group)
    causal = same & (t[:, None] >= t[None, :])
    dmat = np.where(causal[None], np.exp((t[:, None] - t[None, :])[None] * log_gamma[:, None, None]), 0.0)
    ecum = np.exp((t[:, None] + 1.0) * log_gamma[None, :])
    tail = np.exp((group - 1.0 - t[:, None]) * log_gamma[None, :])
    rtab = np.stack([np.repeat(ecum, HEAD_DIM, axis=1), np.repeat(tail, HEAD_DIM, axis=1)])
    dtot = tuple(float(np.float32(v)) for v in np.exp(group * log_gamma))
    return jnp.asarray(dmat, F32), jnp.asarray(rtab, F32), dtot


def _rope_tables(length, offset):
    inv_freq = ROPE_BASE ** (-np.arange(0, HEAD_DIM, 2, dtype=np.float64) / HEAD_DIM)
    ang = (np.arange(length, dtype=np.float64) + offset)[:, None] * inv_freq[None, :]
    cos, sin = np.cos(ang), np.sin(ang)
    return (jnp.asarray(np.concatenate([cos, cos], axis=-1), F32),
            jnp.asarray(np.concatenate([-sin, sin], axis=-1), F32))


def _mixer(x, params, *, prompt, states=None):
    b, l, _ = x.shape
    t = b * l
    if prompt:
        group, lb = ROWS, b * ROWS
        assert l % ROWS == 0
        cos, sin = _rope_tables(l, 0)
        grid = (l // ROWS,)
        x_in = x
        x_spec = pl.BlockSpec((b, ROWS, D_MODEL), lambda j: (0, j, 0))
        pos_spec = pl.BlockSpec((ROWS, HEAD_DIM), lambda j: (j, 0))
    else:
        group = l
        lb = min(2 * ROWS, t)
        assert ROWS % group == 0 and group == 8 and t % lb == 0 and lb % ROWS == 0
        cos, sin = _rope_tables(l, PAST_LEN)
        cos, sin = jnp.tile(cos, (lb // l, 1)), jnp.tile(sin, (lb // l, 1))
        grid = (t // lb,)
        row_map = lambda i: (i, 0)
        x_in = x.reshape(t, D_MODEL)
        x_spec = pl.BlockSpec((lb, D_MODEL), row_map)
        pos_spec = pl.BlockSpec((lb, HEAD_DIM), lambda i: (0, 0))
    nseq = lb // group
    dmat, rtab, dtot = _retention_tables(group)
    cfg = dict(lb=lb, group=group, prompt=prompt, ret_dtot=dtot)

    in_specs = [
        x_spec,
        _const_spec((1, D_MODEL)),
        _const_spec((D_MODEL, IN_W), single=True),
        _const_spec((D_MODEL, LANES), single=True),
        pos_spec,
        pos_spec,
        _const_spec((HEADS, ROWS, ROWS)),
        _const_spec((2, ROWS, GROUP_W)),
        _const_spec((CONV_W, CONV_CH)),
        _const_spec((8, LANES)),
        _const_spec((1, 2 * GROUP_W)),
    ]
    args = [x_in, params["pre_w"], params["w_in"], params["w_ab"], cos, sin, dmat, rtab,
            params["conv_w"], params["pvec"], params["normw"]]
    state_shape = (HEADS, HEAD_DIM, HEAD_DIM)
    if prompt:
        out_specs = [
            pl.BlockSpec((b, ROWS, 2 * GROUP_W), lambda j: (0, j, 0)),
            _const_spec((b,) + state_shape),
            _const_spec((b,) + state_shape),
            _const_spec((b * 8, CONV_CH)),
        ]
        out_shape = [
            jax.ShapeDtypeStruct((b, l, 2 * GROUP_W), BF16),
            jax.ShapeDtypeStruct((b,) + state_shape, F32),
            jax.ShapeDtypeStruct((b,) + state_shape, F32),
            jax.ShapeDtypeStruct((b * 8, CONV_CH), F32),
        ]
    else:
        s_ret, s_gdn, s_conv = states
        st8 = jnp.pad(s_conv.astype(F32), ((0, 0), (8 - (CONV_W - 1), 0), (0, 0))).reshape(t, CONV_CH)
        in_specs += [
            pl.BlockSpec((nseq,) + state_shape, lambda i: (i, 0, 0, 0)),
            pl.BlockSpec((nseq,) + state_shape, lambda i: (i, 0, 0, 0)),
            pl.BlockSpec((lb, CONV_CH), row_map),
        ]
        args += [s_ret.astype(F32), s_gdn.astype(F32), st8]
        out_specs = [
            pl.BlockSpec((lb, 2 * GROUP_W), row_map),
            pl.BlockSpec((nseq,) + state_shape, lambda i: (i, 0, 0, 0)),
            pl.BlockSpec((nseq,) + state_shape, lambda i: (i, 0, 0, 0)),
            pl.BlockSpec((lb, CONV_CH), row_map),
        ]
        out_shape = [
            jax.ShapeDtypeStruct((t, 2 * GROUP_W), BF16),
            jax.ShapeDtypeStruct((b,) + state_shape, F32),
            jax.ShapeDtypeStruct((b,) + state_shape, F32),
            jax.ShapeDtypeStruct((t, CONV_CH), F32),
        ]
    nmat = lb // ROWS * HEADS
    scratch = [
        pltpu.VMEM((lb, GROUP_W), F32), pltpu.VMEM((lb, GROUP_W), F32),
        pltpu.VMEM((lb, GROUP_W), F32), pltpu.VMEM((lb, GROUP_W), F32), pltpu.VMEM((lb, GROUP_W), F32),
        pltpu.VMEM((lb, LANES), F32), pltpu.VMEM((lb, LANES), F32), pltpu.VMEM((lb, LANES), F32),
        pltpu.VMEM((lb, 2 * GROUP_W), F32),
        pltpu.VMEM((lb, GROUP_W), F32), pltpu.VMEM((lb, GROUP_W), F32),
        pltpu.VMEM((nmat, ROWS, ROWS), F32), pltpu.VMEM((nmat, ROWS, ROWS), F32),
        pltpu.VMEM((lb, GROUP_W), BF16), pltpu.VMEM((lb, 2 * GROUP_W), F32),
    ]
    if prompt:
        scratch += [pltpu.VMEM((b,) + state_shape, F32), pltpu.VMEM((b,) + state_shape, F32),
                    pltpu.VMEM((b, 8 + ROWS, CONV_CH), F32)]

    mix, s_ret_new, s_gdn_new, conv_rows = pl.pallas_call(
        functools.partial(_mixer_kernel, cfg),
        grid=grid,
        in_specs=in_specs,
        out_specs=out_specs,
        out_shape=out_shape,
        scratch_shapes=scratch,
        compiler_params=pltpu.CompilerParams(dimension_semantics=("arbitrary",), vmem_limit_bytes=VMEM_LIMIT),
        name="mixer_prompt" if prompt else "mixer_sample",
    )(*args)
    new_conv = conv_rows.reshape(b, 8, CONV_CH)[:, 8 - (CONV_W - 1):]
    return mix.reshape(t, 2 * GROUP_W), s_ret_new, s_gdn_new, new_conv


def _post(mix, x, params):
    b, l, _ = x.shape
    t = b * l
    tm = min(POST_ROWS, t)
    assert t % tm == 0
    rows = pl.BlockSpec((tm, D_MODEL), lambda i: (i, 0))
    y = pl.pallas_call(
        _post_kernel,
        grid=(t // tm,),
        in_specs=[rows, rows, _const_spec((2 * GROUP_W, D_MODEL), single=True), _const_spec((1, D_MODEL)),
                  _const_spec((1, D_MODEL)), _const_spec((D_MODEL, D_FF), single=True),
                  _const_spec((D_FF, D_MODEL), single=True), _const_spec((1, D_MODEL))],
        out_specs=rows,
        out_shape=jax.ShapeDtypeStruct((t, D_MODEL), F32),
        compiler_params=pltpu.CompilerParams(dimension_semantics=("arbitrary",), vmem_limit_bytes=VMEM_LIMIT),
        name="post",
    )(mix, x.reshape(t, D_MODEL), params["w_out"], params["post_w"], params["pre_mlp_w"],
      params["w_up"], params["w_down"], params["post_mlp_w"])
    return y.reshape(b, l, D_MODEL)


def _layer_params(l, pre_mix_w, w_in, conv_w, A_log, dt_bias, ret_norm_w, gdn_norm_w, w_out, post_mix_w,
                  pre_mlp_w, w_up, w_down, post_mlp_w):
    w = _bf(w_in[l])
    pvec = jnp.zeros((8, LANES), F32)
    pvec = pvec.at[0, :HEADS].set(A_log[l].astype(F32)).at[1, :HEADS].set(dt_bias[l].astype(F32))
    return dict(
        pre_w=pre_mix_w[l].astype(F32)[None],
        w_in=w,
        w_ab=jnp.pad(w[:, MAIN_W:], ((0, 0), (0, LANES - 2 * HEADS))),
        conv_w=conv_w[l].astype(F32),
        pvec=pvec,
        normw=jnp.concatenate([jnp.tile(ret_norm_w[l], HEADS), jnp.tile(gdn_norm_w[l], HEADS)]).astype(F32)[None],
        w_out=_bf(w_out[l]),
        post_w=post_mix_w[l].astype(F32)[None],
        pre_mlp_w=pre_mlp_w[l].astype(F32)[None],
        w_up=_bf(w_up[l]),
        w_down=_bf(w_down[l]),
        post_mlp_w=post_mlp_w[l].astype(F32)[None],
    )


def kernel(x_prompt, x_sample, state_ret, state_gdn, state_conv, pre_mix_w, w_in, conv_w, A_log, dt_bias,
           ret_norm_w, gdn_norm_w, w_out, post_mix_w, pre_mlp_w, w_up, w_down, post_mlp_w):
    depth = w_in.shape[0]
    yp, ys = x_prompt, x_sample
    outs = [[] for _ in range(6)]
    for l in range(depth):
        params = _layer_params(l, pre_mix_w, w_in, conv_w, A_log, dt_bias, ret_norm_w, gdn_norm_w, w_out,
                               post_mix_w, pre_mlp_w, w_up, w_down, post_mlp_w)
        mix_p, rp, gp, cp = _mixer(yp, params, prompt=True)
        yp = _post(mix_p, yp, params)
        mix_s, rs, gs, cs = _mixer(ys, params, prompt=False,
                                   states=(state_ret[l], state_gdn[l], state_conv[l]))
        ys = _post(mix_s, ys, params)
        for dst, val, like in zip(outs, (rp, gp, cp, rs, gs, cs),
                                  (state_ret, state_gdn, state_conv) * 2):
            dst.append(val.astype(like.dtype))
    return (yp, ys) + tuple(jnp.stack(o) for o in outs)
```

```python
import functools

import numpy as np
import jax
import jax.numpy as jnp
from jax import lax
from jax.experimental import pallas as pl
from jax.experimental.pallas import tpu as pltpu

D_MODEL = 1024
HEADS = 4
HEAD_DIM = 128
GROUP_W = HEADS * HEAD_DIM
CONV_CH = 3 * GROUP_W
CONV_W = 4
D_FF = 4 * D_MODEL
MAIN_W = 4 * GROUP_W + CONV_CH + GROUP_W
IN_W = MAIN_W + 2 * HEADS
ROPE_BASE = 10000.0
EPS = 1e-6
PAST_LEN = 16384

ROWS = 64
BASE = 8
LANES = 128
TILE = 256
POST_ROWS = 1024
POST_PIECE = 256
VMEM_LIMIT = 58 * 1024 * 1024

OFF_RQ, OFF_RK, OFF_RV, OFF_RG = 0, GROUP_W, 2 * GROUP_W, 3 * GROUP_W
OFF_CONV = 4 * GROUP_W
OFF_Z = OFF_CONV + CONV_CH

F32 = jnp.float32
BF16 = jnp.bfloat16


def _bf(x):
    return x.astype(BF16)


def _dot(a, b):
    return jnp.dot(a, b, preferred_element_type=F32)


def _dot_nt(a, b):
    return lax.dot_general(a, b, (((1,), (1,)), ((), ())), preferred_element_type=F32)


def _split(x, n):
    parts, r = [], x
    for i in range(n):
        p = r.astype(BF16)
        parts.append(p)
        if i + 1 < n:
            r = r - p.astype(F32)
    return parts


def _mm(a, b, nt=False):
    return _dot_nt(_bf(a), _bf(b)) if nt else _dot(_bf(a), _bf(b))


def _mm3(a, b):
    (a_hi, a_lo), (b_hi, b_lo) = _split(a, 2), _split(b, 2)
    n = a.shape[0]
    both = _dot(jnp.concatenate([a_hi, a_lo], axis=0), b_hi)
    return both[:n] + both[n:] + _dot(a_hi, b_lo)


def _rms(x, w):
    return x * lax.rsqrt(jnp.mean(x * x, axis=-1, keepdims=True) + EPS) * w


def _silu(x):
    return x / (1.0 + jnp.exp(-x))


def _tri_inverse_many(mats, group, row, col):
    n = mats[0].shape[0]
    packed = [jnp.concatenate(mats[i:i + HEADS], axis=1) for i in range(0, len(mats), HEADS)]
    prow = lax.broadcasted_iota(jnp.int32, (n, HEADS * n), 0)
    plane = lax.broadcasted_iota(jnp.int32, (n, HEADS * n), 1)
    pcol = plane % n
    seg = [_bf(((plane // n) == hh).astype(F32)) for hh in range(HEADS)]

    def blockdiag(y):
        return jnp.concatenate([y * m for m in seg], axis=0)

    def mm3(x, y):
        (x_hi, x_lo), (y_hi, y_lo) = _split(x, 2), _split(y, 2)
        both = _dot(jnp.concatenate([x_hi, x_lo], axis=0), blockdiag(y_hi))
        return both[:n] + both[n:] + _dot(x_hi, blockdiag(y_lo))

    eye_p = (prow == pcol).astype(F32)
    blk = (prow // BASE) == (pcol // BASE)
    n8 = [jnp.where(blk, -a, 0.0) for a in packed]
    p2 = [mm3(x, x) for x in n8]
    np2 = [mm3(x, p) for x, p in zip(n8, p2)]
    p4 = [mm3(p, p) for p in p2]
    e = [x + p + y for x, p, y in zip(n8, p2, np2)]
    ep4 = [mm3(x, p) for x, p in zip(e, p4)]
    t = [x + p + y + eye_p for x, p, y in zip(e, p4, ep4)]
    s = BASE
    while s < group:
        off = ((prow // (2 * s)) == (pcol // (2 * s))) & ((prow // s) != (pcol // s))
        a_off = [jnp.where(off, a, 0.0) for a in packed]
        ta = [mm3(x, y) for x, y in zip(t, a_off)]
        tat = [mm3(x, y) for x, y in zip(ta, t)]
        t = [x - y for x, y in zip(t, tat)]
        s *= 2
    eye = (row == col).astype(F32)
    return [tp[:, hh * n:(hh + 1) * n] for tp in t for hh in range(HEADS)], eye


def _mixer_kernel(cfg, *refs):
    if not cfg["prompt"]:
        _mixer_step(cfg, refs, deferred=False)
        return
    step, last = pl.program_id(0), pl.num_programs(0) - 1

    @pl.when(step < last)
    def _():
        _mixer_step(cfg, refs, deferred=True)

    @pl.when(step == last)
    def _():
        _mixer_step(cfg, refs, deferred=True, flush=True)


def _mixer_step(cfg, refs, deferred, flush=False):
    lb, group, prompt = cfg["lb"], cfg["group"], cfg["prompt"]
    ngroups = ROWS // group
    nchunk = lb // ROWS
    chunks = list(range(nchunk))
    it = iter(refs)
    x_ref, prew_ref, win_ref, wab_ref, cos_ref, sin_ref = (next(it) for _ in range(6))
    dmat_ref, rtab_ref, convw_ref, pvec_ref, normw_ref = (next(it) for _ in range(5))
    if not prompt:
        sret_in, sgdn_in, st8_ref = (next(it) for _ in range(3))
    mix_ref, sret_out, sgdn_out, conv_out = (next(it) for _ in range(4))
    qr_s, kr_s, qg_s, kg_s, vg_s, cum_s, tot_s, beta_s, o_s = (next(it) for _ in range(9))
    w_s, u_s, pr_s, pg_s, vb_s, gate_s = (next(it) for _ in range(6))
    head_sl = lambda hh: slice(hh * HEAD_DIM, (hh + 1) * HEAD_DIM)

    def head_norm():
        for hh in range(2 * HEADS):
            sl = head_sl(hh)
            y = (_rms(o_s[:, sl], normw_ref[:, sl]) * gate_s[:, sl]).astype(mix_ref.dtype)
            if prompt:
                mix_ref[:, :, sl] = y.reshape(nchunk, ROWS, HEAD_DIM)
            else:
                mix_ref[:, sl] = y

    if prompt:
        sret_s, sgdn_s, ext_s = (next(it) for _ in range(3))
        if flush:
            head_norm()
            sret_out[...] = sret_s[...]
            sgdn_out[...] = sgdn_s[...]
            return

        @pl.when(pl.program_id(0) == 0)
        def _():
            sret_s[...] = jnp.zeros_like(sret_s)
            sgdn_s[...] = jnp.zeros_like(sgdn_s)
            ext_s[:, 0:8, :] = jnp.zeros((nchunk, 8, CONV_CH), F32)
            o_s[...] = jnp.zeros_like(o_s)
            gate_s[...] = jnp.zeros_like(gate_s)

    rows_of = lambda c: slice(c * ROWS, (c + 1) * ROWS)
    lane = lambda x, n: x[:, n:n + 1]
    cat = lambda xs: xs[0] if len(xs) == 1 else jnp.concatenate(xs, axis=0)

    x = x_ref[...].reshape(lb, D_MODEL)
    h = _bf(_rms(x, prew_ref[...]))
    if deferred:
        head_norm()
    ab = _dot(h, wab_ref[...])
    cos, sin = cos_ref[...], sin_ref[...]
    if prompt:
        cos, sin = jnp.concatenate([cos] * nchunk, axis=0), jnp.concatenate([sin] * nchunk, axis=0)
    t_in_group = lax.broadcasted_iota(jnp.int32, (lb, 1), 0) % group

    for c0 in range(0, MAIN_W, TILE):
        res = _dot(h, win_ref[:, c0:c0 + TILE])
        tiles = [res[:, j * HEAD_DIM:(j + 1) * HEAD_DIM] for j in range(TILE // HEAD_DIM)]
        if c0 < OFF_RV:
            for j, tl in enumerate(tiles):
                rot = tl * cos + pltpu.roll(tl, HEAD_DIM // 2, 1) * sin
                if c0 < OFF_RK:
                    qr_s[:, c0 - OFF_RQ + j * HEAD_DIM:c0 - OFF_RQ + (j + 1) * HEAD_DIM] = rot
                else:
                    kr_s[:, c0 - OFF_RK + j * HEAD_DIM:c0 - OFF_RK + (j + 1) * HEAD_DIM] = rot * (HEAD_DIM ** -0.5)
        elif c0 < OFF_RG:
            vb_s[:, c0 - OFF_RV:c0 - OFF_RV + TILE] = _bf(res)
        elif c0 < OFF_CONV:
            gate_s[:, c0 - OFF_RG:c0 - OFF_RG + TILE] = _silu(res)
        elif c0 >= OFF_Z:
            gate_s[:, GROUP_W + c0 - OFF_Z:GROUP_W + c0 - OFF_Z + TILE] = _silu(res)
        else:
            cc = c0 - OFF_CONV
            part = cc // GROUP_W
            dst = (qg_s, kg_s, vg_s)[part]
            cw = convw_ref[:, cc:cc + TILE]
            acc = res * cw[CONV_W - 1:CONV_W]
            if prompt:
                for c in chunks:
                    ext_s[c, 8:, cc:cc + TILE] = res[rows_of(c)]
                for s in range(1, CONV_W):
                    sh = jnp.concatenate([ext_s[c, 8 - s:8 - s + ROWS, cc:cc + TILE] for c in chunks], axis=0)
                    acc = acc + sh * cw[CONV_W - 1 - s:CONV_W - s]
                for c in chunks:
                    tail = ext_s[c, ROWS:, cc:cc + TILE]
                    ext_s[c, 0:8, cc:cc + TILE] = tail
                    conv_out[c * 8:(c + 1) * 8, cc:cc + TILE] = tail
            else:
                st8 = st8_ref[:, cc:cc + TILE]
                for s in range(1, CONV_W):
                    sh = jnp.where(t_in_group < s, pltpu.roll(st8, lb - 8 + s, 0), pltpu.roll(res, s, 0))
                    acc = acc + sh * cw[CONV_W - 1 - s:CONV_W - s]
                conv_out[:, cc:cc + TILE] = res
            conv = _silu(acc)
            for j in range(TILE // HEAD_DIM):
                c = conv[:, j * HEAD_DIM:(j + 1) * HEAD_DIM]
                if part < 2:
                    c = c * lax.rsqrt(jnp.sum(c * c, axis=-1, keepdims=True) + EPS)
                if part == 0:
                    c = c * (HEAD_DIM ** -0.5)
                d0 = cc - part * GROUP_W + j * HEAD_DIM
                dst[:, d0:d0 + HEAD_DIM] = c

    pv = pvec_ref[...]
    a_plus = ab + pv[1:2]
    softplus = jnp.maximum(a_plus, 0.0) + jnp.log1p(jnp.exp(-jnp.abs(a_plus)))
    g = -jnp.exp(pv[0:1]) * softplus
    beta_s[...] = 1.0 / (1.0 + jnp.exp(-ab))

    row = lax.broadcasted_iota(jnp.int32, (ROWS, ROWS), 0)
    col = lax.broadcasted_iota(jnp.int32, (ROWS, ROWS), 1)
    same = (row // group) == (col // group)
    causal = same & (row >= col)
    strict = same & (row > col)
    sum_mat = _bf(jnp.concatenate([causal.astype(F32), same.astype(F32)], axis=0))
    sum_mat3 = jnp.concatenate([sum_mat] * 3, axis=1)
    for c in chunks:
        r = _dot(sum_mat3, jnp.concatenate(_split(g[rows_of(c)], 3), axis=0))
        cum_s[rows_of(c), :] = r[:ROWS]
        tot_s[rows_of(c), :] = r[ROWS:]
    sel = _bf((lax.broadcasted_iota(jnp.int32, (8, LANES), 0)
               == lax.broadcasted_iota(jnp.int32, (8, LANES), 1)).astype(F32))
    sel3 = jnp.concatenate([sel] * 3, axis=1)

    items = [(c, hh) for c in chunks for hh in range(HEADS)]
    cumc = [cum_s[rows_of(c), :] for c in chunks]
    totc = [tot_s[rows_of(c), :] for c in chunks]
    betac = [beta_s[rows_of(c), :] for c in chunks]
    ecumc = [jnp.exp(x) for x in cumc]
    tailc = [jnp.exp(t - x) for t, x in zip(totc, cumc)]
    cumrow = [_dot_nt(sel3, jnp.concatenate(_split(x, 3), axis=1)) for x in cumc]

    for c, hh in items:
        rows, sl = rows_of(c), head_sl(hh)
        q, k = qr_s[rows, sl], kr_s[rows, sl]
        pr_s[c * HEADS + hh] = _mm(q, k, nt=True) * dmat_ref[hh]
        qr_s[rows, sl] = q * rtab_ref[0, :, sl]
        kr_s[rows, sl] = k * rtab_ref[1, :, sl]

    qk_kk, dmats = [], []
    for c, hh in items:
        rows, sl = rows_of(c), head_sl(hh)
        kb = _bf(kg_s[rows, sl])
        qk_kk.append(_dot_nt(jnp.concatenate([_bf(qg_s[rows, sl]), kb], axis=0), kb))
        diff = lane(cumc[c], hh) - cumrow[c][hh:hh + 1, :]
        dmats.append(jnp.where(causal, jnp.exp(jnp.minimum(diff, 0.0)), 0.0))
    a_mats = [jnp.where(strict, lane(betac[c], HEADS + hh) * x[ROWS:] * d, 0.0)
              for (c, hh), x, d in zip(items, qk_kk, dmats)]
    t_inv, eye = _tri_inverse_many(a_mats, group, row, col)
    for i, (c, hh) in enumerate(items):
        rows, sl = rows_of(c), head_sl(hh)
        q, k, v = qg_s[rows, sl], kg_s[rows, sl], vg_s[rows, sl]
        beta_h = lane(betac[c], HEADS + hh)
        rhs = jnp.concatenate([k * (beta_h * lane(ecumc[c], hh)), v * beta_h], axis=1)
        sol = rhs + _mm(t_inv[i] - eye, rhs)
        w_s[rows, sl] = sol[:, :HEAD_DIM]
        u_s[rows, sl] = sol[:, HEAD_DIM:]
        pg_s[c * HEADS + hh] = qk_kk[i][:ROWS] * dmats[i]
        qg_s[rows, sl] = q * lane(ecumc[c], hh)
        kg_s[rows, sl] = k * lane(tailc[c], hh)

    def state_refs(c, hh, gi):
        if prompt:
            return (sret_s.at[c, hh], sret_s.at[c, hh], sgdn_s.at[c, hh], sgdn_s.at[c, hh])
        n = c * ngroups + gi
        return (sret_in.at[n, hh], sret_out.at[n, hh], sgdn_in.at[n, hh], sgdn_out.at[n, hh])

    for c in chunks:
        dtotc = jnp.exp(totc[c])
        qs_r, qs_g, ws_g = {}, {}, {}
        for hh in range(HEADS):
            sl = head_sl(hh)
            for gi in range(ngroups):
                rs = slice(c * ROWS + gi * group, c * ROWS + (gi + 1) * group)
                r_in, _, g_in, _ = state_refs(c, hh, gi)
                qs_r[hh, gi] = _mm(qr_s[rs, sl], r_in[...])
                r = _mm(jnp.concatenate([qg_s[rs, sl], w_s[rs, sl]], axis=0), g_in[...])
                qs_g[hh, gi], ws_g[hh, gi] = r[:group], r[group:]
        for hh in range(HEADS):
            rows, sl = rows_of(c), head_sl(hh)
            i = c * HEADS + hh
            vb = vb_s[rows, sl]
            v_new = u_s[rows, sl] - cat([ws_g[hh, gi] for gi in range(ngroups)])
            vnb = _bf(v_new)
            o_s[rows, sl] = cat([qs_r[hh, gi] for gi in range(ngroups)]) + _mm(pr_s[i], vb)
            o_s[rows, GROUP_W + hh * HEAD_DIM:GROUP_W + (hh + 1) * HEAD_DIM] = (
                cat([qs_g[hh, gi] for gi in range(ngroups)]) + _mm(pg_s[i], vnb))
            for gi in range(ngroups):
                rs = slice(c * ROWS + gi * group, c * ROWS + (gi + 1) * group)
                ls = slice(gi * group, (gi + 1) * group)
                r_in, r_out, g_in, g_out = state_refs(c, hh, gi)
                r_out[...] = cfg["ret_dtot"][hh] * r_in[...] + _mm(kr_s[rs, sl].T, vb[ls])
                dec = dtotc[gi * group:gi * group + 1, hh:hh + 1]
                g_out[...] = dec * g_in[...] + _mm(kg_s[rs, sl].T, vnb[ls])

    if not deferred:
        head_norm()


def _post_kernel(mix_ref, x_ref, wout_ref, postw_ref, premlpw_ref, wup_ref, wdown_ref, postmlpw_ref, out_ref):
    tm = x_ref.shape[0]
    piece = POST_PIECE if tm % POST_PIECE == 0 else tm
    halves = [slice(r, r + piece) for r in range(0, tm, piece)]
    m = [_dot(mix_ref[r, :], wout_ref[...]) for r in halves]
    x1 = [x_ref[r, :] + _rms(mi, postw_ref[...]) for r, mi in zip(halves, m)]
    h2 = [_bf(_rms(xi, premlpw_ref[...])) for xi in x1]
    acc = [None] * len(halves)
    for c in range(D_FF // D_MODEL):
        for i in range(len(halves)):
            f = _dot(h2[i], wup_ref[:, c * D_MODEL:(c + 1) * D_MODEL])
            f = jnp.square(jnp.maximum(f, 0.0))
            part = _dot(_bf(f), wdown_ref[c * D_MODEL:(c + 1) * D_MODEL, :])
            acc[i] = part if acc[i] is None else acc[i] + part
    for r, xi, ai in zip(halves, x1, acc):
        out_ref[r, :] = xi + _rms(ai, postmlpw_ref[...])


def _const_spec(shape, single=False):
    index_map = lambda *_: (0,) * len(shape)
    if single:
        return pl.BlockSpec(shape, index_map, pipeline_mode=pl.Buffered(1))
    return pl.BlockSpec(shape, index_map)


def _retention_tables(group):
    log_gamma = np.log(1.0 - 2.0 ** (-5.0 - np.arange(HEADS, dtype=np.float64)))
    t = np.arange(ROWS) % group
    same = (np.arange(ROWS)[:, None] // group) == (np.arange(ROWS)[None, :] // group)
    causal = same & (t[:, None] >= t[None, :])
    dmat = np.where(causal[None], np.exp((t[:, None] - t[None, :])[None] * log_gamma[:, None, None]), 0.0)
    ecum = np.exp((t[:, None] + 1.0) * log_gamma[None, :])
    tail = np.exp((group - 1.0 - t[:, None]) * log_gamma[None, :])
    rtab = np.stack([np.repeat(ecum, HEAD_DIM, axis=1), np.repeat(tail, HEAD_DIM, axis=1)])
    dtot = tuple(float(np.float32(v)) for v in np.exp(group * log_gamma))
    return jnp.asarray(dmat, F32), jnp.asarray(rtab, F32), dtot


def _rope_tables(length, offset):
    inv_freq = ROPE_BASE ** (-np.arange(0, HEAD_DIM, 2, dtype=np.float64) / HEAD_DIM)
    ang = (np.arange(length, dtype=np.float64) + offset)[:, None] * inv_freq[None, :]
    cos, sin = np.cos(ang), np.sin(ang)
    return (jnp.asarray(np.concatenate([cos, cos], axis=-1), F32),
            jnp.asarray(np.concatenate([-sin, sin], axis=-1), F32))


def _mixer(x, params, *, prompt, states=None):
    b, l, _ = x.shape
    t = b * l
    if prompt:
        group, lb = ROWS, b * ROWS
        assert l % ROWS == 0
        cos, sin = _rope_tables(l, 0)
        nsteps = l // ROWS
        grid = (nsteps + 1,)
        work_blk = lambda j: jnp.minimum(j, nsteps - 1)
        x_in = x
        x_spec = pl.BlockSpec((b, ROWS, D_MODEL), lambda j: (0, work_blk(j), 0))
        pos_spec = pl.BlockSpec((ROWS, HEAD_DIM), lambda j: (work_blk(j), 0))
    else:
        group = l
        lb = min(2 * ROWS, t)
        assert ROWS % group == 0 and group == 8 and t % lb == 0 and lb % ROWS == 0
        cos, sin = _rope_tables(l, PAST_LEN)
        cos, sin = jnp.tile(cos, (lb // l, 1)), jnp.tile(sin, (lb // l, 1))
        grid = (t // lb,)
        row_map = lambda i: (i, 0)
        x_in = x.reshape(t, D_MODEL)
        x_spec = pl.BlockSpec((lb, D_MODEL), row_map)
        pos_spec = pl.BlockSpec((lb, HEAD_DIM), lambda i: (0, 0))
    nseq = lb // group
    dmat, rtab, dtot = _retention_tables(group)
    cfg = dict(lb=lb, group=group, prompt=prompt, ret_dtot=dtot)

    in_specs = [
        x_spec,
        _const_spec((1, D_MODEL)),
        _const_spec((D_MODEL, IN_W), single=True),
        _const_spec((D_MODEL, LANES), single=True),
        pos_spec,
        pos_spec,
        _const_spec((HEADS, ROWS, ROWS)),
        _const_spec((2, ROWS, GROUP_W)),
        _const_spec((CONV_W, CONV_CH)),
        _const_spec((8, LANES)),
        _const_spec((1, 2 * GROUP_W)),
    ]
    args = [x_in, params["pre_w"], params["w_in"], params["w_ab"], cos, sin, dmat, rtab,
            params["conv_w"], params["pvec"], params["normw"]]
    state_shape = (HEADS, HEAD_DIM, HEAD_DIM)
    if prompt:
        out_specs = [
            pl.BlockSpec((b, ROWS, 2 * GROUP_W), lambda j: (0, jnp.maximum(j - 1, 0), 0)),
            _const_spec((b,) + state_shape),
            _const_spec((b,) + state_shape),
            _const_spec((b * 8, CONV_CH)),
        ]
        out_shape = [
            jax.ShapeDtypeStruct((b, l, 2 * GROUP_W), BF16),
            jax.ShapeDtypeStruct((b,) + state_shape, F32),
            jax.ShapeDtypeStruct((b,) + state_shape, F32),
            jax.ShapeDtypeStruct((b * 8, CONV_CH), F32),
        ]
    else:
        s_ret, s_gdn, s_conv = states
        st8 = jnp.pad(s_conv.astype(F32), ((0, 0), (8 - (CONV_W - 1), 0), (0, 0))).reshape(t, CONV_CH)
        in_specs += [
            pl.BlockSpec((nseq,) + state_shape, lambda i: (i, 0, 0, 0)),
            pl.BlockSpec((nseq,) + state_shape, lambda i: (i, 0, 0, 0)),
            pl.BlockSpec((lb, CONV_CH), row_map),
        ]
        args += [s_ret.astype(F32), s_gdn.astype(F32), st8]
        out_specs = [
            pl.BlockSpec((lb, 2 * GROUP_W), row_map),
            pl.BlockSpec((nseq,) + state_shape, lambda i: (i, 0, 0, 0)),
            pl.BlockSpec((nseq,) + state_shape, lambda i: (i, 0, 0, 0)),
            pl.BlockSpec((lb, CONV_CH), row_map),
        ]
        out_shape = [
            jax.ShapeDtypeStruct((t, 2 * GROUP_W), BF16),
            jax.ShapeDtypeStruct((b,) + state_shape, F32),
            jax.ShapeDtypeStruct((b,) + state_shape, F32),
            jax.ShapeDtypeStruct((t, CONV_CH), F32),
        ]
    nmat = lb // ROWS * HEADS
    scratch = [
        pltpu.VMEM((lb, GROUP_W), F32), pltpu.VMEM((lb, GROUP_W), F32),
        pltpu.VMEM((lb, GROUP_W), F32), pltpu.VMEM((lb, GROUP_W), F32), pltpu.VMEM((lb, GROUP_W), F32),
        pltpu.VMEM((lb, LANES), F32), pltpu.VMEM((lb, LANES), F32), pltpu.VMEM((lb, LANES), F32),
        pltpu.VMEM((lb, 2 * GROUP_W), F32),
        pltpu.VMEM((lb, GROUP_W), F32), pltpu.VMEM((lb, GROUP_W), F32),
        pltpu.VMEM((nmat, ROWS, ROWS), F32), pltpu.VMEM((nmat, ROWS, ROWS), F32),
        pltpu.VMEM((lb, GROUP_W), BF16), pltpu.VMEM((lb, 2 * GROUP_W), F32),
    ]
    if prompt:
        scratch += [pltpu.VMEM((b,) + state_shape, F32), pltpu.VMEM((b,) + state_shape, F32),
                    pltpu.VMEM((b, 8 + ROWS, CONV_CH), F32)]

    mix, s_ret_new, s_gdn_new, conv_rows = pl.pallas_call(
        functools.partial(_mixer_kernel, cfg),
        grid=grid,
        in_specs=in_specs,
        out_specs=out_specs,
        out_shape=out_shape,
        scratch_shapes=scratch,
        compiler_params=pltpu.CompilerParams(dimension_semantics=("arbitrary",), vmem_limit_bytes=VMEM_LIMIT),
        name="mixer_prompt" if prompt else "mixer_sample",
    )(*args)
    new_conv = conv_rows.reshape(b, 8, CONV_CH)[:, 8 - (CONV_W - 1):]
    return mix.reshape(t, 2 * GROUP_W), s_ret_new, s_gdn_new, new_conv


def _post(mix, x, params):
    b, l, _ = x.shape
    t = b * l
    tm = min(POST_ROWS, t)
    assert t % tm == 0
    rows = pl.BlockSpec((tm, D_MODEL), lambda i: (i, 0))
    y = pl.pallas_call(
        _post_kernel,
        grid=(t // tm,),
        in_specs=[rows, rows, _const_spec((2 * GROUP_W, D_MODEL), single=True), _const_spec((1, D_MODEL)),
                  _const_spec((1, D_MODEL)), _const_spec((D_MODEL, D_FF), single=True),
                  _const_spec((D_FF, D_MODEL), single=True), _const_spec((1, D_MODEL))],
        out_specs=rows,
        out_shape=jax.ShapeDtypeStruct((t, D_MODEL), F32),
        compiler_params=pltpu.CompilerParams(dimension_semantics=("arbitrary",), vmem_limit_bytes=VMEM_LIMIT),
        name="post",
    )(mix, x.reshape(t, D_MODEL), params["w_out"], params["post_w"], params["pre_mlp_w"],
      params["w_up"], params["w_down"], params["post_mlp_w"])
    return y.reshape(b, l, D_MODEL)


def _layer_params(l, pre_mix_w, w_in, conv_w, A_log, dt_bias, ret_norm_w, gdn_norm_w, w_out, post_mix_w,
                  pre_mlp_w, w_up, w_down, post_mlp_w):
    w = _bf(w_in[l])
    pvec = jnp.zeros((8, LANES), F32)
    pvec = pvec.at[0, :HEADS].set(A_log[l].astype(F32)).at[1, :HEADS].set(dt_bias[l].astype(F32))
    return dict(
        pre_w=pre_mix_w[l].astype(F32)[None],
        w_in=w,
        w_ab=jnp.pad(w[:, MAIN_W:], ((0, 0), (0, LANES - 2 * HEADS))),
        conv_w=conv_w[l].astype(F32),
        pvec=pvec,
        normw=jnp.concatenate([jnp.tile(ret_norm_w[l], HEADS), jnp.tile(gdn_norm_w[l], HEADS)]).astype(F32)[None],
        w_out=_bf(w_out[l]),
        post_w=post_mix_w[l].astype(F32)[None],
        pre_mlp_w=pre_mlp_w[l].astype(F32)[None],
        w_up=_bf(w_up[l]),
        w_down=_bf(w_down[l]),
        post_mlp_w=post_mlp_w[l].astype(F32)[None],
    )


def kernel(x_prompt, x_sample, state_ret, state_gdn, state_conv, pre_mix_w, w_in, conv_w, A_log, dt_bias,
           ret_norm_w, gdn_norm_w, w_out, post_mix_w, pre_mlp_w, w_up, w_down, post_mlp_w):
    depth = w_in.shape[0]
    yp, ys = x_prompt, x_sample
    outs = [[] for _ in range(6)]
    for l in range(depth):
        params = _layer_params(l, pre_mix_w, w_in, conv_w, A_log, dt_bias, ret_norm_w, gdn_norm_w, w_out,
                               post_mix_w, pre_mlp_w, w_up, w_down, post_mlp_w)
        mix_p, rp, gp, cp = _mixer(yp, params, prompt=True)
        yp = _post(mix_p, yp, params)
        mix_s, rs, gs, cs = _mixer(ys, params, prompt=False,
                                   states=(state_ret[l], state_gdn[l], state_conv[l]))
        ys = _post(mix_s, ys, params)
        for dst, val, like in zip(outs, (rp, gp, cp, rs, gs, cs),
                                  (state_ret, state_gdn, state_conv) * 2):
            dst.append(val.astype(like.dtype))
    return (yp, ys) + tuple(jnp.stack(o) for o in outs)
```

```python
import functools

import numpy as np
import jax
import jax.numpy as jnp
from jax import lax
from jax.experimental import pallas as pl
from jax.experimental.pallas import tpu as pltpu

D_MODEL = 1024
HEADS = 4
HEAD_DIM = 128
GROUP_W = HEADS * HEAD_DIM
CONV_CH = 3 * GROUP_W
CONV_W = 4
D_FF = 4 * D_MODEL
MAIN_W = 4 * GROUP_W + CONV_CH + GROUP_W
IN_W = MAIN_W + 2 * HEADS
ROPE_BASE = 10000.0
EPS = 1e-6
PAST_LEN = 16384

ROWS = 64
BASE = 8
LANES = 128
SUBLANES = 8
TILE = 256
POST_ROWS = 1024
POST_PIECE = 256
VMEM_LIMIT = 58 * 1024 * 1024

OFF_RQ, OFF_RK, OFF_RV, OFF_RG = 0, GROUP_W, 2 * GROUP_W, 3 * GROUP_W
OFF_CONV = 4 * GROUP_W
OFF_Z = OFF_CONV + CONV_CH

F32 = jnp.float32
BF16 = jnp.bfloat16


def _bf(x):
    return x.astype(BF16)


def _dot(a, b):
    return jnp.dot(a, b, preferred_element_type=F32)


def _dot_nt(a, b):
    return lax.dot_general(a, b, (((1,), (1,)), ((), ())), preferred_element_type=F32)


def _split(x, n):
    parts, r = [], x
    for i in range(n):
        p = r.astype(BF16)
        parts.append(p)
        if i + 1 < n:
            r = r - p.astype(F32)
    return parts


def _mm(a, b, nt=False):
    return _dot_nt(_bf(a), _bf(b)) if nt else _dot(_bf(a), _bf(b))


def _rms(x, w):
    return x * lax.rsqrt(jnp.mean(x * x, axis=-1, keepdims=True) + EPS) * w


def _silu(x):
    return x / (1.0 + jnp.exp(-x))


def _tri_inverse_many(mats, group, row, col):
    n = mats[0].shape[0]
    packed = [jnp.concatenate(mats[i:i + HEADS], axis=1) for i in range(0, len(mats), HEADS)]
    prow = lax.broadcasted_iota(jnp.int32, (n, HEADS * n), 0)
    plane = lax.broadcasted_iota(jnp.int32, (n, HEADS * n), 1)
    pcol = plane % n
    seg = [_bf(((plane // n) == hh).astype(F32)) for hh in range(HEADS)]

    def blockdiag(y):
        return jnp.concatenate([y * m for m in seg], axis=0)

    def mm3(x, y):
        (x_hi, x_lo), (y_hi, y_lo) = _split(x, 2), _split(y, 2)
        both = _dot(jnp.concatenate([x_hi, x_lo], axis=0), blockdiag(y_hi))
        return both[:n] + both[n:] + _dot(x_hi, blockdiag(y_lo))

    eye_p = (prow == pcol).astype(F32)
    blk = (prow // BASE) == (pcol // BASE)
    n8 = [jnp.where(blk, -a, 0.0) for a in packed]
    p2 = [mm3(x, x) for x in n8]
    np2 = [mm3(x, p) for x, p in zip(n8, p2)]
    p4 = [mm3(p, p) for p in p2]
    e = [x + p + y for x, p, y in zip(n8, p2, np2)]
    ep4 = [mm3(x, p) for x, p in zip(e, p4)]
    t = [x + p + y + eye_p for x, p, y in zip(e, p4, ep4)]
    s = BASE
    while s < group:
        off = ((prow // (2 * s)) == (pcol // (2 * s))) & ((prow // s) != (pcol // s))
        a_off = [jnp.where(off, a, 0.0) for a in packed]
        ta = [mm3(x, y) for x, y in zip(t, a_off)]
        tat = [mm3(x, y) for x, y in zip(ta, t)]
        t = [x - y for x, y in zip(t, tat)]
        s *= 2
    eye = (row == col).astype(F32)
    return [tp[:, hh * n:(hh + 1) * n] for tp in t for hh in range(HEADS)], eye


def _mixer_kernel(cfg, *refs):
    lb, group, prompt = cfg["lb"], cfg["group"], cfg["prompt"]
    ngroups = ROWS // group
    nchunk = lb // ROWS
    chunks = list(range(nchunk))
    it = iter(refs)
    x_ref, prew_ref, win_ref, wab_ref, cos_ref, sin_ref = (next(it) for _ in range(6))
    dmat_ref, rtab_ref, convw_ref, pvec_ref, normw_ref = (next(it) for _ in range(5))
    if not prompt:
        sret_in, sgdn_in, st8_ref = (next(it) for _ in range(3))
    mix_ref, sret_out, sgdn_out, conv_out = (next(it) for _ in range(4))
    qr_s, kr_s, qg_s, kg_s, vg_s, cum_s, tot_s, beta_s, o_s = (next(it) for _ in range(9))
    w_s, u_s, pr_s, pg_s, vb_s, gate_s = (next(it) for _ in range(6))
    if prompt:
        sret_s, sgdn_s, ext_s = (next(it) for _ in range(3))

        @pl.when(pl.program_id(0) == 0)
        def _():
            sret_s[...] = jnp.zeros_like(sret_s)
            sgdn_s[...] = jnp.zeros_like(sgdn_s)
            ext_s[:, 0:SUBLANES, :] = jnp.zeros((nchunk, SUBLANES, CONV_CH), F32)

    rows_of = lambda c: slice(c * ROWS, (c + 1) * ROWS)
    head_sl = lambda hh: slice(hh * HEAD_DIM, (hh + 1) * HEAD_DIM)
    lane = lambda x, n: x[:, n:n + 1]
    cat = lambda xs: xs[0] if len(xs) == 1 else jnp.concatenate(xs, axis=0)

    x = x_ref[...].reshape(lb, D_MODEL)
    h = _bf(_rms(x, prew_ref[...]))
    ab = _dot(h, wab_ref[...])
    cos, sin = cos_ref[...], sin_ref[...]
    if prompt:
        cos, sin = jnp.concatenate([cos] * nchunk, axis=0), jnp.concatenate([sin] * nchunk, axis=0)
    t_in_group = lax.broadcasted_iota(jnp.int32, (lb, 1), 0) % group

    for c0 in range(0, MAIN_W, TILE):
        res = _dot(h, win_ref[:, c0:c0 + TILE])
        tiles = [res[:, j * HEAD_DIM:(j + 1) * HEAD_DIM] for j in range(TILE // HEAD_DIM)]
        if c0 < OFF_RV:
            for j, tl in enumerate(tiles):
                rot = tl * cos + pltpu.roll(tl, HEAD_DIM // 2, 1) * sin
                if c0 < OFF_RK:
                    qr_s[:, c0 - OFF_RQ + j * HEAD_DIM:c0 - OFF_RQ + (j + 1) * HEAD_DIM] = rot
                else:
                    kr_s[:, c0 - OFF_RK + j * HEAD_DIM:c0 - OFF_RK + (j + 1) * HEAD_DIM] = rot * (HEAD_DIM ** -0.5)
        elif c0 < OFF_RG:
            vb_s[:, c0 - OFF_RV:c0 - OFF_RV + TILE] = _bf(res)
        elif c0 < OFF_CONV:
            gate_s[:, c0 - OFF_RG:c0 - OFF_RG + TILE] = _silu(res)
        elif c0 >= OFF_Z:
            gate_s[:, GROUP_W + c0 - OFF_Z:GROUP_W + c0 - OFF_Z + TILE] = _silu(res)
        else:
            cc = c0 - OFF_CONV
            part = cc // GROUP_W
            dst = (qg_s, kg_s, vg_s)[part]
            cw = convw_ref[:, cc:cc + TILE]
            acc = res * cw[CONV_W - 1:CONV_W]
            if prompt:
                for c in chunks:
                    ext_s[c, SUBLANES:, cc:cc + TILE] = res[rows_of(c)]
                for s in range(1, CONV_W):
                    sh = jnp.concatenate([ext_s[c, SUBLANES - s:SUBLANES - s + ROWS, cc:cc + TILE] for c in chunks],
                                         axis=0)
                    acc = acc + sh * cw[CONV_W - 1 - s:CONV_W - s]
                for c in chunks:
                    tail = ext_s[c, ROWS:, cc:cc + TILE]
                    ext_s[c, 0:SUBLANES, cc:cc + TILE] = tail
                    conv_out[c * SUBLANES:(c + 1) * SUBLANES, cc:cc + TILE] = tail
            else:
                st8 = st8_ref[:, cc:cc + TILE]
                for s in range(1, CONV_W):
                    sh = jnp.where(t_in_group < s, pltpu.roll(st8, lb - SUBLANES + s, 0), pltpu.roll(res, s, 0))
                    acc = acc + sh * cw[CONV_W - 1 - s:CONV_W - s]
                conv_out[:, cc:cc + TILE] = res
            conv = _silu(acc)
            for j in range(TILE // HEAD_DIM):
                c = conv[:, j * HEAD_DIM:(j + 1) * HEAD_DIM]
                if part < 2:
                    c = c * lax.rsqrt(jnp.sum(c * c, axis=-1, keepdims=True) + EPS)
                if part == 0:
                    c = c * (HEAD_DIM ** -0.5)
                d0 = cc - part * GROUP_W + j * HEAD_DIM
                dst[:, d0:d0 + HEAD_DIM] = c

    pv = pvec_ref[...]
    a_plus = ab + pv[1:2]
    softplus = jnp.maximum(a_plus, 0.0) + jnp.log1p(jnp.exp(-jnp.abs(a_plus)))
    g = -jnp.exp(pv[0:1]) * softplus
    beta_s[...] = 1.0 / (1.0 + jnp.exp(-ab))

    row = lax.broadcasted_iota(jnp.int32, (ROWS, ROWS), 0)
    col = lax.broadcasted_iota(jnp.int32, (ROWS, ROWS), 1)
    same = (row // group) == (col // group)
    causal = same & (row >= col)
    strict = same & (row > col)
    sum_mat = _bf(jnp.concatenate([causal.astype(F32), same.astype(F32)], axis=0))
    sum_mat3 = jnp.concatenate([sum_mat] * 3, axis=1)
    for c in chunks:
        r = _dot(sum_mat3, jnp.concatenate(_split(g[rows_of(c)], 3), axis=0))
        cum_s[rows_of(c), :] = r[:ROWS]
        tot_s[rows_of(c), :] = r[ROWS:]
    sel = _bf((lax.broadcasted_iota(jnp.int32, (SUBLANES, LANES), 0)
               == lax.broadcasted_iota(jnp.int32, (SUBLANES, LANES), 1)).astype(F32))
    sel3 = jnp.concatenate([sel] * 3, axis=1)

    items = [(c, hh) for c in chunks for hh in range(HEADS)]
    cumc = [cum_s[rows_of(c), :] for c in chunks]
    totc = [tot_s[rows_of(c), :] for c in chunks]
    betac = [beta_s[rows_of(c), :] for c in chunks]
    ecumc = [jnp.exp(x) for x in cumc]
    tailc = [jnp.exp(t - x) for t, x in zip(totc, cumc)]
    cumrow = [_dot_nt(sel3, jnp.concatenate(_split(x, 3), axis=1)) for x in cumc]

    for c, hh in items:
        rows, sl = rows_of(c), head_sl(hh)
        q, k = qr_s[rows, sl], kr_s[rows, sl]
        pr_s[c * HEADS + hh] = _mm(q, k, nt=True) * dmat_ref[hh]
        qr_s[rows, sl] = q * rtab_ref[0, :, sl]
        kr_s[rows, sl] = k * rtab_ref[1, :, sl]

    qk_kk, dmats = [], []
    for c, hh in items:
        rows, sl = rows_of(c), head_sl(hh)
        kb = _bf(kg_s[rows, sl])
        qk_kk.append(_dot_nt(jnp.concatenate([_bf(qg_s[rows, sl]), kb], axis=0), kb))
        diff = lane(cumc[c], hh) - cumrow[c][hh:hh + 1, :]
        dmats.append(jnp.where(causal, jnp.exp(jnp.minimum(diff, 0.0)), 0.0))
    a_mats = [jnp.where(strict, lane(betac[c], HEADS + hh) * x[ROWS:] * d, 0.0)
              for (c, hh), x, d in zip(items, qk_kk, dmats)]
    t_inv, eye = _tri_inverse_many(a_mats, group, row, col)
    for i, (c, hh) in enumerate(items):
        rows, sl = rows_of(c), head_sl(hh)
        q, k, v = qg_s[rows, sl], kg_s[rows, sl], vg_s[rows, sl]
        beta_h = lane(betac[c], HEADS + hh)
        rhs = jnp.concatenate([k * (beta_h * lane(ecumc[c], hh)), v * beta_h], axis=1)
        sol = rhs + _mm(t_inv[i] - eye, rhs)
        w_s[rows, sl] = sol[:, :HEAD_DIM]
        u_s[rows, sl] = sol[:, HEAD_DIM:]
        pg_s[c * HEADS + hh] = qk_kk[i][:ROWS] * dmats[i]
        qg_s[rows, sl] = q * lane(ecumc[c], hh)
        kg_s[rows, sl] = k * lane(tailc[c], hh)

    def state_refs(c, hh, gi):
        if prompt:
            return (sret_s.at[c, hh], sret_s.at[c, hh], sgdn_s.at[c, hh], sgdn_s.at[c, hh])
        n = c * ngroups + gi
        return (sret_in.at[n, hh], sret_out.at[n, hh], sgdn_in.at[n, hh], sgdn_out.at[n, hh])

    for c in chunks:
        dtotc = jnp.exp(totc[c])
        qs_r, qs_g, ws_g = {}, {}, {}
        for hh in range(HEADS):
            sl = head_sl(hh)
            for gi in range(ngroups):
                rs = slice(c * ROWS + gi * group, c * ROWS + (gi + 1) * group)
                r_in, _, g_in, _ = state_refs(c, hh, gi)
                qs_r[hh, gi] = _mm(qr_s[rs, sl], r_in[...])
                r = _mm(jnp.concatenate([qg_s[rs, sl], w_s[rs, sl]], axis=0), g_in[...])
                qs_g[hh, gi], ws_g[hh, gi] = r[:group], r[group:]
        for hh in range(HEADS):
            rows, sl = rows_of(c), head_sl(hh)
            i = c * HEADS + hh
            vb = vb_s[rows, sl]
            v_new = u_s[rows, sl] - cat([ws_g[hh, gi] for gi in range(ngroups)])
            vnb = _bf(v_new)
            o_s[rows, sl] = cat([qs_r[hh, gi] for gi in range(ngroups)]) + _mm(pr_s[i], vb)
            o_s[rows, GROUP_W + hh * HEAD_DIM:GROUP_W + (hh + 1) * HEAD_DIM] = (
                cat([qs_g[hh, gi] for gi in range(ngroups)]) + _mm(pg_s[i], vnb))
            for gi in range(ngroups):
                rs = slice(c * ROWS + gi * group, c * ROWS + (gi + 1) * group)
                ls = slice(gi * group, (gi + 1) * group)
                r_in, r_out, g_in, g_out = state_refs(c, hh, gi)
                r_out[...] = cfg["ret_dtot"][hh] * r_in[...] + _mm(kr_s[rs, sl].T, vb[ls])
                dec = dtotc[gi * group:gi * group + 1, hh:hh + 1]
                g_out[...] = dec * g_in[...] + _mm(kg_s[rs, sl].T, vnb[ls])

    for hh in range(2 * HEADS):
        sl = head_sl(hh)
        y = _rms(o_s[:, sl], normw_ref[:, sl]) * gate_s[:, sl]
        y = y.astype(mix_ref.dtype)
        if prompt:
            mix_ref[:, :, sl] = y.reshape(nchunk, ROWS, HEAD_DIM)
        else:
            mix_ref[:, sl] = y

    if prompt:
        @pl.when(pl.program_id(0) == pl.num_programs(0) - 1)
        def _():
            sret_out[...] = sret_s[...]
            sgdn_out[...] = sgdn_s[...]


def _post_kernel(mix_ref, x_ref, wout_ref, postw_ref, premlpw_ref, wup_ref, wdown_ref, postmlpw_ref, out_ref):
    tm = x_ref.shape[0]
    piece = POST_PIECE if tm % POST_PIECE == 0 else tm
    halves = [slice(r, r + piece) for r in range(0, tm, piece)]
    m = [_dot(mix_ref[r, :], wout_ref[...]) for r in halves]
    x1 = [x_ref[r, :] + _rms(mi, postw_ref[...]) for r, mi in zip(halves, m)]
    h2 = [_bf(_rms(xi, premlpw_ref[...])) for xi in x1]
    acc = [None] * len(halves)
    for c in range(D_FF // D_MODEL):
        for i in range(len(halves)):
            f = _dot(h2[i], wup_ref[:, c * D_MODEL:(c + 1) * D_MODEL])
            f = jnp.square(jnp.maximum(f, 0.0))
            part = _dot(_bf(f), wdown_ref[c * D_MODEL:(c + 1) * D_MODEL, :])
            acc[i] = part if acc[i] is None else acc[i] + part
    for r, xi, ai in zip(halves, x1, acc):
        out_ref[r, :] = xi + _rms(ai, postmlpw_ref[...])


def _const_spec(shape, single=False):
    index_map = lambda *_: (0,) * len(shape)
    if single:
        return pl.BlockSpec(shape, index_map, pipeline_mode=pl.Buffered(1))
    return pl.BlockSpec(shape, index_map)


def _retention_tables(group):
    log_gamma = np.log(1.0 - 2.0 ** (-5.0 - np.arange(HEADS, dtype=np.float64)))
    t = np.arange(ROWS) % group
    same = (np.arange(ROWS)[:, None] // group) == (np.arange(ROWS)[None, :] // group)
    causal = same & (t[:, None] >= t[None, :])
    dmat = np.where(causal[None], np.exp((t[:, None] - t[None, :])[None] * log_gamma[:, None, None]), 0.0)
    ecum = np.exp((t[:, None] + 1.0) * log_gamma[None, :])
    tail = np.exp((group - 1.0 - t[:, None]) * log_gamma[None, :])
    rtab = np.stack([np.repeat(ecum, HEAD_DIM, axis=1), np.repeat(tail, HEAD_DIM, axis=1)])
    dtot = tuple(float(np.float32(v)) for v in np.exp(group * log_gamma))
    return jnp.asarray(dmat, F32), jnp.asarray(rtab, F32), dtot


def _rope_tables(length, offset):
    inv_freq = ROPE_BASE ** (-np.arange(0, HEAD_DIM, 2, dtype=np.float64) / HEAD_DIM)
    ang = (np.arange(length, dtype=np.float64) + offset)[:, None] * inv_freq[None, :]
    cos, sin = np.cos(ang), np.sin(ang)
    return (jnp.asarray(np.concatenate([cos, cos], axis=-1), F32),
            jnp.asarray(np.concatenate([-sin, sin], axis=-1), F32))


def _mixer(x, params, *, prompt, states=None):
    b, l, _ = x.shape
    t = b * l
    if prompt:
        group, lb = ROWS, b * ROWS
        assert l % ROWS == 0
        cos, sin = _rope_tables(l, 0)
        grid = (l // ROWS,)
        x_in = x
        x_spec = pl.BlockSpec((b, ROWS, D_MODEL), lambda j: (0, j, 0))
        pos_spec = pl.BlockSpec((ROWS, HEAD_DIM), lambda j: (j, 0))
    else:
        group = l
        lb = min(2 * ROWS, t)
        assert ROWS % group == 0 and group == SUBLANES and t % lb == 0 and lb % ROWS == 0
        cos, sin = _rope_tables(l, PAST_LEN)
        cos, sin = jnp.tile(cos, (lb // l, 1)), jnp.tile(sin, (lb // l, 1))
        grid = (t // lb,)
        row_map = lambda i: (i, 0)
        x_in = x.reshape(t, D_MODEL)
        x_spec = pl.BlockSpec((lb, D_MODEL), row_map)
        pos_spec = pl.BlockSpec((lb, HEAD_DIM), lambda i: (0, 0))
    nseq = lb // group
    dmat, rtab, dtot = _retention_tables(group)
    cfg = dict(lb=lb, group=group, prompt=prompt, ret_dtot=dtot)

    in_specs = [
        x_spec,
        _const_spec((1, D_MODEL)),
        _const_spec((D_MODEL, IN_W), single=True),
        _const_spec((D_MODEL, LANES), single=True),
        pos_spec,
        pos_spec,
        _const_spec((HEADS, ROWS, ROWS)),
        _const_spec((2, ROWS, GROUP_W)),
        _const_spec((CONV_W, CONV_CH)),
        _const_spec((SUBLANES, LANES)),
        _const_spec((1, 2 * GROUP_W)),
    ]
    args = [x_in, params["pre_w"], params["w_in"], params["w_ab"], cos, sin, dmat, rtab,
            params["conv_w"], params["pvec"], params["normw"]]
    state_shape = (HEADS, HEAD_DIM, HEAD_DIM)
    if prompt:
        out_specs = [
            pl.BlockSpec((b, ROWS, 2 * GROUP_W), lambda j: (0, j, 0)),
            _const_spec((b,) + state_shape),
            _const_spec((b,) + state_shape),
            _const_spec((b * SUBLANES, CONV_CH)),
        ]
        out_shape = [
            jax.ShapeDtypeStruct((b, l, 2 * GROUP_W), BF16),
            jax.ShapeDtypeStruct((b,) + state_shape, F32),
            jax.ShapeDtypeStruct((b,) + state_shape, F32),
            jax.ShapeDtypeStruct((b * SUBLANES, CONV_CH), F32),
        ]
    else:
        s_ret, s_gdn, s_conv = states
        st8 = jnp.pad(s_conv.astype(F32), ((0, 0), (SUBLANES - (CONV_W - 1), 0), (0, 0))).reshape(t, CONV_CH)
        in_specs += [
            pl.BlockSpec((nseq,) + state_shape, lambda i: (i, 0, 0, 0)),
            pl.BlockSpec((nseq,) + state_shape, lambda i: (i, 0, 0, 0)),
            pl.BlockSpec((lb, CONV_CH), row_map),
        ]
        args += [s_ret.astype(F32), s_gdn.astype(F32), st8]
        out_specs = [
            pl.BlockSpec((lb, 2 * GROUP_W), row_map),
            pl.BlockSpec((nseq,) + state_shape, lambda i: (i, 0, 0, 0)),
            pl.BlockSpec((nseq,) + state_shape, lambda i: (i, 0, 0, 0)),
            pl.BlockSpec((lb, CONV_CH), row_map),
        ]
        out_shape = [
            jax.ShapeDtypeStruct((t, 2 * GROUP_W), BF16),
            jax.ShapeDtypeStruct((b,) + state_shape, F32),
            jax.ShapeDtypeStruct((b,) + state_shape, F32),
            jax.ShapeDtypeStruct((t, CONV_CH), F32),
        ]
    nmat = lb // ROWS * HEADS
    scratch = [
        pltpu.VMEM((lb, GROUP_W), F32), pltpu.VMEM((lb, GROUP_W), F32),
        pltpu.VMEM((lb, GROUP_W), F32), pltpu.VMEM((lb, GROUP_W), F32), pltpu.VMEM((lb, GROUP_W), F32),
        pltpu.VMEM((lb, LANES), F32), pltpu.VMEM((lb, LANES), F32), pltpu.VMEM((lb, LANES), F32),
        pltpu.VMEM((lb, 2 * GROUP_W), F32),
        pltpu.VMEM((lb, GROUP_W), F32), pltpu.VMEM((lb, GROUP_W), F32),
        pltpu.VMEM((nmat, ROWS, ROWS), F32), pltpu.VMEM((nmat, ROWS, ROWS), F32),
        pltpu.VMEM((lb, GROUP_W), BF16), pltpu.VMEM((lb, 2 * GROUP_W), F32),
    ]
    if prompt:
        scratch += [pltpu.VMEM((b,) + state_shape, F32), pltpu.VMEM((b,) + state_shape, F32),
                    pltpu.VMEM((b, SUBLANES + ROWS, CONV_CH), F32)]

    mix, s_ret_new, s_gdn_new, conv_rows = pl.pallas_call(
        functools.partial(_mixer_kernel, cfg),
        grid=grid,
        in_specs=in_specs,
        out_specs=out_specs,
        out_shape=out_shape,
        scratch_shapes=scratch,
        compiler_params=pltpu.CompilerParams(dimension_semantics=("arbitrary",), vmem_limit_bytes=VMEM_LIMIT),
        name="mixer_prompt" if prompt else "mixer_sample",
    )(*args)
    new_conv = conv_rows.reshape(b, SUBLANES, CONV_CH)[:, SUBLANES - (CONV_W - 1):]
    return mix.reshape(t, 2 * GROUP_W), s_ret_new, s_gdn_new, new_conv


def _post(mix, x, params):
    b, l, _ = x.shape
    t = b * l
    tm = min(POST_ROWS, t)
    assert t % tm == 0
    rows = pl.BlockSpec((tm, D_MODEL), lambda i: (i, 0))
    y = pl.pallas_call(
        _post_kernel,
        grid=(t // tm,),
        in_specs=[rows, rows, _const_spec((2 * GROUP_W, D_MODEL), single=True), _const_spec((1, D_MODEL)),
                  _const_spec((1, D_MODEL)), _const_spec((D_MODEL, D_FF), single=True),
                  _const_spec((D_FF, D_MODEL), single=True), _const_spec((1, D_MODEL))],
        out_specs=rows,
        out_shape=jax.ShapeDtypeStruct((t, D_MODEL), F32),
        compiler_params=pltpu.CompilerParams(dimension_semantics=("arbitrary",), vmem_limit_bytes=VMEM_LIMIT),
        name="post",
    )(mix, x.reshape(t, D_MODEL), params["w_out"], params["post_w"], params["pre_mlp_w"],
      params["w_up"], params["w_down"], params["post_mlp_w"])
    return y.reshape(b, l, D_MODEL)


def _layer_params(l, pre_mix_w, w_in, conv_w, A_log, dt_bias, ret_norm_w, gdn_norm_w, w_out, post_mix_w,
                  pre_mlp_w, w_up, w_down, post_mlp_w):
    w = _bf(w_in[l])
    pvec = jnp.zeros((SUBLANES, LANES), F32)
    pvec = pvec.at[0, :HEADS].set(A_log[l].astype(F32)).at[1, :HEADS].set(dt_bias[l].astype(F32))
    return dict(
        pre_w=pre_mix_w[l].astype(F32)[None],
        w_in=w,
        w_ab=jnp.pad(w[:, MAIN_W:], ((0, 0), (0, LANES - 2 * HEADS))),
        conv_w=conv_w[l].astype(F32),
        pvec=pvec,
        normw=jnp.concatenate([jnp.tile(ret_norm_w[l], HEADS), jnp.tile(gdn_norm_w[l], HEADS)]).astype(F32)[None],
        w_out=_bf(w_out[l]),
        post_w=post_mix_w[l].astype(F32)[None],
        pre_mlp_w=pre_mlp_w[l].astype(F32)[None],
        w_up=_bf(w_up[l]),
        w_down=_bf(w_down[l]),
        post_mlp_w=post_mlp_w[l].astype(F32)[None],
    )


def kernel(x_prompt, x_sample, state_ret, state_gdn, state_conv, pre_mix_w, w_in, conv_w, A_log, dt_bias,
           ret_norm_w, gdn_norm_w, w_out, post_mix_w, pre_mlp_w, w_up, w_down, post_mlp_w):
    depth = w_in.shape[0]
    yp, ys = x_prompt, x_sample
    outs = [[] for _ in range(6)]
    for l in range(depth):
        params = _layer_params(l, pre_mix_w, w_in, conv_w, A_log, dt_bias, ret_norm_w, gdn_norm_w, w_out,
                               post_mix_w, pre_mlp_w, w_up, w_down, post_mlp_w)
        mix_p, rp, gp, cp = _mixer(yp, params, prompt=True)
        yp = _post(mix_p, yp, params)
        mix_s, rs, gs, cs = _mixer(ys, params, prompt=False,
                                   states=(state_ret[l], state_gdn[l], state_conv[l]))
        ys = _post(mix_s, ys, params)
        for dst, val, like in zip(outs, (rp, gp, cp, rs, gs, cs),
                                  (state_ret, state_gdn, state_conv) * 2):
            dst.append(val.astype(like.dtype))
    return (yp, ys) + tuple(jnp.stack(o) for o in outs)
```

```python
import functools

import numpy as np
import jax
import jax.numpy as jnp
from jax import lax
from jax.experimental import pallas as pl
from jax.experimental.pallas import tpu as pltpu

D_MODEL = 1024
HEADS = 4
HEAD_DIM = 128
GROUP_W = HEADS * HEAD_DIM
CONV_CH = 3 * GROUP_W
CONV_W = 4
D_FF = 4 * D_MODEL
MAIN_W = 4 * GROUP_W + CONV_CH + GROUP_W
IN_W = MAIN_W + 2 * HEADS
ROPE_BASE = 10000.0
EPS = 1e-6
PAST_LEN = 16384

ROWS = 64
BASE = 8
LANES = 128
SUBLANES = 8
TILE = 256
POST_ROWS = 1024
POST_PIECE = 256
VMEM_LIMIT = 60 * 1024 * 1024

OFF_RQ, OFF_RK, OFF_RV, OFF_RG = 0, GROUP_W, 2 * GROUP_W, 3 * GROUP_W
OFF_CONV = 4 * GROUP_W
OFF_Z = OFF_CONV + CONV_CH

F32 = jnp.float32
BF16 = jnp.bfloat16


def _bf(x):
    return x.astype(BF16)


def _dot(a, b):
    return jnp.dot(a, b, preferred_element_type=F32)


def _dot_nt(a, b):
    return lax.dot_general(a, b, (((1,), (1,)), ((), ())), preferred_element_type=F32)


def _split(x, n):
    parts, r = [], x
    for i in range(n):
        p = r.astype(BF16)
        parts.append(p)
        if i + 1 < n:
            r = r - p.astype(F32)
    return parts


def _mm(a, b, nt=False):
    return _dot_nt(_bf(a), _bf(b)) if nt else _dot(_bf(a), _bf(b))


def _rms(x, w):
    return x * lax.rsqrt(jnp.mean(x * x, axis=-1, keepdims=True) + EPS) * w


def _silu(x):
    return x / (1.0 + jnp.exp(-x))


def _tri_inverse_many(mats, group, row, col):
    n = mats[0].shape[0]
    packed = [jnp.concatenate(mats[i:i + HEADS], axis=1) for i in range(0, len(mats), HEADS)]
    prow = lax.broadcasted_iota(jnp.int32, (n, HEADS * n), 0)
    plane = lax.broadcasted_iota(jnp.int32, (n, HEADS * n), 1)
    pcol = plane % n
    seg = [_bf(((plane // n) == hh).astype(F32)) for hh in range(HEADS)]

    def blockdiag(y):
        return jnp.concatenate([y * m for m in seg], axis=0)

    def mm3(x, y):
        (x_hi, x_lo), (y_hi, y_lo) = _split(x, 2), _split(y, 2)
        both = _dot(jnp.concatenate([x_hi, x_lo], axis=0), blockdiag(y_hi))
        return both[:n] + both[n:] + _dot(x_hi, blockdiag(y_lo))

    eye_p = (prow == pcol).astype(F32)
    blk = (prow // BASE) == (pcol // BASE)
    n8 = [jnp.where(blk, -a, 0.0) for a in packed]
    p2 = [mm3(x, x) for x in n8]
    np2 = [mm3(x, p) for x, p in zip(n8, p2)]
    p4 = [mm3(p, p) for p in p2]
    e = [x + p + y for x, p, y in zip(n8, p2, np2)]
    ep4 = [mm3(x, p) for x, p in zip(e, p4)]
    t = [x + p + y + eye_p for x, p, y in zip(e, p4, ep4)]
    s = BASE
    while s < group:
        off = ((prow // (2 * s)) == (pcol // (2 * s))) & ((prow // s) != (pcol // s))
        a_off = [jnp.where(off, a, 0.0) for a in packed]
        ta = [mm3(x, y) for x, y in zip(t, a_off)]
        tat = [mm3(x, y) for x, y in zip(ta, t)]
        t = [x - y for x, y in zip(t, tat)]
        s *= 2
    eye = (row == col).astype(F32)
    return [tp[:, hh * n:(hh + 1) * n] for tp in t for hh in range(HEADS)], eye


def _mixer_kernel(cfg, *refs):
    if cfg["prompt"]:
        _mixer_step(cfg, refs, None)
        return
    for part in (0, 1):
        @pl.when(pl.program_id(1) == part)
        def _(part=part):
            _mixer_step(cfg, refs, part)


def _mixer_step(cfg, refs, part):
    lb, group, prompt = cfg["lb"], cfg["group"], cfg["prompt"]
    ngroups = ROWS // group
    nchunk = lb // ROWS
    chunks = list(range(nchunk))
    q_chunks = chunks if part is None else chunks[part * nchunk // 2:(part + 1) * nchunk // 2]
    it = iter(refs)
    x_ref, prew_ref, win_ref, wab_ref, cos_ref, sin_ref = (next(it) for _ in range(6))
    dmat_ref, rtab_ref, convw_ref, pvec_ref, normw_ref = (next(it) for _ in range(5))
    if not prompt:
        sret_in, sgdn_in, st8_ref = (next(it) for _ in range(3))
    mix_ref, sret_out, sgdn_out, conv_out = (next(it) for _ in range(4))
    qr_s, kr_s, qg_s, kg_s, vg_s, cum_s, tot_s, beta_s, o_s = (next(it) for _ in range(9))
    w_s, u_s, pr_s, pg_s, vb_s, gate_s = (next(it) for _ in range(6))
    if prompt:
        sret_s, sgdn_s, ext_s = (next(it) for _ in range(3))

        @pl.when(pl.program_id(0) == 0)
        def _():
            sret_s[...] = jnp.zeros_like(sret_s)
            sgdn_s[...] = jnp.zeros_like(sgdn_s)
            ext_s[:, 0:SUBLANES, :] = jnp.zeros((nchunk, SUBLANES, CONV_CH), F32)

    rows_of = lambda c: slice(c * ROWS, (c + 1) * ROWS)
    head_sl = lambda hh: slice(hh * HEAD_DIM, (hh + 1) * HEAD_DIM)
    lane = lambda x, n: x[:, n:n + 1]
    cat = lambda xs: xs[0] if len(xs) == 1 else jnp.concatenate(xs, axis=0)

    if part != 1:
        x = x_ref[...].reshape(lb, D_MODEL)
        h = _bf(_rms(x, prew_ref[...]))
        ab = _dot(h, wab_ref[...])
        cos, sin = cos_ref[...], sin_ref[...]
        if prompt:
            cos, sin = jnp.concatenate([cos] * nchunk, axis=0), jnp.concatenate([sin] * nchunk, axis=0)
        t_in_group = lax.broadcasted_iota(jnp.int32, (lb, 1), 0) % group

        for c0 in range(0, MAIN_W, TILE):
            res = _dot(h, win_ref[:, c0:c0 + TILE])
            tiles = [res[:, j * HEAD_DIM:(j + 1) * HEAD_DIM] for j in range(TILE // HEAD_DIM)]
            if c0 < OFF_RV:
                for j, tl in enumerate(tiles):
                    rot = tl * cos + pltpu.roll(tl, HEAD_DIM // 2, 1) * sin
                    if c0 < OFF_RK:
                        qr_s[:, c0 - OFF_RQ + j * HEAD_DIM:c0 - OFF_RQ + (j + 1) * HEAD_DIM] = rot
                    else:
                        kr_s[:, c0 - OFF_RK + j * HEAD_DIM:c0 - OFF_RK + (j + 1) * HEAD_DIM] = rot * (HEAD_DIM ** -0.5)
            elif c0 < OFF_RG:
                vb_s[:, c0 - OFF_RV:c0 - OFF_RV + TILE] = _bf(res)
            elif c0 < OFF_CONV:
                gate_s[:, c0 - OFF_RG:c0 - OFF_RG + TILE] = _silu(res)
            elif c0 >= OFF_Z:
                gate_s[:, GROUP_W + c0 - OFF_Z:GROUP_W + c0 - OFF_Z + TILE] = _silu(res)
            else:
                cc = c0 - OFF_CONV
                third = cc // GROUP_W
                dst = (qg_s, kg_s, vg_s)[third]
                cw = convw_ref[:, cc:cc + TILE]
                acc = res * cw[CONV_W - 1:CONV_W]
                if prompt:
                    for c in chunks:
                        ext_s[c, SUBLANES:, cc:cc + TILE] = res[rows_of(c)]
                    for s in range(1, CONV_W):
                        sh = jnp.concatenate([ext_s[c, SUBLANES - s:SUBLANES - s + ROWS, cc:cc + TILE] for c in chunks],
                                             axis=0)
                        acc = acc + sh * cw[CONV_W - 1 - s:CONV_W - s]
                    for c in chunks:
                        tail = ext_s[c, ROWS:, cc:cc + TILE]
                        ext_s[c, 0:SUBLANES, cc:cc + TILE] = tail
                        conv_out[c * SUBLANES:(c + 1) * SUBLANES, cc:cc + TILE] = tail
                else:
                    st8 = st8_ref[:, cc:cc + TILE]
                    for s in range(1, CONV_W):
                        sh = jnp.where(t_in_group < s, pltpu.roll(st8, lb - SUBLANES + s, 0), pltpu.roll(res, s, 0))
                        acc = acc + sh * cw[CONV_W - 1 - s:CONV_W - s]
                    conv_out[:, cc:cc + TILE] = res
                conv = _silu(acc)
                for j in range(TILE // HEAD_DIM):
                    c = conv[:, j * HEAD_DIM:(j + 1) * HEAD_DIM]
                    if third < 2:
                        c = c * lax.rsqrt(jnp.sum(c * c, axis=-1, keepdims=True) + EPS)
                    if third == 0:
                        c = c * (HEAD_DIM ** -0.5)
                    d0 = cc - third * GROUP_W + j * HEAD_DIM
                    dst[:, d0:d0 + HEAD_DIM] = c

        pv = pvec_ref[...]
        a_plus = ab + pv[1:2]
        softplus = jnp.maximum(a_plus, 0.0) + jnp.log1p(jnp.exp(-jnp.abs(a_plus)))
        g = -jnp.exp(pv[0:1]) * softplus
        beta_s[...] = 1.0 / (1.0 + jnp.exp(-ab))

        row = lax.broadcasted_iota(jnp.int32, (ROWS, ROWS), 0)
        col = lax.broadcasted_iota(jnp.int32, (ROWS, ROWS), 1)
        same = (row // group) == (col // group)
        causal = same & (row >= col)
        strict = same & (row > col)
        sum_mat = _bf(jnp.concatenate([causal.astype(F32), same.astype(F32)], axis=0))
        sum_mat3 = jnp.concatenate([sum_mat] * 3, axis=1)
        for c in chunks:
            r = _dot(sum_mat3, jnp.concatenate(_split(g[rows_of(c)], 3), axis=0))
            cum_s[rows_of(c), :] = r[:ROWS]
            tot_s[rows_of(c), :] = r[ROWS:]
        sel = _bf((lax.broadcasted_iota(jnp.int32, (SUBLANES, LANES), 0)
                   == lax.broadcasted_iota(jnp.int32, (SUBLANES, LANES), 1)).astype(F32))
        sel3 = jnp.concatenate([sel] * 3, axis=1)

        items = [(c, hh) for c in chunks for hh in range(HEADS)]
        cumc = [cum_s[rows_of(c), :] for c in chunks]
        totc = [tot_s[rows_of(c), :] for c in chunks]
        betac = [beta_s[rows_of(c), :] for c in chunks]
        ecumc = [jnp.exp(x) for x in cumc]
        tailc = [jnp.exp(t - x) for t, x in zip(totc, cumc)]
        cumrow = [_dot_nt(sel3, jnp.concatenate(_split(x, 3), axis=1)) for x in cumc]

        for c, hh in items:
            rows, sl = rows_of(c), head_sl(hh)
            q, k = qr_s[rows, sl], kr_s[rows, sl]
            pr_s[c * HEADS + hh] = _mm(q, k, nt=True) * dmat_ref[hh]
            qr_s[rows, sl] = q * rtab_ref[0, :, sl]
            kr_s[rows, sl] = k * rtab_ref[1, :, sl]

        qk_kk, dmats = [], []
        for c, hh in items:
            rows, sl = rows_of(c), head_sl(hh)
            kb = _bf(kg_s[rows, sl])
            qk_kk.append(_dot_nt(jnp.concatenate([_bf(qg_s[rows, sl]), kb], axis=0), kb))
            diff = lane(cumc[c], hh) - cumrow[c][hh:hh + 1, :]
            dmats.append(jnp.where(causal, jnp.exp(jnp.minimum(diff, 0.0)), 0.0))
        a_mats = [jnp.where(strict, lane(betac[c], HEADS + hh) * x[ROWS:] * d, 0.0)
                  for (c, hh), x, d in zip(items, qk_kk, dmats)]
        t_inv, eye = _tri_inverse_many(a_mats, group, row, col)
        for i, (c, hh) in enumerate(items):
            rows, sl = rows_of(c), head_sl(hh)
            q, k, v = qg_s[rows, sl], kg_s[rows, sl], vg_s[rows, sl]
            beta_h = lane(betac[c], HEADS + hh)
            rhs = jnp.concatenate([k * (beta_h * lane(ecumc[c], hh)), v * beta_h], axis=1)
            sol = rhs + _mm(t_inv[i] - eye, rhs)
            w_s[rows, sl] = sol[:, :HEAD_DIM]
            u_s[rows, sl] = sol[:, HEAD_DIM:]
            pg_s[c * HEADS + hh] = qk_kk[i][:ROWS] * dmats[i]
            qg_s[rows, sl] = q * lane(ecumc[c], hh)
            kg_s[rows, sl] = k * lane(tailc[c], hh)

    def state_refs(c, hh, gi):
        if prompt:
            return (sret_s.at[c, hh], sret_s.at[c, hh], sgdn_s.at[c, hh], sgdn_s.at[c, hh])
        n = (c - q_chunks[0]) * ngroups + gi
        return (sret_in.at[n, hh], sret_out.at[n, hh], sgdn_in.at[n, hh], sgdn_out.at[n, hh])

    for c in q_chunks:
        dtotc = jnp.exp(tot_s[rows_of(c), :])
        qs_r, qs_g, ws_g = {}, {}, {}
        for hh in range(HEADS):
            sl = head_sl(hh)
            for gi in range(ngroups):
                rs = slice(c * ROWS + gi * group, c * ROWS + (gi + 1) * group)
                r_in, _, g_in, _ = state_refs(c, hh, gi)
                qs_r[hh, gi] = _mm(qr_s[rs, sl], r_in[...])
                r = _mm(jnp.concatenate([qg_s[rs, sl], w_s[rs, sl]], axis=0), g_in[...])
                qs_g[hh, gi], ws_g[hh, gi] = r[:group], r[group:]
        for hh in range(HEADS):
            rows, sl = rows_of(c), head_sl(hh)
            i = c * HEADS + hh
            vb = vb_s[rows, sl]
            v_new = u_s[rows, sl] - cat([ws_g[hh, gi] for gi in range(ngroups)])
            vnb = _bf(v_new)
            o_s[rows, sl] = cat([qs_r[hh, gi] for gi in range(ngroups)]) + _mm(pr_s[i], vb)
            o_s[rows, GROUP_W + hh * HEAD_DIM:GROUP_W + (hh + 1) * HEAD_DIM] = (
                cat([qs_g[hh, gi] for gi in range(ngroups)]) + _mm(pg_s[i], vnb))
            for gi in range(ngroups):
                rs = slice(c * ROWS + gi * group, c * ROWS + (gi + 1) * group)
                ls = slice(gi * group, (gi + 1) * group)
                r_in, r_out, g_in, g_out = state_refs(c, hh, gi)
                r_out[...] = cfg["ret_dtot"][hh] * r_in[...] + _mm(kr_s[rs, sl].T, vb[ls])
                dec = dtotc[gi * group:gi * group + 1, hh:hh + 1]
                g_out[...] = dec * g_in[...] + _mm(kg_s[rs, sl].T, vnb[ls])

    q_rows = slice(q_chunks[0] * ROWS, (q_chunks[-1] + 1) * ROWS)
    for hh in range(2 * HEADS):
        sl = head_sl(hh)
        y = _rms(o_s[q_rows, sl], normw_ref[:, sl]) * gate_s[q_rows, sl]
        y = y.astype(mix_ref.dtype)
        if prompt:
            mix_ref[:, :, sl] = y.reshape(nchunk, ROWS, HEAD_DIM)
        else:
            mix_ref[q_rows, sl] = y

    if prompt:
        @pl.when(pl.program_id(0) == pl.num_programs(0) - 1)
        def _():
            sret_out[...] = sret_s[...]
            sgdn_out[...] = sgdn_s[...]


def _post_kernel(mix_ref, x_ref, wout_ref, postw_ref, premlpw_ref, wup_ref, wdown_ref, postmlpw_ref, out_ref):
    tm = x_ref.shape[0]
    piece = POST_PIECE if tm % POST_PIECE == 0 else tm
    halves = [slice(r, r + piece) for r in range(0, tm, piece)]
    m = [_dot(mix_ref[r, :], wout_ref[...]) for r in halves]
    x1 = [x_ref[r, :] + _rms(mi, postw_ref[...]) for r, mi in zip(halves, m)]
    h2 = [_bf(_rms(xi, premlpw_ref[...])) for xi in x1]
    acc = [None] * len(halves)
    for c in range(D_FF // D_MODEL):
        for i in range(len(halves)):
            f = _dot(h2[i], wup_ref[:, c * D_MODEL:(c + 1) * D_MODEL])
            f = jnp.square(jnp.maximum(f, 0.0))
            part = _dot(_bf(f), wdown_ref[c * D_MODEL:(c + 1) * D_MODEL, :])
            acc[i] = part if acc[i] is None else acc[i] + part
    for r, xi, ai in zip(halves, x1, acc):
        out_ref[r, :] = xi + _rms(ai, postmlpw_ref[...])


def _const_spec(shape, single=False):
    index_map = lambda *_: (0,) * len(shape)
    if single:
        return pl.BlockSpec(shape, index_map, pipeline_mode=pl.Buffered(1))
    return pl.BlockSpec(shape, index_map)


def _retention_tables(group):
    log_gamma = np.log(1.0 - 2.0 ** (-5.0 - np.arange(HEADS, dtype=np.float64)))
    t = np.arange(ROWS) % group
    same = (np.arange(ROWS)[:, None] // group) == (np.arange(ROWS)[None, :] // group)
    causal = same & (t[:, None] >= t[None, :])
    dmat = np.where(causal[None], np.exp((t[:, None] - t[None, :])[None] * log_gamma[:, None, None]), 0.0)
    ecum = np.exp((t[:, None] + 1.0) * log_gamma[None, :])
    tail = np.exp((group - 1.0 - t[:, None]) * log_gamma[None, :])
    rtab = np.stack([np.repeat(ecum, HEAD_DIM, axis=1), np.repeat(tail, HEAD_DIM, axis=1)])
    dtot = tuple(float(np.float32(v)) for v in np.exp(group * log_gamma))
    return jnp.asarray(dmat, F32), jnp.asarray(rtab, F32), dtot


def _rope_tables(length, offset):
    inv_freq = ROPE_BASE ** (-np.arange(0, HEAD_DIM, 2, dtype=np.float64) / HEAD_DIM)
    ang = (np.arange(length, dtype=np.float64) + offset)[:, None] * inv_freq[None, :]
    cos, sin = np.cos(ang), np.sin(ang)
    return (jnp.asarray(np.concatenate([cos, cos], axis=-1), F32),
            jnp.asarray(np.concatenate([-sin, sin], axis=-1), F32))


def _mixer(x, params, *, prompt, states=None):
    b, l, _ = x.shape
    t = b * l
    if prompt:
        group, lb = ROWS, b * ROWS
        assert l % ROWS == 0
        cos, sin = _rope_tables(l, 0)
        grid = (l // ROWS,)
        x_in = x
        x_spec = pl.BlockSpec((b, ROWS, D_MODEL), lambda j: (0, j, 0))
        pos_spec = pl.BlockSpec((ROWS, HEAD_DIM), lambda j: (j, 0))
    else:
        group = l
        lb = min(4 * ROWS, t)
        assert ROWS % group == 0 and group == SUBLANES and t % lb == 0 and lb % (2 * ROWS) == 0
        cos, sin = _rope_tables(l, PAST_LEN)
        cos, sin = jnp.tile(cos, (lb // l, 1)), jnp.tile(sin, (lb // l, 1))
        grid = (t // lb, 2)
        row_map = lambda i, half: (i, 0)
        x_in = x.reshape(t, D_MODEL)
        x_spec = pl.BlockSpec((lb, D_MODEL), row_map)
        pos_spec = pl.BlockSpec((lb, HEAD_DIM), lambda i, half: (0, 0))
    nseq = lb // group // (1 if prompt else 2)
    dmat, rtab, dtot = _retention_tables(group)
    cfg = dict(lb=lb, group=group, prompt=prompt, ret_dtot=dtot)

    in_specs = [
        x_spec,
        _const_spec((1, D_MODEL)),
        _const_spec((D_MODEL, IN_W), single=True),
        _const_spec((D_MODEL, LANES), single=True),
        pos_spec,
        pos_spec,
        _const_spec((HEADS, ROWS, ROWS)),
        _const_spec((2, ROWS, GROUP_W)),
        _const_spec((CONV_W, CONV_CH)),
        _const_spec((SUBLANES, LANES)),
        _const_spec((1, 2 * GROUP_W)),
    ]
    args = [x_in, params["pre_w"], params["w_in"], params["w_ab"], cos, sin, dmat, rtab,
            params["conv_w"], params["pvec"], params["normw"]]
    state_shape = (HEADS, HEAD_DIM, HEAD_DIM)
    if prompt:
        out_specs = [
            pl.BlockSpec((b, ROWS, 2 * GROUP_W), lambda j: (0, j, 0)),
            _const_spec((b,) + state_shape),
            _const_spec((b,) + state_shape),
            _const_spec((b * SUBLANES, CONV_CH)),
        ]
        out_shape = [
            jax.ShapeDtypeStruct((b, l, 2 * GROUP_W), BF16),
            jax.ShapeDtypeStruct((b,) + state_shape, F32),
            jax.ShapeDtypeStruct((b,) + state_shape, F32),
            jax.ShapeDtypeStruct((b * SUBLANES, CONV_CH), F32),
        ]
    else:
        s_ret, s_gdn, s_conv = states
        st8 = jnp.pad(s_conv.astype(F32), ((0, 0), (SUBLANES - (CONV_W - 1), 0), (0, 0))).reshape(t, CONV_CH)
        half_map = lambda i, half: (2 * i + half, 0, 0, 0)
        in_specs += [
            pl.BlockSpec((nseq,) + state_shape, half_map),
            pl.BlockSpec((nseq,) + state_shape, half_map),
            pl.BlockSpec((lb, CONV_CH), row_map),
        ]
        args += [s_ret.astype(F32), s_gdn.astype(F32), st8]
        out_specs = [
            pl.BlockSpec((lb, 2 * GROUP_W), row_map),
            pl.BlockSpec((nseq,) + state_shape, half_map),
            pl.BlockSpec((nseq,) + state_shape, half_map),
            pl.BlockSpec((lb, CONV_CH), row_map),
        ]
        out_shape = [
            jax.ShapeDtypeStruct((t, 2 * GROUP_W), BF16),
            jax.ShapeDtypeStruct((b,) + state_shape, F32),
            jax.ShapeDtypeStruct((b,) + state_shape, F32),
            jax.ShapeDtypeStruct((t, CONV_CH), F32),
        ]
    nmat = lb // ROWS * HEADS
    scratch = [
        pltpu.VMEM((lb, GROUP_W), F32), pltpu.VMEM((lb, GROUP_W), F32),
        pltpu.VMEM((lb, GROUP_W), F32), pltpu.VMEM((lb, GROUP_W), F32), pltpu.VMEM((lb, GROUP_W), F32),
        pltpu.VMEM((lb, LANES), F32), pltpu.VMEM((lb, LANES), F32), pltpu.VMEM((lb, LANES), F32),
        pltpu.VMEM((lb, 2 * GROUP_W), F32),
        pltpu.VMEM((lb, GROUP_W), F32), pltpu.VMEM((lb, GROUP_W), F32),
        pltpu.VMEM((nmat, ROWS, ROWS), F32), pltpu.VMEM((nmat, ROWS, ROWS), F32),
        pltpu.VMEM((lb, GROUP_W), BF16), pltpu.VMEM((lb, 2 * GROUP_W), F32),
    ]
    if prompt:
        scratch += [pltpu.VMEM((b,) + state_shape, F32), pltpu.VMEM((b,) + state_shape, F32),
                    pltpu.VMEM((b, SUBLANES + ROWS, CONV_CH), F32)]

    mix, s_ret_new, s_gdn_new, conv_rows = pl.pallas_call(
        functools.partial(_mixer_kernel, cfg),
        grid=grid,
        in_specs=in_specs,
        out_specs=out_specs,
        out_shape=out_shape,
        scratch_shapes=scratch,
        compiler_params=pltpu.CompilerParams(dimension_semantics=("arbitrary",) * len(grid),
                                             vmem_limit_bytes=VMEM_LIMIT),
        name="mixer_prompt" if prompt else "mixer_sample",
    )(*args)
    new_conv = conv_rows.reshape(b, SUBLANES, CONV_CH)[:, SUBLANES - (CONV_W - 1):]
    return mix.reshape(t, 2 * GROUP_W), s_ret_new, s_gdn_new, new_conv


def _post(mix, x, params):
    b, l, _ = x.shape
    t = b * l
    tm = min(POST_ROWS, t)
    assert t % tm == 0
    rows = pl.BlockSpec((tm, D_MODEL), lambda i: (i, 0))
    y = pl.pallas_call(
        _post_kernel,
        grid=(t // tm,),
        in_specs=[rows, rows, _const_spec((2 * GROUP_W, D_MODEL), single=True), _const_spec((1, D_MODEL)),
                  _const_spec((1, D_MODEL)), _const_spec((D_MODEL, D_FF), single=True),
                  _const_spec((D_FF, D_MODEL), single=True), _const_spec((1, D_MODEL))],
        out_specs=rows,
        out_shape=jax.ShapeDtypeStruct((t, D_MODEL), F32),
        compiler_params=pltpu.CompilerParams(dimension_semantics=("arbitrary",), vmem_limit_bytes=VMEM_LIMIT),
        name="post",
    )(mix, x.reshape(t, D_MODEL), params["w_out"], params["post_w"], params["pre_mlp_w"],
      params["w_up"], params["w_down"], params["post_mlp_w"])
    return y.reshape(b, l, D_MODEL)


def _layer_params(l, pre_mix_w, w_in, conv_w, A_log, dt_bias, ret_norm_w, gdn_norm_w, w_out, post_mix_w,
                  pre_mlp_w, w_up, w_down, post_mlp_w):
    w = _bf(w_in[l])
    pvec = jnp.zeros((SUBLANES, LANES), F32)
    pvec = pvec.at[0, :HEADS].set(A_log[l].astype(F32)).at[1, :HEADS].set(dt_bias[l].astype(F32))
    return dict(
        pre_w=pre_mix_w[l].astype(F32)[None],
        w_in=w,
        w_ab=jnp.pad(w[:, MAIN_W:], ((0, 0), (0, LANES - 2 * HEADS))),
        conv_w=conv_w[l].astype(F32),
        pvec=pvec,
        normw=jnp.concatenate([jnp.tile(ret_norm_w[l], HEADS), jnp.tile(gdn_norm_w[l], HEADS)]).astype(F32)[None],
        w_out=_bf(w_out[l]),
        post_w=post_mix_w[l].astype(F32)[None],
        pre_mlp_w=pre_mlp_w[l].astype(F32)[None],
        w_up=_bf(w_up[l]),
        w_down=_bf(w_down[l]),
        post_mlp_w=post_mlp_w[l].astype(F32)[None],
    )


def kernel(x_prompt, x_sample, state_ret, state_gdn, state_conv, pre_mix_w, w_in, conv_w, A_log, dt_bias,
           ret_norm_w, gdn_norm_w, w_out, post_mix_w, pre_mlp_w, w_up, w_down, post_mlp_w):
    depth = w_in.shape[0]
    yp, ys = x_prompt, x_sample
    outs = [[] for _ in range(6)]
    for l in range(depth):
        params = _layer_params(l, pre_mix_w, w_in, conv_w, A_log, dt_bias, ret_norm_w, gdn_norm_w, w_out,
                               post_mix_w, pre_mlp_w, w_up, w_down, post_mlp_w)
        mix_p, rp, gp, cp = _mixer(yp, params, prompt=True)
        yp = _post(mix_p, yp, params)
        mix_s, rs, gs, cs = _mixer(ys, params, prompt=False,
                                   states=(state_ret[l], state_gdn[l], state_conv[l]))
        ys = _post(mix_s, ys, params)
        for dst, val, like in zip(outs, (rp, gp, cp, rs, gs, cs),
                                  (state_ret, state_gdn, state_conv) * 2):
            dst.append(val.astype(like.dtype))
    return (yp, ys) + tuple(jnp.stack(o) for o in outs)
```

```python
import functools

import numpy as np
import jax
import jax.numpy as jnp
from jax import lax
from jax.experimental import pallas as pl
from jax.experimental.pallas import tpu as pltpu

D_MODEL = 1024
HEADS = 4
HEAD_DIM = 128
GROUP_W = HEADS * HEAD_DIM
CONV_CH = 3 * GROUP_W
CONV_W = 4
D_FF = 4 * D_MODEL
MAIN_W = 4 * GROUP_W + CONV_CH + GROUP_W
IN_W = MAIN_W + 2 * HEADS
ROPE_BASE = 10000.0
EPS = 1e-6
PAST_LEN = 16384

ROWS = 64
BASE = 8
LANES = 128
SUBLANES = 8
TILE = 256
POST_ROWS = 1024
POST_PIECE = 256
VMEM_LIMIT = 58 * 1024 * 1024

OFF_RQ, OFF_RK, OFF_RV, OFF_RG = 0, GROUP_W, 2 * GROUP_W, 3 * GROUP_W
OFF_CONV = 4 * GROUP_W
OFF_Z = OFF_CONV + CONV_CH

F32 = jnp.float32
BF16 = jnp.bfloat16


def _bf(x):
    return x.astype(BF16)


def _dot(a, b):
    return jnp.dot(a, b, preferred_element_type=F32)


def _dot_nt(a, b):
    return lax.dot_general(a, b, (((1,), (1,)), ((), ())), preferred_element_type=F32)


def _split(x, n):
    parts, r = [], x
    for i in range(n):
        p = r.astype(BF16)
        parts.append(p)
        if i + 1 < n:
            r = r - p.astype(F32)
    return parts


def _mm(a, b, nt=False):
    return _dot_nt(_bf(a), _bf(b)) if nt else _dot(_bf(a), _bf(b))


def _rms(x, w):
    return x * lax.rsqrt(jnp.mean(x * x, axis=-1, keepdims=True) + EPS) * w


def _silu(x):
    return x / (1.0 + jnp.exp(-x))


def _tri_inverse_many(mats, group, row, col):
    n = mats[0].shape[0]
    packed = [jnp.concatenate(mats[i:i + HEADS], axis=1) for i in range(0, len(mats), HEADS)]
    prow = lax.broadcasted_iota(jnp.int32, (n, HEADS * n), 0)
    plane = lax.broadcasted_iota(jnp.int32, (n, HEADS * n), 1)
    pcol = plane % n
    seg = [_bf(((plane // n) == hh).astype(F32)) for hh in range(HEADS)]

    def blockdiag(y):
        return jnp.concatenate([y * m for m in seg], axis=0)

    def mm3(x, y):
        (x_hi, x_lo), (y_hi, y_lo) = _split(x, 2), _split(y, 2)
        both = _dot(jnp.concatenate([x_hi, x_lo], axis=0), blockdiag(y_hi))
        return both[:n] + both[n:] + _dot(x_hi, blockdiag(y_lo))

    eye_p = (prow == pcol).astype(F32)
    blk = (prow // BASE) == (pcol // BASE)
    n8 = [jnp.where(blk, -a, 0.0) for a in packed]
    p2 = [mm3(x, x) for x in n8]
    np2 = [mm3(x, p) for x, p in zip(n8, p2)]
    p4 = [mm3(p, p) for p in p2]
    e = [x + p + y for x, p, y in zip(n8, p2, np2)]
    ep4 = [mm3(x, p) for x, p in zip(e, p4)]
    t = [x + p + y + eye_p for x, p, y in zip(e, p4, ep4)]
    s = BASE
    while s < group:
        off = ((prow // (2 * s)) == (pcol // (2 * s))) & ((prow // s) != (pcol // s))
        a_off = [jnp.where(off, a, 0.0) for a in packed]
        ta = [mm3(x, y) for x, y in zip(t, a_off)]
        tat = [mm3(x, y) for x, y in zip(ta, t)]
        t = [x - y for x, y in zip(t, tat)]
        s *= 2
    return t, eye_p, blockdiag


def _mixer_kernel(cfg, *refs):
    lb, group, prompt = cfg["lb"], cfg["group"], cfg["prompt"]
    ngroups = ROWS // group
    nchunk = lb // ROWS
    chunks = list(range(nchunk))
    it = iter(refs)
    x_ref, prew_ref, win_ref, wab_ref, cos_ref, sin_ref = (next(it) for _ in range(6))
    dmat_ref, rtab_ref, convw_ref, pvec_ref, normw_ref = (next(it) for _ in range(5))
    if not prompt:
        sret_in, sgdn_in, st8_ref = (next(it) for _ in range(3))
    mix_ref, sret_out, sgdn_out, conv_out = (next(it) for _ in range(4))
    qr_s, kr_s, qg_s, kg_s, vg_s, cum_s, tot_s, beta_s, o_s = (next(it) for _ in range(9))
    w_s, u_s, pr_s, pg_s, vb_s, gate_s = (next(it) for _ in range(6))
    if prompt:
        sret_s, sgdn_s, ext_s = (next(it) for _ in range(3))

        @pl.when(pl.program_id(0) == 0)
        def _():
            sret_s[...] = jnp.zeros_like(sret_s)
            sgdn_s[...] = jnp.zeros_like(sgdn_s)
            ext_s[:, 0:SUBLANES, :] = jnp.zeros((nchunk, SUBLANES, CONV_CH), F32)

    rows_of = lambda c: slice(c * ROWS, (c + 1) * ROWS)
    head_sl = lambda hh: slice(hh * HEAD_DIM, (hh + 1) * HEAD_DIM)
    lane = lambda x, n: x[:, n:n + 1]
    cat = lambda xs: xs[0] if len(xs) == 1 else jnp.concatenate(xs, axis=0)

    x = x_ref[...].reshape(lb, D_MODEL)
    h = _bf(_rms(x, prew_ref[...]))
    ab = _dot(h, wab_ref[...])
    cos, sin = cos_ref[...], sin_ref[...]
    if prompt:
        cos, sin = jnp.concatenate([cos] * nchunk, axis=0), jnp.concatenate([sin] * nchunk, axis=0)
    t_in_group = lax.broadcasted_iota(jnp.int32, (lb, 1), 0) % group

    for c0 in range(0, MAIN_W, TILE):
        res = _dot(h, win_ref[:, c0:c0 + TILE])
        tiles = [res[:, j * HEAD_DIM:(j + 1) * HEAD_DIM] for j in range(TILE // HEAD_DIM)]
        if c0 < OFF_RV:
            for j, tl in enumerate(tiles):
                rot = tl * cos + pltpu.roll(tl, HEAD_DIM // 2, 1) * sin
                if c0 < OFF_RK:
                    qr_s[:, c0 - OFF_RQ + j * HEAD_DIM:c0 - OFF_RQ + (j + 1) * HEAD_DIM] = rot
                else:
                    kr_s[:, c0 - OFF_RK + j * HEAD_DIM:c0 - OFF_RK + (j + 1) * HEAD_DIM] = rot * (HEAD_DIM ** -0.5)
        elif c0 < OFF_RG:
            vb_s[:, c0 - OFF_RV:c0 - OFF_RV + TILE] = _bf(res)
        elif c0 < OFF_CONV:
            gate_s[:, c0 - OFF_RG:c0 - OFF_RG + TILE] = _silu(res)
        elif c0 >= OFF_Z:
            gate_s[:, GROUP_W + c0 - OFF_Z:GROUP_W + c0 - OFF_Z + TILE] = _silu(res)
        else:
            cc = c0 - OFF_CONV
            part = cc // GROUP_W
            dst = (qg_s, kg_s, vg_s)[part]
            cw = convw_ref[:, cc:cc + TILE]
            acc = res * cw[CONV_W - 1:CONV_W]
            if prompt:
                for c in chunks:
                    ext_s[c, SUBLANES:, cc:cc + TILE] = res[rows_of(c)]
                for s in range(1, CONV_W):
                    sh = jnp.concatenate([ext_s[c, SUBLANES - s:SUBLANES - s + ROWS, cc:cc + TILE] for c in chunks],
                                         axis=0)
                    acc = acc + sh * cw[CONV_W - 1 - s:CONV_W - s]
                for c in chunks:
                    tail = ext_s[c, ROWS:, cc:cc + TILE]
                    ext_s[c, 0:SUBLANES, cc:cc + TILE] = tail
                    conv_out[c * SUBLANES:(c + 1) * SUBLANES, cc:cc + TILE] = tail
            else:
                st8 = st8_ref[:, cc:cc + TILE]
                for s in range(1, CONV_W):
                    sh = jnp.where(t_in_group < s, pltpu.roll(st8, lb - SUBLANES + s, 0), pltpu.roll(res, s, 0))
                    acc = acc + sh * cw[CONV_W - 1 - s:CONV_W - s]
                conv_out[:, cc:cc + TILE] = res
            conv = _silu(acc)
            for j in range(TILE // HEAD_DIM):
                c = conv[:, j * HEAD_DIM:(j + 1) * HEAD_DIM]
                if part < 2:
                    c = c * lax.rsqrt(jnp.sum(c * c, axis=-1, keepdims=True) + EPS)
                if part == 0:
                    c = c * (HEAD_DIM ** -0.5)
                d0 = cc - part * GROUP_W + j * HEAD_DIM
                dst[:, d0:d0 + HEAD_DIM] = c

    pv = pvec_ref[...]
    a_plus = ab + pv[1:2]
    softplus = jnp.maximum(a_plus, 0.0) + jnp.log1p(jnp.exp(-jnp.abs(a_plus)))
    g = -jnp.exp(pv[0:1]) * softplus
    beta_s[...] = 1.0 / (1.0 + jnp.exp(-ab))

    row = lax.broadcasted_iota(jnp.int32, (ROWS, ROWS), 0)
    col = lax.broadcasted_iota(jnp.int32, (ROWS, ROWS), 1)
    same = (row // group) == (col // group)
    causal = same & (row >= col)
    strict = same & (row > col)
    sum_mat = _bf(jnp.concatenate([causal.astype(F32), same.astype(F32)], axis=0))
    sum_mat3 = jnp.concatenate([sum_mat] * 3, axis=1)
    for c in chunks:
        r = _dot(sum_mat3, jnp.concatenate(_split(g[rows_of(c)], 3), axis=0))
        cum_s[rows_of(c), :] = r[:ROWS]
        tot_s[rows_of(c), :] = r[ROWS:]
    sel = _bf((lax.broadcasted_iota(jnp.int32, (SUBLANES, LANES), 0)
               == lax.broadcasted_iota(jnp.int32, (SUBLANES, LANES), 1)).astype(F32))
    sel3 = jnp.concatenate([sel] * 3, axis=1)

    items = [(c, hh) for c in chunks for hh in range(HEADS)]
    cumc = [cum_s[rows_of(c), :] for c in chunks]
    totc = [tot_s[rows_of(c), :] for c in chunks]
    betac = [beta_s[rows_of(c), :] for c in chunks]
    ecumc = [jnp.exp(x) for x in cumc]
    tailc = [jnp.exp(t - x) for t, x in zip(totc, cumc)]
    cumrow = [_dot_nt(sel3, jnp.concatenate(_split(x, 3), axis=1)) for x in cumc]

    for c, hh in items:
        rows, sl = rows_of(c), head_sl(hh)
        q, k = qr_s[rows, sl], kr_s[rows, sl]
        pr_s[c * HEADS + hh] = _mm(q, k, nt=True) * dmat_ref[hh]
        qr_s[rows, sl] = q * rtab_ref[0, :, sl]
        kr_s[rows, sl] = k * rtab_ref[1, :, sl]

    qk_kk, dmats = [], []
    for c, hh in items:
        rows, sl = rows_of(c), head_sl(hh)
        kb = _bf(kg_s[rows, sl])
        qk_kk.append(_dot_nt(jnp.concatenate([_bf(qg_s[rows, sl]), kb], axis=0), kb))
        diff = lane(cumc[c], hh) - cumrow[c][hh:hh + 1, :]
        dmats.append(jnp.where(causal, jnp.exp(jnp.minimum(diff, 0.0)), 0.0))
    a_mats = [jnp.where(strict, lane(betac[c], HEADS + hh) * x[ROWS:] * d, 0.0)
              for (c, hh), x, d in zip(items, qk_kk, dmats)]
    t_inv, eye_p, blockdiag = _tri_inverse_many(a_mats, group, row, col)
    for c in chunks:
        rows = rows_of(c)
        rhs = []
        for hh in range(HEADS):
            sl = head_sl(hh)
            beta_h = lane(betac[c], HEADS + hh)
            rhs.append(jnp.concatenate([kg_s[rows, sl] * (beta_h * lane(ecumc[c], hh)), vg_s[rows, sl] * beta_h],
                                       axis=1))
        rhs = jnp.concatenate(rhs, axis=0)
        sol = rhs + _dot(blockdiag(_bf(t_inv[c] - eye_p)), _bf(rhs))
        for hh in range(HEADS):
            sl, i = head_sl(hh), c * HEADS + hh
            w_s[rows, sl] = sol[hh * ROWS:(hh + 1) * ROWS, :HEAD_DIM]
            u_s[rows, sl] = sol[hh * ROWS:(hh + 1) * ROWS, HEAD_DIM:]
            pg_s[i] = qk_kk[i][:ROWS] * dmats[i]
            qg_s[rows, sl] = qg_s[rows, sl] * lane(ecumc[c], hh)
            kg_s[rows, sl] = kg_s[rows, sl] * lane(tailc[c], hh)

    def state_refs(c, hh, gi):
        if prompt:
            return (sret_s.at[c, hh], sret_s.at[c, hh], sgdn_s.at[c, hh], sgdn_s.at[c, hh])
        n = c * ngroups + gi
        return (sret_in.at[n, hh], sret_out.at[n, hh], sgdn_in.at[n, hh], sgdn_out.at[n, hh])

    for c in chunks:
        dtotc = jnp.exp(totc[c])
        qs_r, qs_g, ws_g = {}, {}, {}
        for hh in range(HEADS):
            sl = head_sl(hh)
            for gi in range(ngroups):
                rs = slice(c * ROWS + gi * group, c * ROWS + (gi + 1) * group)
                r_in, _, g_in, _ = state_refs(c, hh, gi)
                qs_r[hh, gi] = _mm(qr_s[rs, sl], r_in[...])
                r = _mm(jnp.concatenate([qg_s[rs, sl], w_s[rs, sl]], axis=0), g_in[...])
                qs_g[hh, gi], ws_g[hh, gi] = r[:group], r[group:]
        for hh in range(HEADS):
            rows, sl = rows_of(c), head_sl(hh)
            i = c * HEADS + hh
            vb = vb_s[rows, sl]
            v_new = u_s[rows, sl] - cat([ws_g[hh, gi] for gi in range(ngroups)])
            vnb = _bf(v_new)
            o_s[rows, sl] = cat([qs_r[hh, gi] for gi in range(ngroups)]) + _mm(pr_s[i], vb)
            o_s[rows, GROUP_W + hh * HEAD_DIM:GROUP_W + (hh + 1) * HEAD_DIM] = (
                cat([qs_g[hh, gi] for gi in range(ngroups)]) + _mm(pg_s[i], vnb))
            for gi in range(ngroups):
                rs = slice(c * ROWS + gi * group, c * ROWS + (gi + 1) * group)
                ls = slice(gi * group, (gi + 1) * group)
                r_in, r_out, g_in, g_out = state_refs(c, hh, gi)
                r_out[...] = cfg["ret_dtot"][hh] * r_in[...] + _mm(kr_s[rs, sl].T, vb[ls])
                dec = dtotc[gi * group:gi * group + 1, hh:hh + 1]
                g_out[...] = dec * g_in[...] + _mm(kg_s[rs, sl].T, vnb[ls])

    for hh in range(2 * HEADS):
        sl = head_sl(hh)
        y = _rms(o_s[:, sl], normw_ref[:, sl]) * gate_s[:, sl]
        y = y.astype(mix_ref.dtype)
        if prompt:
            mix_ref[:, :, sl] = y.reshape(nchunk, ROWS, HEAD_DIM)
        else:
            mix_ref[:, sl] = y

    if prompt:
        @pl.when(pl.program_id(0) == pl.num_programs(0) - 1)
        def _():
            sret_out[...] = sret_s[...]
            sgdn_out[...] = sgdn_s[...]


def _post_kernel(mix_ref, x_ref, wout_ref, postw_ref, premlpw_ref, wup_ref, wdown_ref, postmlpw_ref, out_ref):
    tm = x_ref.shape[0]
    piece = POST_PIECE if tm % POST_PIECE == 0 else tm
    halves = [slice(r, r + piece) for r in range(0, tm, piece)]
    m = [_dot(mix_ref[r, :], wout_ref[...]) for r in halves]
    x1 = [x_ref[r, :] + _rms(mi, postw_ref[...]) for r, mi in zip(halves, m)]
    h2 = [_bf(_rms(xi, premlpw_ref[...])) for xi in x1]
    acc = [None] * len(halves)
    for c in range(D_FF // D_MODEL):
        for i in range(len(halves)):
            f = _dot(h2[i], wup_ref[:, c * D_MODEL:(c + 1) * D_MODEL])
            f = jnp.square(jnp.maximum(f, 0.0))
            part = _dot(_bf(f), wdown_ref[c * D_MODEL:(c + 1) * D_MODEL, :])
            acc[i] = part if acc[i] is None else acc[i] + part
    for r, xi, ai in zip(halves, x1, acc):
        out_ref[r, :] = xi + _rms(ai, postmlpw_ref[...])


def _const_spec(shape, single=False):
    index_map = lambda *_: (0,) * len(shape)
    if single:
        return pl.BlockSpec(shape, index_map, pipeline_mode=pl.Buffered(1))
    return pl.BlockSpec(shape, index_map)


def _retention_tables(group):
    log_gamma = np.log(1.0 - 2.0 ** (-5.0 - np.arange(HEADS, dtype=np.float64)))
    t = np.arange(ROWS) % group
    same = (np.arange(ROWS)[:, None] // group) == (np.arange(ROWS)[None, :] // group)
    causal = same & (t[:, None] >= t[None, :])
    dmat = np.where(causal[None], np.exp((t[:, None] - t[None, :])[None] * log_gamma[:, None, None]), 0.0)
    ecum = np.exp((t[:, None] + 1.0) * log_gamma[None, :])
    tail = np.exp((group - 1.0 - t[:, None]) * log_gamma[None, :])
    rtab = np.stack([np.repeat(ecum, HEAD_DIM, axis=1), np.repeat(tail, HEAD_DIM, axis=1)])
    dtot = tuple(float(np.float32(v)) for v in np.exp(group * log_gamma))
    return jnp.asarray(dmat, F32), jnp.asarray(rtab, F32), dtot


def _rope_tables(length, offset):
    inv_freq = ROPE_BASE ** (-np.arange(0, HEAD_DIM, 2, dtype=np.float64) / HEAD_DIM)
    ang = (np.arange(length, dtype=np.float64) + offset)[:, None] * inv_freq[None, :]
    cos, sin = np.cos(ang), np.sin(ang)
    return (jnp.asarray(np.concatenate([cos, cos], axis=-1), F32),
            jnp.asarray(np.concatenate([-sin, sin], axis=-1), F32))


def _mixer(x, params, *, prompt, states=None):
    b, l, _ = x.shape
    t = b * l
    if prompt:
        group, lb = ROWS, b * ROWS
        assert l % ROWS == 0
        cos, sin = _rope_tables(l, 0)
        grid = (l // ROWS,)
        x_in = x
        x_spec = pl.BlockSpec((b, ROWS, D_MODEL), lambda j: (0, j, 0))
        pos_spec = pl.BlockSpec((ROWS, HEAD_DIM), lambda j: (j, 0))
    else:
        group = l
        lb = min(2 * ROWS, t)
        assert ROWS % group == 0 and group == SUBLANES and t % lb == 0 and lb % ROWS == 0
        cos, sin = _rope_tables(l, PAST_LEN)
        cos, sin = jnp.tile(cos, (lb // l, 1)), jnp.tile(sin, (lb // l, 1))
        grid = (t // lb,)
        row_map = lambda i: (i, 0)
        x_in = x.reshape(t, D_MODEL)
        x_spec = pl.BlockSpec((lb, D_MODEL), row_map)
        pos_spec = pl.BlockSpec((lb, HEAD_DIM), lambda i: (0, 0))
    nseq = lb // group
    dmat, rtab, dtot = _retention_tables(group)
    cfg = dict(lb=lb, group=group, prompt=prompt, ret_dtot=dtot)

    in_specs = [
        x_spec,
        _const_spec((1, D_MODEL)),
        _const_spec((D_MODEL, IN_W), single=True),
        _const_spec((D_MODEL, LANES), single=True),
        pos_spec,
        pos_spec,
        _const_spec((HEADS, ROWS, ROWS)),
        _const_spec((2, ROWS, GROUP_W)),
        _const_spec((CONV_W, CONV_CH)),
        _const_spec((SUBLANES, LANES)),
        _const_spec((1, 2 * GROUP_W)),
    ]
    args = [x_in, params["pre_w"], params["w_in"], params["w_ab"], cos, sin, dmat, rtab,
            params["conv_w"], params["pvec"], params["normw"]]
    state_shape = (HEADS, HEAD_DIM, HEAD_DIM)
    if prompt:
        out_specs = [
            pl.BlockSpec((b, ROWS, 2 * GROUP_W), lambda j: (0, j, 0)),
            _const_spec((b,) + state_shape),
            _const_spec((b,) + state_shape),
            _const_spec((b * SUBLANES, CONV_CH)),
        ]
        out_shape = [
            jax.ShapeDtypeStruct((b, l, 2 * GROUP_W), BF16),
            jax.ShapeDtypeStruct((b,) + state_shape, F32),
            jax.ShapeDtypeStruct((b,) + state_shape, F32),
            jax.ShapeDtypeStruct((b * SUBLANES, CONV_CH), F32),
        ]
    else:
        s_ret, s_gdn, s_conv = states
        st8 = jnp.pad(s_conv.astype(F32), ((0, 0), (SUBLANES - (CONV_W - 1), 0), (0, 0))).reshape(t, CONV_CH)
        in_specs += [
            pl.BlockSpec((nseq,) + state_shape, lambda i: (i, 0, 0, 0)),
            pl.BlockSpec((nseq,) + state_shape, lambda i: (i, 0, 0, 0)),
            pl.BlockSpec((lb, CONV_CH), row_map),
        ]
        args += [s_ret.astype(F32), s_gdn.astype(F32), st8]
        out_specs = [
            pl.BlockSpec((lb, 2 * GROUP_W), row_map),
            pl.BlockSpec((nseq,) + state_shape, lambda i: (i, 0, 0, 0)),
            pl.BlockSpec((nseq,) + state_shape, lambda i: (i, 0, 0, 0)),
            pl.BlockSpec((lb, CONV_CH), row_map),
        ]
        out_shape = [
            jax.ShapeDtypeStruct((t, 2 * GROUP_W), BF16),
            jax.ShapeDtypeStruct((b,) + state_shape, F32),
            jax.ShapeDtypeStruct((b,) + state_shape, F32),
            jax.ShapeDtypeStruct((t, CONV_CH), F32),
        ]
    nmat = lb // ROWS * HEADS
    scratch = [
        pltpu.VMEM((lb, GROUP_W), F32), pltpu.VMEM((lb, GROUP_W), F32),
        pltpu.VMEM((lb, GROUP_W), F32), pltpu.VMEM((lb, GROUP_W), F32), pltpu.VMEM((lb, GROUP_W), F32),
        pltpu.VMEM((lb, LANES), F32), pltpu.VMEM((lb, LANES), F32), pltpu.VMEM((lb, LANES), F32),
        pltpu.VMEM((lb, 2 * GROUP_W), F32),
        pltpu.VMEM((lb, GROUP_W), F32), pltpu.VMEM((lb, GROUP_W), F32),
        pltpu.VMEM((nmat, ROWS, ROWS), F32), pltpu.VMEM((nmat, ROWS, ROWS), F32),
        pltpu.VMEM((lb, GROUP_W), BF16), pltpu.VMEM((lb, 2 * GROUP_W), F32),
    ]
    if prompt:
        scratch += [pltpu.VMEM((b,) + state_shape, F32), pltpu.VMEM((b,) + state_shape, F32),
                    pltpu.VMEM((b, SUBLANES + ROWS, CONV_CH), F32)]

    mix, s_ret_new, s_gdn_new, conv_rows = pl.pallas_call(
        functools.partial(_mixer_kernel, cfg),
        grid=grid,
        in_specs=in_specs,
        out_specs=out_specs,
        out_shape=out_shape,
        scratch_shapes=scratch,
        compiler_params=pltpu.CompilerParams(dimension_semantics=("arbitrary",), vmem_limit_bytes=VMEM_LIMIT),
        name="mixer_prompt" if prompt else "mixer_sample",
    )(*args)
    new_conv = conv_rows.reshape(b, SUBLANES, CONV_CH)[:, SUBLANES - (CONV_W - 1):]
    return mix.reshape(t, 2 * GROUP_W), s_ret_new, s_gdn_new, new_conv


def _post(mix, x, params):
    b, l, _ = x.shape
    t = b * l
    tm = min(POST_ROWS, t)
    assert t % tm == 0
    rows = pl.BlockSpec((tm, D_MODEL), lambda i: (i, 0))
    y = pl.pallas_call(
        _post_kernel,
        grid=(t // tm,),
        in_specs=[rows, rows, _const_spec((2 * GROUP_W, D_MODEL), single=True), _const_spec((1, D_MODEL)),
                  _const_spec((1, D_MODEL)), _const_spec((D_MODEL, D_FF), single=True),
                  _const_spec((D_FF, D_MODEL), single=True), _const_spec((1, D_MODEL))],
        out_specs=rows,
        out_shape=jax.ShapeDtypeStruct((t, D_MODEL), F32),
        compiler_params=pltpu.CompilerParams(dimension_semantics=("arbitrary",), vmem_limit_bytes=VMEM_LIMIT),
        name="post",
    )(mix, x.reshape(t, D_MODEL), params["w_out"], params["post_w"], params["pre_mlp_w"],
      params["w_up"], params["w_down"], params["post_mlp_w"])
    return y.reshape(b, l, D_MODEL)


def _layer_params(l, pre_mix_w, w_in, conv_w, A_log, dt_bias, ret_norm_w, gdn_norm_w, w_out, post_mix_w,
                  pre_mlp_w, w_up, w_down, post_mlp_w):
    w = _bf(w_in[l])
    pvec = jnp.zeros((SUBLANES, LANES), F32)
    pvec = pvec.at[0, :HEADS].set(A_log[l].astype(F32)).at[1, :HEADS].set(dt_bias[l].astype(F32))
    return dict(
        pre_w=pre_mix_w[l].astype(F32)[None],
        w_in=w,
        w_ab=jnp.pad(w[:, MAIN_W:], ((0, 0), (0, LANES - 2 * HEADS))),
        conv_w=conv_w[l].astype(F32),
        pvec=pvec,
        normw=jnp.concatenate([jnp.tile(ret_norm_w[l], HEADS), jnp.tile(gdn_norm_w[l], HEADS)]).astype(F32)[None],
        w_out=_bf(w_out[l]),
        post_w=post_mix_w[l].astype(F32)[None],
        pre_mlp_w=pre_mlp_w[l].astype(F32)[None],
        w_up=_bf(w_up[l]),
        w_down=_bf(w_down[l]),
        post_mlp_w=post_mlp_w[l].astype(F32)[None],
    )


def kernel(x_prompt, x_sample, state_ret, state_gdn, state_conv, pre_mix_w, w_in, conv_w, A_log, dt_bias,
           ret_norm_w, gdn_norm_w, w_out, post_mix_w, pre_mlp_w, w_up, w_down, post_mlp_w):
    depth = w_in.shape[0]
    yp, ys = x_prompt, x_sample
    outs = [[] for _ in range(6)]
    for l in range(depth):
        params = _layer_params(l, pre_mix_w, w_in, conv_w, A_log, dt_bias, ret_norm_w, gdn_norm_w, w_out,
                               post_mix_w, pre_mlp_w, w_up, w_down, post_mlp_w)
        mix_p, rp, gp, cp = _mixer(yp, params, prompt=True)
        yp = _post(mix_p, yp, params)
        mix_s, rs, gs, cs = _mixer(ys, params, prompt=False,
                                   states=(state_ret[l], state_gdn[l], state_conv[l]))
        ys = _post(mix_s, ys, params)
        for dst, val, like in zip(outs, (rp, gp, cp, rs, gs, cs),
                                  (state_ret, state_gdn, state_conv) * 2):
            dst.append(val.astype(like.dtype))
    return (yp, ys) + tuple(jnp.stack(o) for o in outs)
```
